```python
import math
import jax
import jax.numpy as jnp
from jax import lax
import numpy as np

D_MODEL = 1024
BATCH = 4
SEQ = 4096
DEPTH = 2

N_MEM = 256
EPS = 1e-6
HEAD_DIM = 128
RET_HEADS = D_MODEL // (2 * HEAD_DIM)
RET_DK = HEAD_DIM
RET_DV = HEAD_DIM
RET_CHUNK = 128
ROPE_BASE = 10000.0
HG_HEADS = D_MODEL // (2 * HEAD_DIM)
HG_DK = HEAD_DIM
HG_DV = HEAD_DIM
HG_CHUNK = 64
F_FLOOR = 1e-6
DSA_HEADS = D_MODEL // HEAD_DIM
DSA_HD = HEAD_DIM
DSA_BRANCHES = ((128, 1), (512, 4), (2048, 16))
DSA_BLOCK = 128
REL_BUCKETS = 32
REL_MAX_DIST = 2048
XA_HEADS = 4
XA_HD = D_MODEL // XA_HEADS
D_FF = ((8 * D_MODEL // 3 + 127) // 128) * 128
CONV_W = 3

N_EVEN = (DEPTH + 1) // 2
N_ODD = DEPTH // 2
EV_SIZES = (RET_HEADS * RET_DK, RET_HEADS * RET_DK, RET_HEADS * RET_DV, RET_HEADS * RET_DV,
            HG_HEADS * HG_DK, HG_HEADS * HG_DK, HG_HEADS * HG_DV, HG_HEADS * HG_DV)
EV_IN = sum(EV_SIZES)
EV_OUT = RET_HEADS * RET_DV + HG_HEADS * HG_DV
OD_IN = 3 * DSA_HEADS * DSA_HD
OD_OUT = DSA_HEADS * DSA_HD

kernel_name = 'hybrid_retention_hgrn2_dilated_attention_block'


def _rms(x, g):
    xf = x.astype(jnp.float32)
    y = xf * lax.rsqrt(jnp.mean(xf * xf, axis=-1, keepdims=True) + EPS)
    return (y * g.astype(jnp.float32)).astype(x.dtype)


def _head_layer_norm(x, g):
    mu = jnp.mean(x, axis=-1, keepdims=True)
    xc = x - mu
    var = jnp.mean(xc * xc, axis=-1, keepdims=True)
    return xc * lax.rsqrt(var + EPS) * g.astype(jnp.float32)[None, :, None, :]


def _head_rms_norm(x, g):
    y = x * lax.rsqrt(jnp.mean(x * x, axis=-1, keepdims=True) + EPS)
    return y * g.astype(jnp.float32)[None, :, None, :]


def _split_heads(t, n_heads):
    b, s, _ = t.shape
    return t.reshape(b, s, n_heads, -1).transpose(0, 2, 1, 3)


def _merge_heads(t):
    b, h, s, d = t.shape
    return t.transpose(0, 2, 1, 3).reshape(b, s, h * d)


def _split_cols(t, sizes):
    parts, off = [], 0
    for n in sizes:
        parts.append(t[..., off:off + n])
        off += n
    return parts


def _rotary(x, positions):
    half = x.shape[-1] // 2
    inv = 1.0 / (ROPE_BASE ** (jnp.arange(half, dtype=jnp.float32) / half))
    ang = positions.astype(jnp.float32)[:, None] * inv[None, :]
    cos, sin = jnp.cos(ang), jnp.sin(ang)
    xf = x.astype(jnp.float32)
    x1, x2 = xf[..., :half], xf[..., half:]
    return jnp.concatenate([x1 * cos - x2 * sin, x1 * sin + x2 * cos], axis=-1)


def _retention_chunkwise(q, k, v):
    b, h, s, dk = q.shape
    dv = v.shape[-1]
    c = RET_CHUNK
    nc = s // c
    log_gamma = jnp.log(1.0 - jnp.exp2(-5.0 - jnp.arange(h, dtype=jnp.float32)))
    idx = jnp.arange(c, dtype=jnp.float32)
    diff = idx[:, None] - idx[None, :]
    decay = jnp.where(diff >= 0, jnp.exp(log_gamma[:, None, None] * jnp.maximum(diff, 0.0)), 0.0)
    xi = jnp.exp(log_gamma[:, None] * (idx + 1.0))
    zeta = jnp.exp(log_gamma[:, None] * (c - 1.0 - idx))
    chunk_decay = jnp.exp(log_gamma * c)
    qc = q.astype(jnp.float32).reshape(b, h, nc, c, dk)
    kc = k.astype(jnp.float32).reshape(b, h, nc, c, dk)
    vc = v.astype(jnp.float32).reshape(b, h, nc, c, dv)
    scores = jnp.einsum('bhnid,bhnjd->bhnij', qc, kc) * decay[None, :, None]
    inner = jnp.einsum('bhnij,bhnjv->bhniv', scores, vc)
    chunk_kv = jnp.einsum('bhnjd,bhnjv->bhndv', kc * zeta[None, :, None, :, None], vc)

    def step(state, kv):
        return kv + chunk_decay[None, :, None, None] * state, state

    _, prev = lax.scan(step, jnp.zeros((b, h, dk, dv), jnp.float32), jnp.moveaxis(chunk_kv, 2, 0))
    prev = jnp.moveaxis(prev, 0, 2)
    cross = jnp.einsum('bhnid,bhndv->bhniv', qc, prev) * xi[None, :, None, :, None]
    return (inner + cross).reshape(b, h, s, dv)


def _hgrn2_chunkwise(q, f_logit, i, lb):
    b, h, s, dk = q.shape
    dv = i.shape[-1]
    c = HG_CHUNK
    nc = s // c
    lbb = lb.astype(jnp.float32)[None, :, None, :]
    f = lbb + (1.0 - lbb) * jax.nn.sigmoid(f_logit.astype(jnp.float32))
    log_f = jnp.log(jnp.maximum(f, F_FLOOR))
    key = 1.0 - f

    def chunks(t):
        return jnp.moveaxis(t.reshape(b, h, nc, c, t.shape[-1]), 2, 0)

    causal = jnp.tril(jnp.ones((c, c), dtype=bool))[None, None, :, :, None]

    def step(state, xs):
        qn, kn, vn, lfn = xs
        cum = jnp.cumsum(lfn, axis=2)
        rel = cum[:, :, :, None, :] - cum[:, :, None, :, :]
        w = jnp.exp(jnp.where(causal, rel, -jnp.inf))
        scores = jnp.sum(qn[:, :, :, None, :] * kn[:, :, None, :, :] * w, axis=-1)
        out = (jnp.einsum('bhts,bhsv->bhtv', scores, vn)
               + jnp.einsum('bhtc,bhcv->bhtv', qn * jnp.exp(cum), state))
        last = cum[:, :, -1:, :]
        new_state = (jnp.exp(last[:, :, 0, :])[..., None] * state
                     + jnp.einsum('bhsc,bhsv->bhcv', kn * jnp.exp(last - cum), vn))
        return new_state, out

    _, outs = lax.scan(step, jnp.zeros((b, h, dk, dv), jnp.float32),
                       (chunks(q.astype(jnp.float32)), chunks(key), chunks(i.astype(jnp.float32)), chunks(log_f)))
    return jnp.moveaxis(outs, 0, 2).reshape(b, h, s, dv)


def _t5_bucket(dist):
    exact = REL_BUCKETS // 2
    d = jnp.maximum(dist, 0)
    log_ratio = jnp.log(jnp.maximum(d, 1).astype(jnp.float32) / exact) / math.log(REL_MAX_DIST / exact)
    large = jnp.minimum(exact + (log_ratio * (REL_BUCKETS - exact)).astype(jnp.int32), REL_BUCKETS - 1)
    return jnp.where(d < exact, d, large)


def _dilated_branch(q, k, v, rel_bias, window, dil):
    b, h, s, hd = q.shape
    blk = DSA_BLOCK
    n_back = window // dil
    length = s // dil
    nb = -(-length // blk)
    padded = nb * blk

    def to_blocks(t):
        t = t.reshape(b, h, length, dil, hd).transpose(0, 1, 3, 2, 4)
        t = jnp.pad(t, ((0, 0), (0, 0), (0, 0), (0, padded - length), (0, 0)))
        return t.reshape(b, h, dil, nb, blk, hd)

    def band(t):
        tp = jnp.pad(t, ((0, 0), (0, 0), (0, 0), (1, 0), (0, 0), (0, 0)))
        return jnp.concatenate([tp[:, :, :, :-1], tp[:, :, :, 1:]], axis=4)

    qb = to_blocks(q)
    kb = band(to_blocks(k))
    vb = band(to_blocks(v))
    logits = jnp.einsum('bhrnqd,bhrnkd->bhrnqk', qb, kb).astype(jnp.float32) * (hd ** -0.5)
    qi = jnp.arange(blk)
    ki = jnp.arange(2 * blk)
    delta = qi[:, None] + blk - ki[None, :]
    key_idx = jnp.arange(nb)[:, None] * blk - blk + ki[None, :]
    valid = ((delta >= 0) & (delta <= n_back))[None] & (key_idx >= 0)[:, None, :]
    bias = rel_bias[_t5_bucket(delta * dil)].transpose(2, 0, 1).astype(jnp.float32)
    logits = jnp.where(valid[None, None, None], logits + bias[None, :, None, None], -jnp.inf)
    m = jnp.max(logits, axis=-1)
    p = jnp.exp(logits - m[..., None])
    den = jnp.sum(p, axis=-1)
    num = jnp.einsum('bhrnqk,bhrnkd->bhrnqd', p, vb.astype(jnp.float32))

    def to_seq(t):
        t = t.reshape((b, h, dil, padded) + t.shape[5:])[:, :, :, :length]
        t = jnp.swapaxes(t, 2, 3)
        return t.reshape((b, h, s) + t.shape[4:])

    return to_seq(m), to_seq(den), to_seq(num)


def _dilated_attention(q, k, v, rel_bias):
    ms, dens, nums = [], [], []
    for window, dil in DSA_BRANCHES:
        m, den, num = _dilated_branch(q, k, v, rel_bias, window, dil)
        ms.append(m)
        dens.append(den)
        nums.append(num)
    m_all = jnp.max(jnp.stack(ms, axis=0), axis=0)
    w = [jnp.exp(mi - m_all) for mi in ms]
    den_all = w[0] * dens[0] + w[1] * dens[1] + w[2] * dens[2]
    num_all = w[0][..., None] * nums[0] + w[1][..., None] * nums[1] + w[2][..., None] * nums[2]
    return num_all / den_all[..., None]


def _even_mixer(h, w_in, ret_norm_g, hg_norm_g, lb, w_out):
    b, s, _ = h.shape
    rq, rk, rv, rg, gq, gf, gi, gg = _split_cols(h @ w_in, EV_SIZES)
    pos = jnp.arange(s)
    rq = _rotary(_split_heads(rq, RET_HEADS), pos)
    rk = _rotary(_split_heads(rk, RET_HEADS), pos) * (RET_DK ** -0.5)
    ret = _retention_chunkwise(rq, rk, _split_heads(rv, RET_HEADS))
    ret = _merge_heads(_head_layer_norm(ret, ret_norm_g)) * jax.nn.silu(rg.astype(jnp.float32))
    hg = _hgrn2_chunkwise(_split_heads(gq, HG_HEADS), _split_heads(gf, HG_HEADS), _split_heads(gi, HG_HEADS), lb)
    hg = _merge_heads(_head_rms_norm(hg, hg_norm_g)) * jax.nn.silu(gg.astype(jnp.float32))
    y = jnp.concatenate([ret, hg], axis=-1).astype(h.dtype)
    return y @ w_out


def _odd_mixer(h, w_in, q_norm_g, k_norm_g, rel_bias, w_out):
    q, k, v = _split_cols(h @ w_in, (OD_OUT, OD_OUT, OD_OUT))
    q = _rms(_split_heads(q, DSA_HEADS), q_norm_g)
    k = _rms(_split_heads(k, DSA_HEADS), k_norm_g)
    o = _dilated_attention(q, k, _split_heads(v, DSA_HEADS), rel_bias)
    return _merge_heads(o).astype(h.dtype) @ w_out


def _memory_cross_attention(h, mem_h, w_q, w_kv, q_norm_g, k_norm_g, w_o):
    q = _rms(_split_heads(h @ w_q, XA_HEADS), q_norm_g)
    mk, mv = _split_cols(mem_h @ w_kv, (XA_HEADS * XA_HD, XA_HEADS * XA_HD))
    mk = _rms(_split_heads(mk, XA_HEADS), k_norm_g)
    mv = _split_heads(mv, XA_HEADS)
    logits = jnp.einsum('bhqd,bhkd->bhqk', q, mk).astype(jnp.float32) * (XA_HD ** -0.5)
    p = jax.nn.softmax(logits, axis=-1)
    o = jnp.einsum('bhqk,bhkd->bhqd', p, mv.astype(jnp.float32))
    return _merge_heads(o).astype(h.dtype) @ w_o


def _causal_dwconv(u, w, bias):
    kw = w.shape[0]
    s = u.shape[1]
    up = jnp.pad(u, ((0, 0), (kw - 1, 0), (0, 0)))
    y = up[:, 0:s] * w[0]
    for j in range(1, kw):
        y = y + up[:, j:j + s] * w[j]
    return y + bias


def _conv_ffn(h, w_in, conv_w, conv_b, w_out):
    gate, up = _split_cols(h @ w_in, (D_FF, D_FF))
    gate = _causal_dwconv(gate, conv_w, conv_b)
    return (jax.nn.gelu(gate) * up) @ w_out


def setup_inputs(seed: int = 0) -> dict:
    key = jax.random.key(seed)
    ks = iter(jax.random.split(key, 32))

    def nrm(shape, scale):
        return jax.random.normal(next(ks), shape, jnp.float32) * scale

    def gain(shape):
        return 1.0 + nrm(shape, 0.05)

    out_scale = 0.5
    return {
        'x': nrm((BATCH, SEQ, D_MODEL), 1.0),
        'mem': nrm((BATCH, N_MEM, D_MODEL), 1.0),
        'mix_norm_g': gain((DEPTH, D_MODEL)),
        'ev_w_in': nrm((N_EVEN, D_MODEL, EV_IN), D_MODEL ** -0.5),
        'ev_ret_norm_g': gain((N_EVEN, RET_HEADS, RET_DV)),
        'ev_hg_norm_g': gain((N_EVEN, HG_HEADS, HG_DV)),
        'hg_lb_logits': nrm((DEPTH + 1, HG_HEADS * HG_DK), 0.5),
        'ev_w_out': nrm((N_EVEN, EV_OUT, D_MODEL), out_scale * EV_OUT ** -0.5),
        'od_w_in': nrm((N_ODD, D_MODEL, OD_IN), D_MODEL ** -0.5),
        'od_q_norm_g': gain((N_ODD, DSA_HD)),
        'od_k_norm_g': gain((N_ODD, DSA_HD)),
        'rel_bias': nrm((REL_BUCKETS, DSA_HEADS), 0.5),
        'od_w_out': nrm((N_ODD, OD_OUT, D_MODEL), out_scale * OD_OUT ** -0.5),
        'xa_norm_g': gain((DEPTH, D_MODEL)),
        'xa_mem_norm_g': gain((DEPTH, D_MODEL)),
        'xa_w_q': nrm((DEPTH, D_MODEL, XA_HEADS * XA_HD), D_MODEL ** -0.5),
        'xa_w_kv': nrm((DEPTH, D_MODEL, 2 * XA_HEADS * XA_HD), D_MODEL ** -0.5),
        'xa_q_norm_g': gain((DEPTH, XA_HD)),
        'xa_k_norm_g': gain((DEPTH, XA_HD)),
        'xa_w_o': nrm((DEPTH, XA_HEADS * XA_HD, D_MODEL), out_scale * (XA_HEADS * XA_HD) ** -0.5),
        'ffn_norm_g': gain((DEPTH, D_MODEL)),
        'ffn_w_in': nrm((DEPTH, D_MODEL, 2 * D_FF), D_MODEL ** -0.5),
        'ffn_conv_w': nrm((DEPTH, CONV_W, D_FF), CONV_W ** -0.5),
        'ffn_conv_b': nrm((DEPTH, D_FF), 0.02),
        'ffn_w_out': nrm((DEPTH, D_FF, D_MODEL), out_scale * D_FF ** -0.5),
    }


def reference(x, mem, mix_norm_g, ev_w_in, ev_ret_norm_g, ev_hg_norm_g, hg_lb_logits, ev_w_out,
              od_w_in, od_q_norm_g, od_k_norm_g, rel_bias, od_w_out,
              xa_norm_g, xa_mem_norm_g, xa_w_q, xa_w_kv, xa_q_norm_g, xa_k_norm_g, xa_w_o,
              ffn_norm_g, ffn_w_in, ffn_conv_w, ffn_conv_b, ffn_w_out):
    lb_all = jnp.cumsum(jax.nn.softmax(hg_lb_logits.astype(jnp.float32), axis=0), axis=0)
    for l in range(DEPTH):
        h = _rms(x, mix_norm_g[l])
        if l % 2 == 0:
            e = l // 2
            lb = lb_all[l].reshape(HG_HEADS, HG_DK)
            x = x + _even_mixer(h, ev_w_in[e], ev_ret_norm_g[e], ev_hg_norm_g[e], lb, ev_w_out[e])
        else:
            o = l // 2
            x = x + _odd_mixer(h, od_w_in[o], od_q_norm_g[o], od_k_norm_g[o], rel_bias, od_w_out[o])
        x = x + _memory_cross_attention(_rms(x, xa_norm_g[l]), _rms(mem, xa_mem_norm_g[l]),
                                        xa_w_q[l], xa_w_kv[l], xa_q_norm_g[l], xa_k_norm_g[l], xa_w_o[l])
        x = x + _conv_ffn(_rms(x, ffn_norm_g[l]), ffn_w_in[l], ffn_conv_w[l], ffn_conv_b[l], ffn_w_out[l])
    return x
```

```python
import functools
import math

import jax
import jax.numpy as jnp
import numpy as np
from jax import lax
from jax.experimental import pallas as pl
from jax.experimental.pallas import tpu as pltpu

F32 = jnp.float32
BF16 = jnp.bfloat16

D_MODEL = 1024
N_MEM = 256
EPS = 1e-6
HEAD_DIM = 128
RET_HEADS = 4
RET_CHUNK = 128
ROPE_BASE = 10000.0
HG_HEADS = 4
HG_CHUNK = 64
F_FLOOR = 1e-6
DSA_HEADS = 8
DSA_BRANCHES = ((128, 1), (512, 4), (2048, 16))
DSA_BLOCK = 128
REL_BUCKETS = 32
REL_MAX_DIST = 2048
XA_HEADS = 4
XA_HD = 256
D_FF = 2816
CONV_W = 3
EV_GROUP = RET_HEADS * HEAD_DIM

VMEM_LIMIT_BYTES = 56 * 1024 * 1024
NEG_INF = float("-inf")

_NT = (((1,), (1,)), ((), ()))


def _params(*sem):
    return pltpu.CompilerParams(dimension_semantics=sem, vmem_limit_bytes=VMEM_LIMIT_BYTES)


def _rms_rows(x, g):
    return x * lax.rsqrt(jnp.mean(x * x, axis=-1, keepdims=True) + EPS) * g


def _dot(a, b):
    return jnp.dot(a, b, preferred_element_type=F32)


def _dot_nt(a, b):
    return lax.dot_general(a, b, _NT, preferred_element_type=F32)


def _rms_proj_kernel(x_ref, g_ref, w_ref, o_ref, *, col_chunk):
    h = _rms_rows(x_ref[...], g_ref[...]).astype(BF16)
    n = o_ref.shape[1]
    for c in range(n // col_chunk):
        sl = slice(c * col_chunk, (c + 1) * col_chunk)
        o_ref[:, sl] = _dot(h, w_ref[:, sl]).astype(o_ref.dtype)


def _rms_proj(x2d, g, w, *, tm, out_dtype=F32, col_chunk=512):
    t, d = x2d.shape
    n = w.shape[1]
    return pl.pallas_call(
        functools.partial(_rms_proj_kernel, col_chunk=col_chunk),
        grid=(t // tm,),
        in_specs=[
            pl.BlockSpec((tm, d), lambda i: (i, 0)),
            pl.BlockSpec((1, d), lambda i: (0, 0)),
            pl.BlockSpec((d, n), lambda i: (0, 0)),
        ],
        out_specs=pl.BlockSpec((tm, n), lambda i: (i, 0)),
        out_shape=jax.ShapeDtypeStruct((t, n), out_dtype),
        compiler_params=_params("parallel"),
        name="rms_proj",
    )(x2d, g.reshape(1, d).astype(F32), w.astype(BF16))


def _out_proj_kernel(*refs, n_y):
    x_ref = refs[0]
    y_refs = refs[1:1 + n_y]
    w_refs = refs[1 + n_y:1 + 2 * n_y]
    o_ref = refs[1 + 2 * n_y]
    acc = x_ref[...]
    for y_ref, w_ref in zip(y_refs, w_refs):
        acc = acc + _dot(y_ref[...], w_ref[...])
    o_ref[...] = acc


def _out_proj(x2d, ys, w, *, tm):
    t, d = x2d.shape
    n_y = len(ys)
    ws, off = [], 0
    for y in ys:
        ws.append(w[off:off + y.shape[1]].astype(BF16))
        off += y.shape[1]
    in_specs = [pl.BlockSpec((tm, d), lambda i: (i, 0))]
    in_specs += [pl.BlockSpec((tm, y.shape[1]), lambda i: (i, 0)) for y in ys]
    in_specs += [pl.BlockSpec(wk.shape, lambda i: (0, 0)) for wk in ws]
    return pl.pallas_call(
        functools.partial(_out_proj_kernel, n_y=n_y),
        grid=(t // tm,),
        in_specs=in_specs,
        out_specs=pl.BlockSpec((tm, d), lambda i: (i, 0)),
        out_shape=jax.ShapeDtypeStruct((t, d), F32),
        compiler_params=_params("parallel"),
        name="out_proj",
    )(x2d, *ys, *ws)


def _retention_consts(seq):
    h = jnp.arange(RET_HEADS, dtype=F32)
    log_gamma = jnp.log(1.0 - jnp.exp2(-5.0 - h))
    c = RET_CHUNK
    idx = jnp.arange(c, dtype=F32)
    diff = idx[:, None] - idx[None, :]
    decay = jnp.where(diff >= 0, jnp.exp(log_gamma[:, None, None] * jnp.maximum(diff, 0.0)), 0.0)
    xi = jnp.exp(log_gamma[:, None] * (idx + 1.0))
    zeta = jnp.exp(log_gamma[:, None] * (c - 1.0 - idx))
    chunk_decay = jnp.exp(log_gamma * c)
    ones = jnp.ones((RET_HEADS, c, HEAD_DIM), F32)
    half = HEAD_DIM // 2
    inv = 1.0 / (ROPE_BASE ** (jnp.arange(half, dtype=F32) / half))
    ang = jnp.arange(seq, dtype=F32)[:, None] * inv[None, :]
    cos, sin = jnp.cos(ang), jnp.sin(ang)
    return dict(
        decay=decay,
        xi=xi[:, :, None] * ones,
        zeta=zeta[:, :, None] * ones,
        chunk_decay=chunk_decay[:, None, None] * jnp.ones((RET_HEADS, 8, HEAD_DIM), F32),
        cos=jnp.concatenate([cos, cos], axis=-1),
        sin=jnp.concatenate([-sin, sin], axis=-1),
    )


def _retention_kernel(q_ref, k_ref, v_ref, g_ref, cos_ref, sin_ref, decay_ref, xi_ref, zeta_ref,
                      cd_ref, ng_ref, o_ref, state_ref):
    @pl.when(pl.program_id(2) == 0)
    def _():
        state_ref[...] = jnp.zeros_like(state_ref)

    c = RET_CHUNK
    half = HEAD_DIM // 2
    decay = decay_ref[...]
    xi = xi_ref[...]
    zeta = zeta_ref[...]
    cd = cd_ref[0:1, :]
    ng = ng_ref[...]
    for ci in range(q_ref.shape[0] // c):
        sl = slice(ci * c, (ci + 1) * c)
        cos = cos_ref[sl, :]
        sin = sin_ref[sl, :]
        q = q_ref[sl, :]
        k = k_ref[sl, :]
        q = q * cos + pltpu.roll(q, half, axis=1) * sin
        k = (k * cos + pltpu.roll(k, half, axis=1) * sin) * (HEAD_DIM ** -0.5)
        qb = q.astype(BF16)
        vb = v_ref[sl, :].astype(BF16)
        scores = _dot_nt(qb, k.astype(BF16)) * decay
        inner = _dot(scores.astype(BF16), vb)
        state = state_ref[...]
        cross = _dot(qb, state.astype(BF16)) * xi
        kv = _dot((k * zeta).T.astype(BF16), vb)
        state_ref[...] = kv + cd * state
        o = inner + cross
        mu = jnp.mean(o, axis=-1, keepdims=True)
        oc = o - mu
        var = jnp.mean(oc * oc, axis=-1, keepdims=True)
        y = oc * lax.rsqrt(var + EPS) * ng
        o_ref[sl, :] = (y * jax.nn.silu(g_ref[sl, :])).astype(o_ref.dtype)


def _retention(proj, norm_g, *, batch, seq, tm):
    consts = _retention_consts(seq)
    nt = seq // tm
    hh = RET_HEADS

    def col(group):
        return pl.BlockSpec((tm, HEAD_DIM), lambda b, h, i: (b * nt + i, group * hh + h))

    def per_head(rows):
        return pl.BlockSpec((None, rows, HEAD_DIM), lambda b, h, i: (h, 0, 0))

    pos = pl.BlockSpec((tm, HEAD_DIM), lambda b, h, i: (i, 0))
    return pl.pallas_call(
        _retention_kernel,
        grid=(batch, hh, nt),
        in_specs=[col(0), col(1), col(2), col(3), pos, pos,
                  per_head(RET_CHUNK), per_head(RET_CHUNK), per_head(RET_CHUNK), per_head(8), per_head(1)],
        out_specs=pl.BlockSpec((tm, HEAD_DIM), lambda b, h, i: (b * nt + i, h)),
        out_shape=jax.ShapeDtypeStruct((batch * seq, EV_GROUP), BF16),
        scratch_shapes=[pltpu.VMEM((HEAD_DIM, HEAD_DIM), F32)],
        compiler_params=_params("parallel", "parallel", "arbitrary"),
        name="retention",
    )(proj, proj, proj, proj, consts["cos"], consts["sin"], consts["decay"], consts["xi"],
      consts["zeta"], consts["chunk_decay"], norm_g.astype(F32).reshape(hh, 1, HEAD_DIM))


HG_LEVELS = (64, 32, 16, 8)


def _hgrn_consts():
    c = HG_CHUNK
    t = np.arange(c)[:, None]
    s = np.arange(c)[None, :]
    tril = (s <= t).astype(np.float32)
    mats, masks = [tril], []
    for lvl, bs in enumerate(HG_LEVELS):
        mid = (t // bs) * bs + bs // 2 - 1
        mats.append(tril - (s <= mid).astype(np.float32))
        same = (t // bs) == (s // bs)
        if lvl < len(HG_LEVELS) - 1:
            masks.append(same & (t % bs >= bs // 2) & (s % bs < bs // 2))
        else:
            masks.append(same & (s <= t))
    return jnp.asarray(np.concatenate(mats, axis=0)), jnp.asarray(np.stack(masks).astype(np.float32))


def _hgrn_kernel(q_ref, f_ref, i_ref, g_ref, lbl_ref, cum_w_ref, mask_ref, ng_ref, o_ref, state_ref, *,
                 layer):
    @pl.when(pl.program_id(2) == 0)
    def _():
        state_ref[...] = jnp.zeros_like(state_ref)

    c = HG_CHUNK
    logits = lbl_ref[...]
    e = jnp.exp(logits - jnp.max(logits, axis=0, keepdims=True))
    lb = jnp.sum(e[:layer + 1, :], axis=0, keepdims=True) / jnp.sum(e, axis=0, keepdims=True)
    cum_w = cum_w_ref[...]
    masks = [mask_ref[l] > 0.5 for l in range(len(HG_LEVELS))]
    ng = ng_ref[...]

    def chunk(ci, carry):
        sl = pl.ds(pl.multiple_of(ci * c, c), c)
        f = lb + (1.0 - lb) * jax.nn.sigmoid(f_ref[sl, :])
        log_f = jnp.log(jnp.maximum(f, F_FLOOR))
        key = 1.0 - f
        q = q_ref[sl, :]
        vb = i_ref[sl, :]
        sums = jnp.dot(cum_w, log_f, preferred_element_type=F32, precision=lax.Precision.HIGHEST)
        cum = sums[0:c, :]
        scores = jnp.zeros((c, c), F32)
        for lvl in range(len(HG_LEVELS)):
            a = sums[(lvl + 1) * c:(lvl + 2) * c, :]
            if lvl < len(HG_LEVELS) - 1:
                qa = q * jnp.exp(jnp.minimum(a, 0.0))
                ka = key * jnp.exp(jnp.minimum(-a, 0.0))
            else:
                qa = q * jnp.exp(a)
                ka = key * jnp.exp(-a)
            scores = jnp.where(masks[lvl], _dot_nt(qa.astype(BF16), ka.astype(BF16)), scores)
        vb16 = vb.astype(BF16)
        state_t = state_ref[...]
        out = _dot(scores.astype(BF16), vb16) + _dot_nt((q * jnp.exp(cum)).astype(BF16), state_t.astype(BF16))
        last = cum[c - 1:c, :]
        kd = (key * jnp.exp(last - cum)).astype(BF16)
        state_ref[...] = jnp.exp(last) * state_t + _dot(vb.T.astype(BF16), kd)
        y = out * lax.rsqrt(jnp.mean(out * out, axis=-1, keepdims=True) + EPS) * ng
        o_ref[sl, :] = (y * jax.nn.silu(g_ref[sl, :])).astype(o_ref.dtype)
        return carry

    lax.fori_loop(0, q_ref.shape[0] // c, chunk, 0)


def _hgrn(proj, lb_logits, norm_g, *, layer, batch, seq, tm):
    cum_w, masks = _hgrn_consts()
    nt = seq // tm
    hh = HG_HEADS
    slots = lb_logits.shape[0]
    lbl = lb_logits.astype(F32).reshape(slots, hh, HEAD_DIM).transpose(1, 0, 2)

    def col(group):
        return pl.BlockSpec((tm, HEAD_DIM), lambda b, h, i: (b * nt + i, group * hh + h))

    return pl.pallas_call(
        functools.partial(_hgrn_kernel, layer=layer),
        grid=(batch, hh, nt),
        in_specs=[col(4), col(5), col(6), col(7),
                  pl.BlockSpec((None, slots, HEAD_DIM), lambda b, h, i: (h, 0, 0)),
                  pl.BlockSpec(cum_w.shape, lambda b, h, i: (0, 0)),
                  pl.BlockSpec(masks.shape, lambda b, h, i: (0, 0, 0)),
                  pl.BlockSpec((None, 1, HEAD_DIM), lambda b, h, i: (h, 0, 0))],
        out_specs=pl.BlockSpec((tm, HEAD_DIM), lambda b, h, i: (b * nt + i, h)),
        out_shape=jax.ShapeDtypeStruct((batch * seq, EV_GROUP), BF16),
        scratch_shapes=[pltpu.VMEM((HEAD_DIM, HEAD_DIM), F32)],
        compiler_params=_params("parallel", "parallel", "arbitrary"),
        name="hgrn2",
    )(proj, proj, proj, proj, lbl, cum_w, masks, norm_g.astype(F32).reshape(hh, 1, HEAD_DIM))


def _t5_bucket(dist):
    exact = REL_BUCKETS // 2
    d = jnp.maximum(dist, 0)
    log_ratio = jnp.log(jnp.maximum(d, 1).astype(F32) / exact) / math.log(REL_MAX_DIST / exact)
    large = jnp.minimum(exact + (log_ratio * (REL_BUCKETS - exact)).astype(jnp.int32), REL_BUCKETS - 1)
    return jnp.where(d < exact, d, large)


def _dsa_bucket_table():
    blk = DSA_BLOCK
    qi = jnp.arange(blk)[:, None]
    ki = jnp.arange(2 * blk)[None, :]
    delta = qi + blk - ki
    tabs = []
    for window, dil in DSA_BRANCHES:
        n_back = window // dil
        valid = (delta >= 0) & (delta <= n_back)
        tabs.append(jnp.where(valid, _t5_bucket(delta * dil), -1).astype(jnp.int32))
    return jnp.stack(tabs)


def _dsa_kernel(rb_ref, q_ref, k_ref, v_ref, qg_ref, kg_ref, bucket_ref, o_ref,
                qn_ref, kp_ref, vp_ref, m_ref, den_ref, num_ref, bias_ref, *, seq, pad):
    blk = DSA_BLOCK
    head = pl.program_id(1)
    qn_ref[...] = _rms_rows(q_ref[...], qg_ref[...])
    zeros = jnp.zeros((pad, HEAD_DIM), F32)
    kp_ref[0:pad, :] = zeros
    vp_ref[0:pad, :] = zeros
    kp_ref[pad:pad + seq, :] = _rms_rows(k_ref[...], kg_ref[...])
    vp_ref[pad:pad + seq, :] = v_ref[...]

    for br in range(len(DSA_BRANCHES)):
        bucket = bucket_ref[br]
        bias = jnp.full(bucket.shape, NEG_INF, F32)
        for b in range(REL_BUCKETS):
            bias = jnp.where(bucket == b, rb_ref[b, head], bias)
        bias_ref[br] = bias

    first_block_keys = lax.broadcasted_iota(jnp.int32, (blk, 2 * blk), 1) >= blk
    scale = HEAD_DIM ** -0.5

    for br, (window, dil) in enumerate(DSA_BRANCHES):
        nb = seq // dil // blk
        assert dil * blk <= pad

        def block(it, carry, br=br, dil=dil, nb=nb):
            r = it // nb
            n = it % nb
            q0 = r + n * (dil * blk)
            qb = qn_ref[pl.ds(q0, blk, stride=dil), :].astype(BF16)
            k0 = pad + q0 - dil * blk
            kb = kp_ref[pl.ds(k0, 2 * blk, stride=dil), :].astype(BF16)
            vb = vp_ref[pl.ds(k0, 2 * blk, stride=dil), :].astype(BF16)
            logits = _dot_nt(qb, kb) * scale + bias_ref[br]
            logits = jnp.where(jnp.logical_or(n > 0, first_block_keys), logits, NEG_INF)
            m = jnp.max(logits, axis=-1, keepdims=True)
            p = jnp.exp(logits - m)
            den = jnp.sum(p, axis=-1, keepdims=True)
            num = _dot(p.astype(BF16), vb)
            rows = pl.ds(q0, blk, stride=dil)
            m_b = jnp.broadcast_to(m, (blk, HEAD_DIM))
            den_b = jnp.broadcast_to(den, (blk, HEAD_DIM))
            if br == 0:
                m_ref[rows, :] = m_b
                den_ref[rows, :] = den_b
                num_ref[rows, :] = num
            else:
                m_old = m_ref[rows, :]
                m_new = jnp.maximum(m_old, m_b)
                w_old = jnp.exp(m_old - m_new)
                w_new = jnp.exp(m_b - m_new)
                m_ref[rows, :] = m_new
                den_ref[rows, :] = w_old * den_ref[rows, :] + w_new * den_b
                num_ref[rows, :] = w_old * num_ref[rows, :] + w_new * num
            return carry

        lax.fori_loop(0, dil * nb, block, 0)

    o_ref[...] = (num_ref[...] / den_ref[...]).astype(o_ref.dtype)


def _dilated_attention(qkv, q_norm_g, k_norm_g, rel_bias, *, batch, seq):
    hh = DSA_HEADS
    pad = max(dil for _, dil in DSA_BRANCHES) * DSA_BLOCK
    bucket = _dsa_bucket_table()

    def col(group):
        return pl.BlockSpec((seq, HEAD_DIM), lambda b, h: (b, group * hh + h))

    gain = pl.BlockSpec((1, HEAD_DIM), lambda b, h: (0, 0))
    return pl.pallas_call(
        functools.partial(_dsa_kernel, seq=seq, pad=pad),
        grid=(batch, hh),
        in_specs=[pl.BlockSpec(memory_space=pltpu.SMEM),
                  col(0), col(1), col(2), gain, gain,
                  pl.BlockSpec(bucket.shape, lambda b, h: (0, 0, 0))],
        out_specs=pl.BlockSpec((seq, HEAD_DIM), lambda b, h: (b, h)),
        out_shape=jax.ShapeDtypeStruct((batch * seq, hh * HEAD_DIM), BF16),
        scratch_shapes=[pltpu.VMEM((seq, HEAD_DIM), F32),
                        pltpu.VMEM((pad + seq, HEAD_DIM), F32),
                        pltpu.VMEM((pad + seq, HEAD_DIM), F32),
                        pltpu.VMEM((seq, HEAD_DIM), F32),
                        pltpu.VMEM((seq, HEAD_DIM), F32),
                        pltpu.VMEM((seq, HEAD_DIM), F32),
                        pltpu.VMEM((len(DSA_BRANCHES), DSA_BLOCK, 2 * DSA_BLOCK), F32)],
        compiler_params=_params("parallel", "parallel"),
        name="dilated_attention",
    )(rel_bias.astype(F32), qkv, qkv, qkv, q_norm_g.astype(F32).reshape(1, HEAD_DIM),
      k_norm_g.astype(F32).reshape(1, HEAD_DIM), bucket)


def _mem_kv_kernel(mem_ref, g_ref, w_ref, kg_ref, k_ref, v_ref):
    h = _rms_rows(mem_ref[...], g_ref[...]).astype(BF16)
    d = XA_HEADS * XA_HD
    kg = kg_ref[...]
    for hd in range(XA_HEADS):
        sl = slice(hd * XA_HD, (hd + 1) * XA_HD)
        k_ref[:, sl] = _rms_rows(_dot(h, w_ref[:, sl]), kg).astype(k_ref.dtype)
    v_ref[...] = _dot(h, w_ref[:, d:2 * d]).astype(v_ref.dtype)


def _mem_kv(mem2d, g, w_kv, k_norm_g, *, batch):
    d = XA_HEADS * XA_HD
    return pl.pallas_call(
        _mem_kv_kernel,
        grid=(batch,),
        in_specs=[pl.BlockSpec((N_MEM, D_MODEL), lambda b: (b, 0)),
                  pl.BlockSpec((1, D_MODEL), lambda b: (0, 0)),
                  pl.BlockSpec((D_MODEL, 2 * d), lambda b: (0, 0)),
                  pl.BlockSpec((1, XA_HD), lambda b: (0, 0))],
        out_specs=[pl.BlockSpec((N_MEM, d), lambda b: (b, 0)),
                   pl.BlockSpec((N_MEM, d), lambda b: (b, 0))],
        out_shape=[jax.ShapeDtypeStruct((batch * N_MEM, d), BF16),
                   jax.ShapeDtypeStruct((batch * N_MEM, d), BF16)],
        compiler_params=_params("parallel"),
        name="mem_kv",
    )(mem2d, g.astype(F32).reshape(1, D_MODEL), w_kv.astype(BF16), k_norm_g.astype(F32).reshape(1, XA_HD))


def _xattn_kernel(x_ref, g_ref, wq_ref, qg_ref, mk_ref, mv_ref, wo_ref, o_ref, att_ref):
    x = x_ref[...]
    h = _rms_rows(x, g_ref[...]).astype(BF16)
    qg = qg_ref[...]
    for hd in range(XA_HEADS):
        sl = slice(hd * XA_HD, (hd + 1) * XA_HD)
        q = _rms_rows(_dot(h, wq_ref[:, sl]), qg).astype(BF16)
        logits = _dot_nt(q, mk_ref[:, sl]) * (XA_HD ** -0.5)
        p = jnp.exp(logits - jnp.max(logits, axis=-1, keepdims=True))
        p = p / jnp.sum(p, axis=-1, keepdims=True)
        att_ref[:, sl] = _dot(p.astype(BF16), mv_ref[:, sl]).astype(att_ref.dtype)
    o_ref[...] = x + _dot(att_ref[...], wo_ref[...])


def _xattn(x2d, g, w_q, q_norm_g, mk, mv, w_o, *, seq, tm):
    t, d = x2d.shape
    nt = seq // tm
    const = lambda shape: pl.BlockSpec(shape, lambda i: (0, 0))
    return pl.pallas_call(
        _xattn_kernel,
        grid=(t // tm,),
        in_specs=[pl.BlockSpec((tm, d), lambda i: (i, 0)),
                  const((1, d)), const((d, d)), const((1, XA_HD)),
                  pl.BlockSpec((N_MEM, d), lambda i: (i // nt, 0)),
                  pl.BlockSpec((N_MEM, d), lambda i: (i // nt, 0)),
                  const((d, d))],
        out_specs=pl.BlockSpec((tm, d), lambda i: (i, 0)),
        out_shape=jax.ShapeDtypeStruct((t, d), F32),
        scratch_shapes=[pltpu.VMEM((tm, d), BF16)],
        compiler_params=_params("parallel"),
        name="xattn",
    )(x2d, g.astype(F32).reshape(1, d), w_q.astype(BF16), q_norm_g.astype(F32).reshape(1, XA_HD),
      mk, mv, w_o.astype(BF16))


def _ffn_kernel(x_ref, g_ref, win_ref, cw_ref, cb_ref, wout_ref, o_ref, carry_ref, *, nt, fc):
    tm = x_ref.shape[0]
    keep = CONV_W - 1

    @pl.when(pl.program_id(0) % nt == 0)
    def _():
        carry_ref[...] = jnp.zeros_like(carry_ref)

    x = x_ref[...]
    h = _rms_rows(x, g_ref[...]).astype(BF16)
    row = lax.broadcasted_iota(jnp.int32, (tm, fc), 0)
    acc = x
    for c in range(D_FF // fc):
        sl = slice(c * fc, (c + 1) * fc)
        gate = _dot(h, win_ref[:, sl])
        up = _dot(h, win_ref[:, D_FF + c * fc:D_FF + (c + 1) * fc])
        prev = carry_ref[c]
        g1 = jnp.where(row == 0, prev[7:8, :], pltpu.roll(gate, 1, axis=0))
        g2 = jnp.where(row == 0, prev[6:7, :], jnp.where(row == 1, prev[7:8, :], pltpu.roll(gate, 2, axis=0)))
        carry_ref[c] = gate[tm - 8:tm, :]
        assert keep == 2
        conv = g2 * cw_ref[0:1, sl] + g1 * cw_ref[1:2, sl] + gate * cw_ref[2:3, sl] + cb_ref[:, sl]
        act = (jax.nn.gelu(conv) * up).astype(BF16)
        acc = acc + _dot(act, wout_ref[sl, :])
    o_ref[...] = acc


def _ffn(x2d, g, w_in, conv_w, conv_b, w_out, *, seq, tm, fc=256):
    t, d = x2d.shape
    nt = seq // tm
    const = lambda shape: pl.BlockSpec(shape, lambda i: (0, 0))
    return pl.pallas_call(
        functools.partial(_ffn_kernel, nt=nt, fc=fc),
        grid=(t // tm,),
        in_specs=[pl.BlockSpec((tm, d), lambda i: (i, 0)),
                  const((1, d)), const((d, 2 * D_FF)), const((CONV_W, D_FF)), const((1, D_FF)),
                  const((D_FF, d))],
        out_specs=pl.BlockSpec((tm, d), lambda i: (i, 0)),
        out_shape=jax.ShapeDtypeStruct((t, d), F32),
        scratch_shapes=[pltpu.VMEM((D_FF // fc, 8, fc), F32)],
        compiler_params=_params("arbitrary"),
        name="conv_ffn",
    )(x2d, g.astype(F32).reshape(1, d), w_in.astype(BF16), conv_w.astype(F32),
      conv_b.astype(F32).reshape(1, D_FF), w_out.astype(BF16))


def kernel(x, mem, mix_norm_g, ev_w_in, ev_ret_norm_g, ev_hg_norm_g, hg_lb_logits, ev_w_out, od_w_in, od_q_norm_g, od_k_norm_g, rel_bias, od_w_out, xa_norm_g, xa_mem_norm_g, xa_w_q, xa_w_kv, xa_q_norm_g, xa_k_norm_g, xa_w_o, ffn_norm_g, ffn_w_in, ffn_conv_w, ffn_conv_b, ffn_w_out):
    batch, seq, d = x.shape
    depth = mix_norm_g.shape[0]
    x2d = x.reshape(batch * seq, d)
    mem2d = mem.reshape(batch * mem.shape[1], d)
    tm = 512
    for l in range(depth):
        if l % 2 == 0:
            e = l // 2
            proj = _rms_proj(x2d, mix_norm_g[l], ev_w_in[e], tm=tm)
            y_ret = _retention(proj, ev_ret_norm_g[e], batch=batch, seq=seq, tm=tm)
            y_hg = _hgrn(proj, hg_lb_logits, ev_hg_norm_g[e], layer=l, batch=batch, seq=seq, tm=tm)
            x2d = _out_proj(x2d, [y_ret, y_hg], ev_w_out[e], tm=tm)
        else:
            o = l // 2
            qkv = _rms_proj(x2d, mix_norm_g[l], od_w_in[o], tm=tm)
            att = _dilated_attention(qkv, od_q_norm_g[o], od_k_norm_g[o], rel_bias, batch=batch, seq=seq)
            x2d = _out_proj(x2d, [att], od_w_out[o], tm=tm)
        mk, mv = _mem_kv(mem2d, xa_mem_norm_g[l], xa_w_kv[l], xa_k_norm_g[l], batch=batch)
        x2d = _xattn(x2d, xa_norm_g[l], xa_w_q[l], xa_q_norm_g[l], mk, mv, xa_w_o[l], seq=seq, tm=tm)
        x2d = _ffn(x2d, ffn_norm_g[l], ffn_w_in[l], ffn_conv_w[l], ffn_conv_b[l], ffn_w_out[l], seq=seq, tm=tm)
    return x2d.reshape(batch, seq, d)
```

```python
import functools
import math

import jax
import jax.numpy as jnp
import numpy as np
from jax import lax
from jax.experimental import pallas as pl
from jax.experimental.pallas import tpu as pltpu

F32 = jnp.float32
BF16 = jnp.bfloat16

D_MODEL = 1024
N_MEM = 256
EPS = 1e-6
HEAD_DIM = 128
RET_HEADS = 4
RET_CHUNK = 128
ROPE_BASE = 10000.0
HG_HEADS = 4
HG_CHUNK = 64
F_FLOOR = 1e-6
DSA_HEADS = 8
DSA_BRANCHES = ((128, 1), (512, 4), (2048, 16))
DSA_BLOCK = 128
DSA_UNROLL = 8
REL_BUCKETS = 32
REL_MAX_DIST = 2048
XA_HEADS = 4
XA_HD = 256
D_FF = 2816
CONV_W = 3
EV_GROUP = RET_HEADS * HEAD_DIM

VMEM_LIMIT_BYTES = 56 * 1024 * 1024
NEG_INF = float("-inf")

_NT = (((1,), (1,)), ((), ()))


def _params(*sem):
    return pltpu.CompilerParams(dimension_semantics=sem, vmem_limit_bytes=VMEM_LIMIT_BYTES)


def _rms_rows(x, g):
    return x * lax.rsqrt(jnp.mean(x * x, axis=-1, keepdims=True) + EPS) * g


def _dot(a, b):
    return jnp.dot(a, b, preferred_element_type=F32)


def _dot_nt(a, b):
    return lax.dot_general(a, b, _NT, preferred_element_type=F32)


PERM_BLOCK = 128
PERM_RESIDUES = 16
PERM_RUN = PERM_BLOCK // PERM_RESIDUES


def _block_permutation():
    rho = np.arange(PERM_BLOCK)
    src = PERM_RESIDUES * (rho % PERM_RUN) + rho // PERM_RUN
    p = np.zeros((PERM_BLOCK, PERM_BLOCK), np.float32)
    p[rho, src] = 1.0
    return p


def _permute_rows(p, x):
    blocks = [_dot(p, x[b * PERM_BLOCK:(b + 1) * PERM_BLOCK, :]).astype(BF16)
              for b in range(x.shape[0] // PERM_BLOCK)]
    return jnp.concatenate(blocks, axis=0)


def _rms_proj_kernel(*refs, col_chunk, permute):
    if permute:
        x_ref, g_ref, w_ref, p_ref, o_ref = refs
    else:
        x_ref, g_ref, w_ref, o_ref = refs
    h = _rms_rows(x_ref[...], g_ref[...]).astype(BF16)
    if permute:
        h = _permute_rows(p_ref[...], h)
    n = o_ref.shape[1]
    for c in range(n // col_chunk):
        sl = slice(c * col_chunk, (c + 1) * col_chunk)
        o_ref[:, sl] = _dot(h, w_ref[:, sl]).astype(o_ref.dtype)


def _rms_proj(x2d, g, w, *, tm, out_dtype=F32, col_chunk=512, permute=False):
    t, d = x2d.shape
    n = w.shape[1]
    in_specs = [
        pl.BlockSpec((tm, d), lambda i: (i, 0)),
        pl.BlockSpec((1, d), lambda i: (0, 0)),
        pl.BlockSpec((d, n), lambda i: (0, 0)),
    ]
    args = [x2d, g.reshape(1, d).astype(F32), w.astype(BF16)]
    if permute:
        in_specs.append(pl.BlockSpec((PERM_BLOCK, PERM_BLOCK), lambda i: (0, 0)))
        args.append(jnp.asarray(_block_permutation(), BF16))
    return pl.pallas_call(
        functools.partial(_rms_proj_kernel, col_chunk=col_chunk, permute=permute),
        grid=(t // tm,),
        in_specs=in_specs,
        out_specs=pl.BlockSpec((tm, n), lambda i: (i, 0)),
        out_shape=jax.ShapeDtypeStruct((t, n), out_dtype),
        compiler_params=_params("parallel"),
        name="rms_proj",
    )(*args)


def _out_proj_kernel(*refs, n_y, unpermute):
    x_ref = refs[0]
    y_refs = refs[1:1 + n_y]
    w_refs = refs[1 + n_y:1 + 2 * n_y]
    o_ref = refs[-1]
    acc = x_ref[...]
    for y_ref, w_ref in zip(y_refs, w_refs):
        y = y_ref[...]
        if unpermute:
            y = _permute_rows(refs[1 + 2 * n_y][...], y)
        acc = acc + _dot(y, w_ref[...])
    o_ref[...] = acc


def _out_proj(x2d, ys, w, *, tm, unpermute=False):
    t, d = x2d.shape
    n_y = len(ys)
    ws, off = [], 0
    for y in ys:
        ws.append(w[off:off + y.shape[1]].astype(BF16))
        off += y.shape[1]
    in_specs = [pl.BlockSpec((tm, d), lambda i: (i, 0))]
    in_specs += [pl.BlockSpec((tm, y.shape[1]), lambda i: (i, 0)) for y in ys]
    in_specs += [pl.BlockSpec(wk.shape, lambda i: (0, 0)) for wk in ws]
    args = [x2d, *ys, *ws]
    if unpermute:
        in_specs.append(pl.BlockSpec((PERM_BLOCK, PERM_BLOCK), lambda i: (0, 0)))
        args.append(jnp.asarray(_block_permutation().T, BF16))
    return pl.pallas_call(
        functools.partial(_out_proj_kernel, n_y=n_y, unpermute=unpermute),
        grid=(t // tm,),
        in_specs=in_specs,
        out_specs=pl.BlockSpec((tm, d), lambda i: (i, 0)),
        out_shape=jax.ShapeDtypeStruct((t, d), F32),
        compiler_params=_params("parallel"),
        name="out_proj",
    )(*args)


def _retention_consts(seq):
    h = jnp.arange(RET_HEADS, dtype=F32)
    log_gamma = jnp.log(1.0 - jnp.exp2(-5.0 - h))
    c = RET_CHUNK
    idx = jnp.arange(c, dtype=F32)
    diff = idx[:, None] - idx[None, :]
    decay = jnp.where(diff >= 0, jnp.exp(log_gamma[:, None, None] * jnp.maximum(diff, 0.0)), 0.0)
    xi = jnp.exp(log_gamma[:, None] * (idx + 1.0))
    zeta = jnp.exp(log_gamma[:, None] * (c - 1.0 - idx))
    chunk_decay = jnp.exp(log_gamma * c)
    ones = jnp.ones((RET_HEADS, c, HEAD_DIM), F32)
    half = HEAD_DIM // 2
    inv = 1.0 / (ROPE_BASE ** (jnp.arange(half, dtype=F32) / half))
    ang = jnp.arange(seq, dtype=F32)[:, None] * inv[None, :]
    cos, sin = jnp.cos(ang), jnp.sin(ang)
    return dict(
        decay=decay,
        xi=xi[:, :, None] * ones,
        zeta=zeta[:, :, None] * ones,
        chunk_decay=chunk_decay[:, None, None] * jnp.ones((RET_HEADS, 8, HEAD_DIM), F32),
        cos=jnp.concatenate([cos, cos], axis=-1),
        sin=jnp.concatenate([-sin, sin], axis=-1),
    )


def _retention_kernel(q_ref, k_ref, v_ref, g_ref, cos_ref, sin_ref, decay_ref, xi_ref, zeta_ref,
                      cd_ref, ng_ref, o_ref, state_ref):
    @pl.when(pl.program_id(2) == 0)
    def _():
        state_ref[...] = jnp.zeros_like(state_ref)

    c = RET_CHUNK
    half = HEAD_DIM // 2
    decay = decay_ref[...]
    xi = xi_ref[...]
    zeta = zeta_ref[...]
    cd = cd_ref[0:1, :]
    ng = ng_ref[...]
    for ci in range(q_ref.shape[0] // c):
        sl = slice(ci * c, (ci + 1) * c)
        cos = cos_ref[sl, :]
        sin = sin_ref[sl, :]
        q = q_ref[sl, :]
        k = k_ref[sl, :]
        q = q * cos + pltpu.roll(q, half, axis=1) * sin
        k = (k * cos + pltpu.roll(k, half, axis=1) * sin) * (HEAD_DIM ** -0.5)
        qb = q.astype(BF16)
        vb = v_ref[sl, :].astype(BF16)
        scores = _dot_nt(qb, k.astype(BF16)) * decay
        inner = _dot(scores.astype(BF16), vb)
        state = state_ref[...]
        cross = _dot(qb, state.astype(BF16)) * xi
        kv = _dot((k * zeta).T.astype(BF16), vb)
        state_ref[...] = kv + cd * state
        o = inner + cross
        mu = jnp.mean(o, axis=-1, keepdims=True)
        oc = o - mu
        var = jnp.mean(oc * oc, axis=-1, keepdims=True)
        y = oc * lax.rsqrt(var + EPS) * ng
        o_ref[sl, :] = (y * jax.nn.silu(g_ref[sl, :])).astype(o_ref.dtype)


def _retention(proj, norm_g, *, batch, seq, tm):
    consts = _retention_consts(seq)
    nt = seq // tm
    hh = RET_HEADS

    def col(group):
        return pl.BlockSpec((tm, HEAD_DIM), lambda b, h, i: (b * nt + i, group * hh + h))

    def per_head(rows):
        return pl.BlockSpec((None, rows, HEAD_DIM), lambda b, h, i: (h, 0, 0))

    pos = pl.BlockSpec((tm, HEAD_DIM), lambda b, h, i: (i, 0))
    return pl.pallas_call(
        _retention_kernel,
        grid=(batch, hh, nt),
        in_specs=[col(0), col(1), col(2), col(3), pos, pos,
                  per_head(RET_CHUNK), per_head(RET_CHUNK), per_head(RET_CHUNK), per_head(8), per_head(1)],
        out_specs=pl.BlockSpec((tm, HEAD_DIM), lambda b, h, i: (b * nt + i, h)),
        out_shape=jax.ShapeDtypeStruct((batch * seq, EV_GROUP), BF16),
        scratch_shapes=[pltpu.VMEM((HEAD_DIM, HEAD_DIM), F32)],
        compiler_params=_params("parallel", "parallel", "arbitrary"),
        name="retention",
    )(proj, proj, proj, proj, consts["cos"], consts["sin"], consts["decay"], consts["xi"],
      consts["zeta"], consts["chunk_decay"], norm_g.astype(F32).reshape(hh, 1, HEAD_DIM))


HG_LEVELS = (64, 32, 16, 8)


def _hgrn_consts():
    c = HG_CHUNK
    t = np.arange(c)[:, None]
    s = np.arange(c)[None, :]
    tril = (s <= t).astype(np.float32)
    mats, masks = [tril], []
    for lvl, bs in enumerate(HG_LEVELS):
        mid = (t // bs) * bs + bs // 2 - 1
        mats.append(tril - (s <= mid).astype(np.float32))
        same = (t // bs) == (s // bs)
        if lvl < len(HG_LEVELS) - 1:
            masks.append(same & (t % bs >= bs // 2) & (s % bs < bs // 2))
        else:
            masks.append(same & (s <= t))
    return jnp.asarray(np.concatenate(mats, axis=0)), jnp.asarray(np.stack(masks).astype(np.float32))


def _hgrn_kernel(q_ref, f_ref, i_ref, g_ref, lbl_ref, cum_w_ref, mask_ref, ng_ref, o_ref, state_ref, *,
                 layer):
    @pl.when(pl.program_id(2) == 0)
    def _():
        state_ref[...] = jnp.zeros_like(state_ref)

    c = HG_CHUNK
    logits = lbl_ref[...]
    e = jnp.exp(logits - jnp.max(logits, axis=0, keepdims=True))
    lb = jnp.sum(e[:layer + 1, :], axis=0, keepdims=True) / jnp.sum(e, axis=0, keepdims=True)
    cum_w = cum_w_ref[...]
    masks = [mask_ref[l] > 0.5 for l in range(len(HG_LEVELS))]
    ng = ng_ref[...]

    def chunk(ci, carry):
        sl = pl.ds(pl.multiple_of(ci * c, c), c)
        f = lb + (1.0 - lb) * jax.nn.sigmoid(f_ref[sl, :])
        log_f = jnp.log(jnp.maximum(f, F_FLOOR))
        key = 1.0 - f
        q = q_ref[sl, :]
        vb = i_ref[sl, :]
        sums = jnp.dot(cum_w, log_f, preferred_element_type=F32, precision=lax.Precision.HIGHEST)
        cum = sums[0:c, :]
        scores = jnp.zeros((c, c), F32)
        for lvl in range(len(HG_LEVELS)):
            a = sums[(lvl + 1) * c:(lvl + 2) * c, :]
            if lvl < len(HG_LEVELS) - 1:
                qa = q * jnp.exp(jnp.minimum(a, 0.0))
                ka = key * jnp.exp(jnp.minimum(-a, 0.0))
            else:
                qa = q * jnp.exp(a)
                ka = key * jnp.exp(-a)
            scores = jnp.where(masks[lvl], _dot_nt(qa.astype(BF16), ka.astype(BF16)), scores)
        vb16 = vb.astype(BF16)
        state_t = state_ref[...]
        out = _dot(scores.astype(BF16), vb16) + _dot_nt((q * jnp.exp(cum)).astype(BF16), state_t.astype(BF16))
        last = cum[c - 1:c, :]
        kd = (key * jnp.exp(last - cum)).astype(BF16)
        state_ref[...] = jnp.exp(last) * state_t + _dot(vb.T.astype(BF16), kd)
        y = out * lax.rsqrt(jnp.mean(out * out, axis=-1, keepdims=True) + EPS) * ng
        o_ref[sl, :] = (y * jax.nn.silu(g_ref[sl, :])).astype(o_ref.dtype)
        return carry

    lax.fori_loop(0, q_ref.shape[0] // c, chunk, 0)


def _hgrn(proj, lb_logits, norm_g, *, layer, batch, seq, tm):
    cum_w, masks = _hgrn_consts()
    nt = seq // tm
    hh = HG_HEADS
    slots = lb_logits.shape[0]
    lbl = lb_logits.astype(F32).reshape(slots, hh, HEAD_DIM).transpose(1, 0, 2)

    def col(group):
        return pl.BlockSpec((tm, HEAD_DIM), lambda b, h, i: (b * nt + i, group * hh + h))

    return pl.pallas_call(
        functools.partial(_hgrn_kernel, layer=layer),
        grid=(batch, hh, nt),
        in_specs=[col(4), col(5), col(6), col(7),
                  pl.BlockSpec((None, slots, HEAD_DIM), lambda b, h, i: (h, 0, 0)),
                  pl.BlockSpec(cum_w.shape, lambda b, h, i: (0, 0)),
                  pl.BlockSpec(masks.shape, lambda b, h, i: (0, 0, 0)),
                  pl.BlockSpec((None, 1, HEAD_DIM), lambda b, h, i: (h, 0, 0))],
        out_specs=pl.BlockSpec((tm, HEAD_DIM), lambda b, h, i: (b * nt + i, h)),
        out_shape=jax.ShapeDtypeStruct((batch * seq, EV_GROUP), BF16),
        scratch_shapes=[pltpu.VMEM((HEAD_DIM, HEAD_DIM), F32)],
        compiler_params=_params("parallel", "parallel", "arbitrary"),
        name="hgrn2",
    )(proj, proj, proj, proj, lbl, cum_w, masks, norm_g.astype(F32).reshape(hh, 1, HEAD_DIM))


def _t5_bucket(dist):
    exact = REL_BUCKETS // 2
    d = jnp.maximum(dist, 0)
    log_ratio = jnp.log(jnp.maximum(d, 1).astype(F32) / exact) / math.log(REL_MAX_DIST / exact)
    large = jnp.minimum(exact + (log_ratio * (REL_BUCKETS - exact)).astype(jnp.int32), REL_BUCKETS - 1)
    return jnp.where(d < exact, d, large)


def _dsa_row_order(dil):
    rho = np.arange(DSA_BLOCK)
    run = PERM_RUN * dil
    c, j, a = rho // run, (rho % run) // PERM_RUN, rho % PERM_RUN
    return (PERM_BLOCK // dil) * j + (PERM_RESIDUES // dil) * a + c


def _dsa_bucket_table():
    blk = DSA_BLOCK
    tabs = []
    for window, dil in DSA_BRANCHES:
        order = _dsa_row_order(dil)
        qi = jnp.asarray(order)[:, None]
        ki = jnp.asarray(np.concatenate([order, blk + order]))[None, :]
        delta = qi + blk - ki
        n_back = window // dil
        valid = (delta >= 0) & (delta <= n_back)
        tabs.append(jnp.where(valid, _t5_bucket(delta * dil), -1).astype(jnp.int32))
    return jnp.stack(tabs)


def _dsa_kernel(rb_ref, q_ref, k_ref, v_ref, qg_ref, kg_ref, bucket_ref, o_ref,
                qn_ref, kp_ref, vp_ref, m_ref, den_ref, num_ref, bias_ref, *, seq, pad_blocks):
    blk = DSA_BLOCK
    nj = seq // PERM_BLOCK
    tiled = (nj, PERM_RESIDUES, PERM_RUN, HEAD_DIM)
    head = pl.program_id(0)
    qn_ref[...] = _rms_rows(q_ref[...], qg_ref[...]).reshape(tiled)
    kp_ref[pad_blocks:pad_blocks + nj] = _rms_rows(k_ref[...], kg_ref[...]).reshape(tiled)
    vp_ref[pad_blocks:pad_blocks + nj] = v_ref[...].reshape(tiled)

    @pl.when(pl.program_id(1) == 0)
    def _():
        zeros = jnp.zeros((pad_blocks,) + tiled[1:], F32)
        kp_ref[0:pad_blocks] = zeros
        vp_ref[0:pad_blocks] = zeros
        for br in range(len(DSA_BRANCHES)):
            bucket = bucket_ref[br]
            bias = jnp.full(bucket.shape, NEG_INF, F32)
            for b in range(REL_BUCKETS):
                bias = jnp.where(bucket == b, rb_ref[b, head], bias)
            bias_ref[br] = bias

    first_block_keys = lax.broadcasted_iota(jnp.int32, (blk, 2 * blk), 1) >= blk
    scale = HEAD_DIM ** -0.5

    def gather(ref, j0, r, dil, spans):
        if dil == 1:
            return ref[pl.ds(j0, spans)].reshape(spans * blk, HEAD_DIM)
        pieces = [ref[pl.ds(j0 + s * dil, dil), r + dil * c]
                  for s in range(spans) for c in range(PERM_RESIDUES // dil)]
        return jnp.concatenate(pieces, axis=0).reshape(spans * blk, HEAD_DIM)

    def scatter(ref, j0, r, dil, val):
        if dil == 1:
            ref[j0] = val.reshape(tiled[1:])
            return
        run = PERM_RUN * dil
        for c in range(PERM_RESIDUES // dil):
            ref[pl.ds(j0, dil), r + dil * c] = val[c * run:(c + 1) * run, :].reshape(dil, PERM_RUN, HEAD_DIM)

    for br, (window, dil) in enumerate(DSA_BRANCHES):
        nb = nj // dil
        assert dil <= pad_blocks

        def blocks(it, carry, br=br, dil=dil, nb=nb):
            for u in range(DSA_UNROLL):
                idx = it * DSA_UNROLL + u
                r = idx // nb
                n = idx % nb
                j0 = n * dil
                qb = gather(qn_ref, j0, r, dil, 1)
                kb = gather(kp_ref, pad_blocks + j0 - dil, r, dil, 2)
                vb = gather(vp_ref, pad_blocks + j0 - dil, r, dil, 2)
                logits = _dot_nt(qb.astype(BF16), kb.astype(BF16)) * scale + bias_ref[br]
                logits = jnp.where(jnp.logical_or(n > 0, first_block_keys), logits, NEG_INF)
                m = jnp.max(logits, axis=-1, keepdims=True)
                p = jnp.exp(logits - m)
                den = jnp.sum(p, axis=-1, keepdims=True)
                num = _dot(p.astype(BF16), vb.astype(BF16))
                scatter(m_ref.at[br], j0, r, dil, jnp.broadcast_to(m, (blk, HEAD_DIM)))
                scatter(den_ref.at[br], j0, r, dil, jnp.broadcast_to(den, (blk, HEAD_DIM)))
                scatter(num_ref.at[br], j0, r, dil, num)
            return carry

        assert nj % DSA_UNROLL == 0
        lax.fori_loop(0, nj // DSA_UNROLL, blocks, 0)

    def merge(j, carry):
        n_br = len(DSA_BRANCHES)
        ms = [m_ref[br, j] for br in range(n_br)]
        m_all = functools.reduce(jnp.maximum, ms)
        ws = [jnp.exp(mi - m_all) for mi in ms]
        den = functools.reduce(lambda x, y: x + y, [ws[br] * den_ref[br, j] for br in range(n_br)])
        num = functools.reduce(lambda x, y: x + y, [ws[br] * num_ref[br, j] for br in range(n_br)])
        rows = pl.ds(pl.multiple_of(j * blk, blk), blk)
        o_ref[rows, :] = (num / den).reshape(blk, HEAD_DIM).astype(o_ref.dtype)
        return carry

    lax.fori_loop(0, nj, merge, 0, unroll=2)


def _dilated_attention(qkv, q_norm_g, k_norm_g, rel_bias, *, batch, seq):
    hh = DSA_HEADS
    nj = seq // PERM_BLOCK
    pad_blocks = max(dil for _, dil in DSA_BRANCHES)
    bucket = _dsa_bucket_table()
    tiled = (PERM_RESIDUES, PERM_RUN, HEAD_DIM)

    def col(group):
        return pl.BlockSpec((seq, HEAD_DIM), lambda h, b: (b, group * hh + h))

    gain = pl.BlockSpec((1, HEAD_DIM), lambda h, b: (0, 0))
    return pl.pallas_call(
        functools.partial(_dsa_kernel, seq=seq, pad_blocks=pad_blocks),
        grid=(hh, batch),
        in_specs=[pl.BlockSpec(memory_space=pltpu.SMEM),
                  col(0), col(1), col(2), gain, gain,
                  pl.BlockSpec(bucket.shape, lambda h, b: (0, 0, 0))],
        out_specs=pl.BlockSpec((seq, HEAD_DIM), lambda h, b: (b, h)),
        out_shape=jax.ShapeDtypeStruct((batch * seq, hh * HEAD_DIM), BF16),
        scratch_shapes=[pltpu.VMEM((nj,) + tiled, F32),
                        pltpu.VMEM((pad_blocks + nj,) + tiled, F32),
                        pltpu.VMEM((pad_blocks + nj,) + tiled, F32),
                        pltpu.VMEM((len(DSA_BRANCHES), nj) + tiled, F32),
                        pltpu.VMEM((len(DSA_BRANCHES), nj) + tiled, F32),
                        pltpu.VMEM((len(DSA_BRANCHES), nj) + tiled, F32),
                        pltpu.VMEM((len(DSA_BRANCHES), DSA_BLOCK, 2 * DSA_BLOCK), F32)],
        compiler_params=_params("parallel", "arbitrary"),
        name="dilated_attention",
    )(rel_bias.astype(F32), qkv, qkv, qkv, q_norm_g.astype(F32).reshape(1, HEAD_DIM),
      k_norm_g.astype(F32).reshape(1, HEAD_DIM), bucket)


def _mem_kv_kernel(mem_ref, g_ref, w_ref, kg_ref, k_ref, v_ref):
    h = _rms_rows(mem_ref[...], g_ref[...]).astype(BF16)
    d = XA_HEADS * XA_HD
    kg = kg_ref[...]
    for hd in range(XA_HEADS):
        sl = slice(hd * XA_HD, (hd + 1) * XA_HD)
        k_ref[:, sl] = _rms_rows(_dot(h, w_ref[:, sl]), kg).astype(k_ref.dtype)
    v_ref[...] = _dot(h, w_ref[:, d:2 * d]).astype(v_ref.dtype)


def _mem_kv(mem2d, g, w_kv, k_norm_g, *, batch):
    d = XA_HEADS * XA_HD
    return pl.pallas_call(
        _mem_kv_kernel,
        grid=(batch,),
        in_specs=[pl.BlockSpec((N_MEM, D_MODEL), lambda b: (b, 0)),
                  pl.BlockSpec((1, D_MODEL), lambda b: (0, 0)),
                  pl.BlockSpec((D_MODEL, 2 * d), lambda b: (0, 0)),
                  pl.BlockSpec((1, XA_HD), lambda b: (0, 0))],
        out_specs=[pl.BlockSpec((N_MEM, d), lambda b: (b, 0)),
                   pl.BlockSpec((N_MEM, d), lambda b: (b, 0))],
        out_shape=[jax.ShapeDtypeStruct((batch * N_MEM, d), BF16),
                   jax.ShapeDtypeStruct((batch * N_MEM, d), BF16)],
        compiler_params=_params("parallel"),
        name="mem_kv",
    )(mem2d, g.astype(F32).reshape(1, D_MODEL), w_kv.astype(BF16), k_norm_g.astype(F32).reshape(1, XA_HD))


def _xattn_kernel(x_ref, g_ref, wq_ref, qg_ref, mk_ref, mv_ref, wo_ref, o_ref, att_ref):
    x = x_ref[...]
    h = _rms_rows(x, g_ref[...]).astype(BF16)
    qg = qg_ref[...]
    for hd in range(XA_HEADS):
        sl = slice(hd * XA_HD, (hd + 1) * XA_HD)
        q = _rms_rows(_dot(h, wq_ref[:, sl]), qg).astype(BF16)
        logits = _dot_nt(q, mk_ref[:, sl]) * (XA_HD ** -0.5)
        p = jnp.exp(logits - jnp.max(logits, axis=-1, keepdims=True))
        p = p / jnp.sum(p, axis=-1, keepdims=True)
        att_ref[:, sl] = _dot(p.astype(BF16), mv_ref[:, sl]).astype(att_ref.dtype)
    o_ref[...] = x + _dot(att_ref[...], wo_ref[...])


def _xattn(x2d, g, w_q, q_norm_g, mk, mv, w_o, *, seq, tm):
    t, d = x2d.shape
    nt = seq // tm
    const = lambda shape: pl.BlockSpec(shape, lambda i: (0, 0))
    return pl.pallas_call(
        _xattn_kernel,
        grid=(t // tm,),
        in_specs=[pl.BlockSpec((tm, d), lambda i: (i, 0)),
                  const((1, d)), const((d, d)), const((1, XA_HD)),
                  pl.BlockSpec((N_MEM, d), lambda i: (i // nt, 0)),
                  pl.BlockSpec((N_MEM, d), lambda i: (i // nt, 0)),
                  const((d, d))],
        out_specs=pl.BlockSpec((tm, d), lambda i: (i, 0)),
        out_shape=jax.ShapeDtypeStruct((t, d), F32),
        scratch_shapes=[pltpu.VMEM((tm, d), BF16)],
        compiler_params=_params("parallel"),
        name="xattn",
    )(x2d, g.astype(F32).reshape(1, d), w_q.astype(BF16), q_norm_g.astype(F32).reshape(1, XA_HD),
      mk, mv, w_o.astype(BF16))


def _ffn_kernel(x_ref, g_ref, win_ref, cw_ref, cb_ref, wout_ref, o_ref, carry_ref, *, nt, fc):
    tm = x_ref.shape[0]
    keep = CONV_W - 1

    @pl.when(pl.program_id(0) % nt == 0)
    def _():
        carry_ref[...] = jnp.zeros_like(carry_ref)

    x = x_ref[...]
    h = _rms_rows(x, g_ref[...]).astype(BF16)
    row = lax.broadcasted_iota(jnp.int32, (tm, fc), 0)
    acc = x
    for c in range(D_FF // fc):
        sl = slice(c * fc, (c + 1) * fc)
        gate = _dot(h, win_ref[:, sl])
        up = _dot(h, win_ref[:, D_FF + c * fc:D_FF + (c + 1) * fc])
        prev = carry_ref[c]
        g1 = jnp.where(row == 0, prev[7:8, :], pltpu.roll(gate, 1, axis=0))
        g2 = jnp.where(row == 0, prev[6:7, :], jnp.where(row == 1, prev[7:8, :], pltpu.roll(gate, 2, axis=0)))
        carry_ref[c] = gate[tm - 8:tm, :]
        assert keep == 2
        conv = g2 * cw_ref[0:1, sl] + g1 * cw_ref[1:2, sl] + gate * cw_ref[2:3, sl] + cb_ref[:, sl]
        act = (jax.nn.gelu(conv) * up).astype(BF16)
        acc = acc + _dot(act, wout_ref[sl, :])
    o_ref[...] = acc


def _ffn(x2d, g, w_in, conv_w, conv_b, w_out, *, seq, tm, fc=256):
    t, d = x2d.shape
    nt = seq // tm
    const = lambda shape: pl.BlockSpec(shape, lambda i: (0, 0))
    return pl.pallas_call(
        functools.partial(_ffn_kernel, nt=nt, fc=fc),
        grid=(t // tm,),
        in_specs=[pl.BlockSpec((tm, d), lambda i: (i, 0)),
                  const((1, d)), const((d, 2 * D_FF)), const((CONV_W, D_FF)), const((1, D_FF)),
                  const((D_FF, d))],
        out_specs=pl.BlockSpec((tm, d), lambda i: (i, 0)),
        out_shape=jax.ShapeDtypeStruct((t, d), F32),
        scratch_shapes=[pltpu.VMEM((D_FF // fc, 8, fc), F32)],
        compiler_params=_params("arbitrary"),
        name="conv_ffn",
    )(x2d, g.astype(F32).reshape(1, d), w_in.astype(BF16), conv_w.astype(F32),
      conv_b.astype(F32).reshape(1, D_FF), w_out.astype(BF16))


def kernel(x, mem, mix_norm_g, ev_w_in, ev_ret_norm_g, ev_hg_norm_g, hg_lb_logits, ev_w_out, od_w_in, od_q_norm_g, od_k_norm_g, rel_bias, od_w_out, xa_norm_g, xa_mem_norm_g, xa_w_q, xa_w_kv, xa_q_norm_g, xa_k_norm_g, xa_w_o, ffn_norm_g, ffn_w_in, ffn_conv_w, ffn_conv_b, ffn_w_out):
    batch, seq, d = x.shape
    depth = mix_norm_g.shape[0]
    x2d = x.reshape(batch * seq, d)
    mem2d = mem.reshape(batch * mem.shape[1], d)
    tm = 512
    for l in range(depth):
        if l % 2 == 0:
            e = l // 2
            proj = _rms_proj(x2d, mix_norm_g[l], ev_w_in[e], tm=tm)
            y_ret = _retention(proj, ev_ret_norm_g[e], batch=batch, seq=seq, tm=tm)
            y_hg = _hgrn(proj, hg_lb_logits, ev_hg_norm_g[e], layer=l, batch=batch, seq=seq, tm=tm)
            x2d = _out_proj(x2d, [y_ret, y_hg], ev_w_out[e], tm=tm)
        else:
            o = l // 2
            qkv = _rms_proj(x2d, mix_norm_g[l], od_w_in[o], tm=tm, permute=True)
            att = _dilated_attention(qkv, od_q_norm_g[o], od_k_norm_g[o], rel_bias, batch=batch, seq=seq)
            x2d = _out_proj(x2d, [att], od_w_out[o], tm=tm, unpermute=True)
        mk, mv = _mem_kv(mem2d, xa_mem_norm_g[l], xa_w_kv[l], xa_k_norm_g[l], batch=batch)
        x2d = _xattn(x2d, xa_norm_g[l], xa_w_q[l], xa_q_norm_g[l], mk, mv, xa_w_o[l], seq=seq, tm=tm)
        x2d = _ffn(x2d, ffn_norm_g[l], ffn_w_in[l], ffn_conv_w[l], ffn_conv_b[l], ffn_w_out[l], seq=seq, tm=tm)
    return x2d.reshape(batch, seq, d)
```

```python
import functools
import math

import jax
import jax.numpy as jnp
import numpy as np
from jax import lax
from jax.experimental import pallas as pl
from jax.experimental.pallas import tpu as pltpu

F32 = jnp.float32
BF16 = jnp.bfloat16

D_MODEL = 1024
N_MEM = 256
EPS = 1e-6
HEAD_DIM = 128
RET_HEADS = 4
RET_CHUNK = 128
ROPE_BASE = 10000.0
HG_HEADS = 4
HG_CHUNK = 64
F_FLOOR = 1e-6
DSA_HEADS = 8
DSA_BRANCHES = ((128, 1), (512, 4), (2048, 16))
DSA_BLOCK = 128
DSA_UNROLL = 8
REL_BUCKETS = 32
REL_MAX_DIST = 2048
XA_HEADS = 4
XA_HD = 256
D_FF = 2816
CONV_W = 3
EV_GROUP = RET_HEADS * HEAD_DIM

VMEM_LIMIT_BYTES = 56 * 1024 * 1024
NEG_INF = float("-inf")

_NT = (((1,), (1,)), ((), ()))


def _params(*sem):
    return pltpu.CompilerParams(dimension_semantics=sem, vmem_limit_bytes=VMEM_LIMIT_BYTES)


def _rms_rows(x, g):
    return x * lax.rsqrt(jnp.mean(x * x, axis=-1, keepdims=True) + EPS) * g


def _dot(a, b):
    return jnp.dot(a, b, preferred_element_type=F32)


def _dot_nt(a, b):
    return lax.dot_general(a, b, _NT, preferred_element_type=F32)


PERM_BLOCK = 128
PERM_RESIDUES = 16
PERM_RUN = PERM_BLOCK // PERM_RESIDUES


def _block_permutation():
    rho = np.arange(PERM_BLOCK)
    src = PERM_RESIDUES * (rho % PERM_RUN) + rho // PERM_RUN
    p = np.zeros((PERM_BLOCK, PERM_BLOCK), np.float32)
    p[rho, src] = 1.0
    return p


def _permute_rows(p, x):
    blocks = [_dot(p, x[b * PERM_BLOCK:(b + 1) * PERM_BLOCK, :]).astype(BF16)
              for b in range(x.shape[0] // PERM_BLOCK)]
    return jnp.concatenate(blocks, axis=0)


def _rms_proj_kernel(*refs, col_chunk, permute):
    if permute:
        x_ref, g_ref, w_ref, p_ref, o_ref = refs
    else:
        x_ref, g_ref, w_ref, o_ref = refs
    h = _rms_rows(x_ref[...], g_ref[...]).astype(BF16)
    if permute:
        h = _permute_rows(p_ref[...], h)
    n = o_ref.shape[1]
    for c in range(n // col_chunk):
        sl = slice(c * col_chunk, (c + 1) * col_chunk)
        o_ref[:, sl] = _dot(h, w_ref[:, sl]).astype(o_ref.dtype)


def _rms_proj(x2d, g, w, *, tm, out_dtype=F32, col_chunk=512, permute=False):
    t, d = x2d.shape
    n = w.shape[1]
    in_specs = [
        pl.BlockSpec((tm, d), lambda i: (i, 0)),
        pl.BlockSpec((1, d), lambda i: (0, 0)),
        pl.BlockSpec((d, n), lambda i: (0, 0)),
    ]
    args = [x2d, g.reshape(1, d).astype(F32), w.astype(BF16)]
    if permute:
        in_specs.append(pl.BlockSpec((PERM_BLOCK, PERM_BLOCK), lambda i: (0, 0)))
        args.append(jnp.asarray(_block_permutation(), BF16))
    return pl.pallas_call(
        functools.partial(_rms_proj_kernel, col_chunk=col_chunk, permute=permute),
        grid=(t // tm,),
        in_specs=in_specs,
        out_specs=pl.BlockSpec((tm, n), lambda i: (i, 0)),
        out_shape=jax.ShapeDtypeStruct((t, n), out_dtype),
        compiler_params=_params("parallel"),
        name="rms_proj",
    )(*args)


def _out_proj_kernel(*refs, n_y, unpermute):
    x_ref = refs[0]
    y_refs = refs[1:1 + n_y]
    w_refs = refs[1 + n_y:1 + 2 * n_y]
    o_ref = refs[-1]
    acc = x_ref[...]
    for y_ref, w_ref in zip(y_refs, w_refs):
        y = y_ref[...]
        if unpermute:
            y = _permute_rows(refs[1 + 2 * n_y][...], y)
        acc = acc + _dot(y, w_ref[...])
    o_ref[...] = acc


def _out_proj(x2d, ys, w, *, tm, unpermute=False):
    t, d = x2d.shape
    n_y = len(ys)
    ws, off = [], 0
    for y in ys:
        ws.append(w[off:off + y.shape[1]].astype(BF16))
        off += y.shape[1]
    in_specs = [pl.BlockSpec((tm, d), lambda i: (i, 0))]
    in_specs += [pl.BlockSpec((tm, y.shape[1]), lambda i: (i, 0)) for y in ys]
    in_specs += [pl.BlockSpec(wk.shape, lambda i: (0, 0)) for wk in ws]
    args = [x2d, *ys, *ws]
    if unpermute:
        in_specs.append(pl.BlockSpec((PERM_BLOCK, PERM_BLOCK), lambda i: (0, 0)))
        args.append(jnp.asarray(_block_permutation().T, BF16))
    return pl.pallas_call(
        functools.partial(_out_proj_kernel, n_y=n_y, unpermute=unpermute),
        grid=(t // tm,),
        in_specs=in_specs,
        out_specs=pl.BlockSpec((tm, d), lambda i: (i, 0)),
        out_shape=jax.ShapeDtypeStruct((t, d), F32),
        compiler_params=_params("parallel"),
        name="out_proj",
    )(*args)


def _retention_consts(seq):
    h = jnp.arange(RET_HEADS, dtype=F32)
    log_gamma = jnp.log(1.0 - jnp.exp2(-5.0 - h))
    c = RET_CHUNK
    idx = jnp.arange(c, dtype=F32)
    diff = idx[:, None] - idx[None, :]
    decay = jnp.where(diff >= 0, jnp.exp(log_gamma[:, None, None] * jnp.maximum(diff, 0.0)), 0.0)
    xi = jnp.exp(log_gamma[:, None] * (idx + 1.0))
    zeta = jnp.exp(log_gamma[:, None] * (c - 1.0 - idx))
    chunk_decay = jnp.exp(log_gamma * c)
    ones = jnp.ones((RET_HEADS, c, HEAD_DIM), F32)
    half = HEAD_DIM // 2
    inv = 1.0 / (ROPE_BASE ** (jnp.arange(half, dtype=F32) / half))
    ang = jnp.arange(seq, dtype=F32)[:, None] * inv[None, :]
    cos, sin = jnp.cos(ang), jnp.sin(ang)
    return dict(
        decay=decay,
        xi=xi[:, :, None] * ones,
        zeta=zeta[:, :, None] * ones,
        chunk_decay=chunk_decay[:, None, None] * jnp.ones((RET_HEADS, 8, HEAD_DIM), F32),
        cos=jnp.concatenate([cos, cos], axis=-1),
        sin=jnp.concatenate([-sin, sin], axis=-1),
    )


def _retention_kernel(q_ref, k_ref, v_ref, g_ref, cos_ref, sin_ref, decay_ref, xi_ref, zeta_ref,
                      cd_ref, ng_ref, o_ref, state_ref):
    @pl.when(pl.program_id(2) == 0)
    def _():
        state_ref[...] = jnp.zeros_like(state_ref)

    c = RET_CHUNK
    half = HEAD_DIM // 2
    decay = decay_ref[...]
    xi = xi_ref[...]
    zeta = zeta_ref[...]
    cd = cd_ref[0:1, :]
    ng = ng_ref[...]
    for ci in range(q_ref.shape[0] // c):
        sl = slice(ci * c, (ci + 1) * c)
        cos = cos_ref[sl, :]
        sin = sin_ref[sl, :]
        q = q_ref[sl, :]
        k = k_ref[sl, :]
        q = q * cos + pltpu.roll(q, half, axis=1) * sin
        k = (k * cos + pltpu.roll(k, half, axis=1) * sin) * (HEAD_DIM ** -0.5)
        qb = q.astype(BF16)
        vb = v_ref[sl, :].astype(BF16)
        scores = _dot_nt(qb, k.astype(BF16)) * decay
        inner = _dot(scores.astype(BF16), vb)
        state = state_ref[...]
        cross = _dot(qb, state.astype(BF16)) * xi
        kv = _dot((k * zeta).T.astype(BF16), vb)
        state_ref[...] = kv + cd * state
        o = inner + cross
        mu = jnp.mean(o, axis=-1, keepdims=True)
        oc = o - mu
        var = jnp.mean(oc * oc, axis=-1, keepdims=True)
        y = oc * lax.rsqrt(var + EPS) * ng
        o_ref[sl, :] = (y * jax.nn.silu(g_ref[sl, :])).astype(o_ref.dtype)


def _retention(proj, norm_g, *, batch, seq, tm):
    consts = _retention_consts(seq)
    nt = seq // tm
    hh = RET_HEADS

    def col(group):
        return pl.BlockSpec((tm, HEAD_DIM), lambda b, h, i: (b * nt + i, group * hh + h))

    def per_head(rows):
        return pl.BlockSpec((None, rows, HEAD_DIM), lambda b, h, i: (h, 0, 0))

    pos = pl.BlockSpec((tm, HEAD_DIM), lambda b, h, i: (i, 0))
    return pl.pallas_call(
        _retention_kernel,
        grid=(batch, hh, nt),
        in_specs=[col(0), col(1), col(2), col(3), pos, pos,
                  per_head(RET_CHUNK), per_head(RET_CHUNK), per_head(RET_CHUNK), per_head(8), per_head(1)],
        out_specs=pl.BlockSpec((tm, HEAD_DIM), lambda b, h, i: (b * nt + i, h)),
        out_shape=jax.ShapeDtypeStruct((batch * seq, EV_GROUP), BF16),
        scratch_shapes=[pltpu.VMEM((HEAD_DIM, HEAD_DIM), F32)],
        compiler_params=_params("parallel", "parallel", "arbitrary"),
        name="retention",
    )(proj, proj, proj, proj, consts["cos"], consts["sin"], consts["decay"], consts["xi"],
      consts["zeta"], consts["chunk_decay"], norm_g.astype(F32).reshape(hh, 1, HEAD_DIM))


HG_LEVELS = (64, 32, 16, 8)


HG_GROUP = 4


def _hgrn_consts():
    c = HG_CHUNK
    t = np.arange(c)[:, None]
    s = np.arange(c)[None, :]
    tril = (s <= t).astype(np.float32)
    masks = []
    for lvl, bs in enumerate(HG_LEVELS):
        same = (t // bs) == (s // bs)
        if lvl < len(HG_LEVELS) - 1:
            masks.append(same & (t % bs >= bs // 2) & (s % bs < bs // 2))
        else:
            masks.append(same & (s <= t))
    return jnp.asarray(tril, BF16), jnp.asarray(np.stack(masks).astype(np.float32))


def _split3(x):
    hi = x.astype(BF16)
    rest = x - hi.astype(F32)
    mid = rest.astype(BF16)
    lo = (rest - mid.astype(F32)).astype(BF16)
    return hi, mid, lo


def _block_reference_rows(cum, bs, row):
    pieces = [jnp.broadcast_to(cum[b0 + row:b0 + row + 1, :], (bs, cum.shape[1]))
              for b0 in range(0, cum.shape[0], bs)]
    return pieces[0] if len(pieces) == 1 else jnp.concatenate(pieces, axis=0)


def _hgrn_kernel(q_ref, f_ref, i_ref, g_ref, lbl_ref, tril_ref, mask_ref, ng_ref, o_ref, state_ref, *,
                 layer):
    @pl.when(pl.program_id(2) == 0)
    def _():
        state_ref[...] = jnp.zeros_like(state_ref)

    c = HG_CHUNK
    logits = lbl_ref[...]
    e = jnp.exp(logits - jnp.max(logits, axis=0, keepdims=True))
    lb = jnp.sum(e[:layer + 1, :], axis=0, keepdims=True) / jnp.sum(e, axis=0, keepdims=True)
    tril = tril_ref[...]
    masks = [mask_ref[l] > 0.5 for l in range(len(HG_LEVELS))]
    ng = ng_ref[...]

    def group(gi, carry):
        chunks = range(HG_GROUP)
        sls = [pl.ds(pl.multiple_of((gi * HG_GROUP + u) * c, c), c) for u in chunks]
        fs = [lb + (1.0 - lb) * jax.nn.sigmoid(f_ref[sl, :]) for sl in sls]
        keys = [1.0 - f for f in fs]
        splits = [_split3(jnp.log(jnp.maximum(f, F_FLOOR))) for f in fs]
        cums = [_dot(tril, hi) + (_dot(tril, mid) + _dot(tril, lo)) for hi, mid, lo in splits]
        qs = [q_ref[sl, :] for sl in sls]
        scores = [jnp.zeros((c, c), F32) for _ in chunks]
        for lvl, bs in enumerate(HG_LEVELS):
            for u in chunks:
                a = cums[u] - _block_reference_rows(cums[u], bs, bs // 2 - 1)
                if lvl < len(HG_LEVELS) - 1:
                    decay = jnp.exp(-jnp.abs(a))
                    qa = qs[u] * decay
                    ka = keys[u] * decay
                else:
                    qa = qs[u] * jnp.exp(a)
                    ka = keys[u] * jnp.exp(-a)
                scores[u] = jnp.where(masks[lvl], _dot_nt(qa.astype(BF16), ka.astype(BF16)), scores[u])
        vs = [i_ref[sl, :] for sl in sls]
        lasts = [cum[c - 1:c, :] for cum in cums]
        intra = [_dot(scores[u].astype(BF16), vs[u].astype(BF16)) for u in chunks]
        kvs = [_dot(vs[u].T.astype(BF16), (keys[u] * jnp.exp(lasts[u] - cums[u])).astype(BF16)) for u in chunks]
        qcs = [(qs[u] * jnp.exp(cums[u])).astype(BF16) for u in chunks]
        state_t = state_ref[...]
        outs = []
        for u in chunks:
            outs.append(intra[u] + _dot_nt(qcs[u], state_t.astype(BF16)))
            state_t = jnp.exp(lasts[u]) * state_t + kvs[u]
        state_ref[...] = state_t
        for u in chunks:
            out = outs[u]
            y = out * lax.rsqrt(jnp.mean(out * out, axis=-1, keepdims=True) + EPS) * ng
            o_ref[sls[u], :] = (y * jax.nn.silu(g_ref[sls[u], :])).astype(o_ref.dtype)
        return carry

    assert (q_ref.shape[0] // c) % HG_GROUP == 0
    lax.fori_loop(0, q_ref.shape[0] // c // HG_GROUP, group, 0)


def _hgrn(proj, lb_logits, norm_g, *, layer, batch, seq, tm):
    cum_w, masks = _hgrn_consts()
    nt = seq // tm
    hh = HG_HEADS
    slots = lb_logits.shape[0]
    lbl = lb_logits.astype(F32).reshape(slots, hh, HEAD_DIM).transpose(1, 0, 2)

    def col(group):
        return pl.BlockSpec((tm, HEAD_DIM), lambda b, h, i: (b * nt + i, group * hh + h))

    return pl.pallas_call(
        functools.partial(_hgrn_kernel, layer=layer),
        grid=(batch, hh, nt),
        in_specs=[col(4), col(5), col(6), col(7),
                  pl.BlockSpec((None, slots, HEAD_DIM), lambda b, h, i: (h, 0, 0)),
                  pl.BlockSpec(cum_w.shape, lambda b, h, i: (0, 0)),
                  pl.BlockSpec(masks.shape, lambda b, h, i: (0, 0, 0)),
                  pl.BlockSpec((None, 1, HEAD_DIM), lambda b, h, i: (h, 0, 0))],
        out_specs=pl.BlockSpec((tm, HEAD_DIM), lambda b, h, i: (b * nt + i, h)),
        out_shape=jax.ShapeDtypeStruct((batch * seq, EV_GROUP), BF16),
        scratch_shapes=[pltpu.VMEM((HEAD_DIM, HEAD_DIM), F32)],
        compiler_params=_params("parallel", "parallel", "arbitrary"),
        name="hgrn2",
    )(proj, proj, proj, proj, lbl, cum_w, masks, norm_g.astype(F32).reshape(hh, 1, HEAD_DIM))


def _t5_bucket(dist):
    exact = REL_BUCKETS // 2
    d = jnp.maximum(dist, 0)
    log_ratio = jnp.log(jnp.maximum(d, 1).astype(F32) / exact) / math.log(REL_MAX_DIST / exact)
    large = jnp.minimum(exact + (log_ratio * (REL_BUCKETS - exact)).astype(jnp.int32), REL_BUCKETS - 1)
    return jnp.where(d < exact, d, large)


def _dsa_row_order(dil):
    rho = np.arange(DSA_BLOCK)
    run = PERM_RUN * dil
    c, j, a = rho // run, (rho % run) // PERM_RUN, rho % PERM_RUN
    return (PERM_BLOCK // dil) * j + (PERM_RESIDUES // dil) * a + c


def _dsa_bucket_table():
    blk = DSA_BLOCK
    tabs = []
    for window, dil in DSA_BRANCHES:
        order = _dsa_row_order(dil)
        qi = jnp.asarray(order)[:, None]
        ki = jnp.asarray(np.concatenate([order, blk + order]))[None, :]
        delta = qi + blk - ki
        n_back = window // dil
        valid = (delta >= 0) & (delta <= n_back)
        tabs.append(jnp.where(valid, _t5_bucket(delta * dil), -1).astype(jnp.int32))
    return jnp.stack(tabs)


def _dsa_kernel(rb_ref, q_ref, k_ref, v_ref, qg_ref, kg_ref, bucket_ref, o_ref,
                qn_ref, kp_ref, vp_ref, m_ref, den_ref, num_ref, bias_ref, *, seq, pad_blocks):
    blk = DSA_BLOCK
    nj = seq // PERM_BLOCK
    tiled = (nj, PERM_RESIDUES, PERM_RUN, HEAD_DIM)
    head = pl.program_id(0)
    qn_ref[...] = _rms_rows(q_ref[...], qg_ref[...]).reshape(tiled)
    kp_ref[pad_blocks:pad_blocks + nj] = _rms_rows(k_ref[...], kg_ref[...]).reshape(tiled)
    vp_ref[pad_blocks:pad_blocks + nj] = v_ref[...].reshape(tiled)

    @pl.when(pl.program_id(1) == 0)
    def _():
        zeros = jnp.zeros((pad_blocks,) + tiled[1:], F32)
        kp_ref[0:pad_blocks] = zeros
        vp_ref[0:pad_blocks] = zeros
        for br in range(len(DSA_BRANCHES)):
            bucket = bucket_ref[br]
            bias = jnp.full(bucket.shape, NEG_INF, F32)
            for b in range(REL_BUCKETS):
                bias = jnp.where(bucket == b, rb_ref[b, head], bias)
            bias_ref[br] = bias

    first_block_keys = lax.broadcasted_iota(jnp.int32, (blk, 2 * blk), 1) >= blk
    scale = HEAD_DIM ** -0.5

    def gather(ref, j0, r, dil, spans):
        if dil == 1:
            return ref[pl.ds(j0, spans)].reshape(spans * blk, HEAD_DIM)
        pieces = [ref[pl.ds(j0 + s * dil, dil), r + dil * c]
                  for s in range(spans) for c in range(PERM_RESIDUES // dil)]
        return jnp.concatenate(pieces, axis=0).reshape(spans * blk, HEAD_DIM)

    def scatter(ref, j0, r, dil, val):
        if dil == 1:
            ref[j0] = val.reshape(tiled[1:])
            return
        run = PERM_RUN * dil
        for c in range(PERM_RESIDUES // dil):
            ref[pl.ds(j0, dil), r + dil * c] = val[c * run:(c + 1) * run, :].reshape(dil, PERM_RUN, HEAD_DIM)

    for br, (window, dil) in enumerate(DSA_BRANCHES):
        nb = nj // dil
        assert dil <= pad_blocks

        def blocks(it, carry, br=br, dil=dil, nb=nb):
            for u in range(DSA_UNROLL):
                idx = it * DSA_UNROLL + u
                r = idx // nb
                n = idx % nb
                j0 = n * dil
                qb = gather(qn_ref, j0, r, dil, 1)
                kb = gather(kp_ref, pad_blocks + j0 - dil, r, dil, 2)
                vb = gather(vp_ref, pad_blocks + j0 - dil, r, dil, 2)
                logits = _dot_nt(qb.astype(BF16), kb.astype(BF16)) * scale + bias_ref[br]
                logits = jnp.where(jnp.logical_or(n > 0, first_block_keys), logits, NEG_INF)
                m = jnp.max(logits, axis=-1, keepdims=True)
                p = jnp.exp(logits - m)
                den = jnp.sum(p, axis=-1, keepdims=True)
                num = _dot(p.astype(BF16), vb.astype(BF16))
                scatter(m_ref.at[br], j0, r, dil, jnp.broadcast_to(m, (blk, HEAD_DIM)))
                scatter(den_ref.at[br], j0, r, dil, jnp.broadcast_to(den, (blk, HEAD_DIM)))
                scatter(num_ref.at[br], j0, r, dil, num)
            return carry

        assert nj % DSA_UNROLL == 0
        lax.fori_loop(0, nj // DSA_UNROLL, blocks, 0)

    def merge(j, carry):
        n_br = len(DSA_BRANCHES)
        ms = [m_ref[br, j] for br in range(n_br)]
        m_all = functools.reduce(jnp.maximum, ms)
        ws = [jnp.exp(mi - m_all) for mi in ms]
        den = functools.reduce(lambda x, y: x + y, [ws[br] * den_ref[br, j] for br in range(n_br)])
        num = functools.reduce(lambda x, y: x + y, [ws[br] * num_ref[br, j] for br in range(n_br)])
        rows = pl.ds(pl.multiple_of(j * blk, blk), blk)
        o_ref[rows, :] = (num / den).reshape(blk, HEAD_DIM).astype(o_ref.dtype)
        return carry

    lax.fori_loop(0, nj, merge, 0, unroll=2)


def _dilated_attention(qkv, q_norm_g, k_norm_g, rel_bias, *, batch, seq):
    hh = DSA_HEADS
    nj = seq // PERM_BLOCK
    pad_blocks = max(dil for _, dil in DSA_BRANCHES)
    bucket = _dsa_bucket_table()
    tiled = (PERM_RESIDUES, PERM_RUN, HEAD_DIM)

    def col(group):
        return pl.BlockSpec((seq, HEAD_DIM), lambda h, b: (b, group * hh + h))

    gain = pl.BlockSpec((1, HEAD_DIM), lambda h, b: (0, 0))
    return pl.pallas_call(
        functools.partial(_dsa_kernel, seq=seq, pad_blocks=pad_blocks),
        grid=(hh, batch),
        in_specs=[pl.BlockSpec(memory_space=pltpu.SMEM),
                  col(0), col(1), col(2), gain, gain,
                  pl.BlockSpec(bucket.shape, lambda h, b: (0, 0, 0))],
        out_specs=pl.BlockSpec((seq, HEAD_DIM), lambda h, b: (b, h)),
        out_shape=jax.ShapeDtypeStruct((batch * seq, hh * HEAD_DIM), BF16),
        scratch_shapes=[pltpu.VMEM((nj,) + tiled, F32),
                        pltpu.VMEM((pad_blocks + nj,) + tiled, F32),
                        pltpu.VMEM((pad_blocks + nj,) + tiled, F32),
                        pltpu.VMEM((len(DSA_BRANCHES), nj) + tiled, F32),
                        pltpu.VMEM((len(DSA_BRANCHES), nj) + tiled, F32),
                        pltpu.VMEM((len(DSA_BRANCHES), nj) + tiled, F32),
                        pltpu.VMEM((len(DSA_BRANCHES), DSA_BLOCK, 2 * DSA_BLOCK), F32)],
        compiler_params=_params("parallel", "arbitrary"),
        name="dilated_attention",
    )(rel_bias.astype(F32), qkv, qkv, qkv, q_norm_g.astype(F32).reshape(1, HEAD_DIM),
      k_norm_g.astype(F32).reshape(1, HEAD_DIM), bucket)


def _mem_kv_kernel(mem_ref, g_ref, w_ref, kg_ref, k_ref, v_ref):
    h = _rms_rows(mem_ref[...], g_ref[...]).astype(BF16)
    d = XA_HEADS * XA_HD
    kg = kg_ref[...]
    for hd in range(XA_HEADS):
        sl = slice(hd * XA_HD, (hd + 1) * XA_HD)
        k_ref[:, sl] = _rms_rows(_dot(h, w_ref[:, sl]), kg).astype(k_ref.dtype)
    v_ref[...] = _dot(h, w_ref[:, d:2 * d]).astype(v_ref.dtype)


def _mem_kv(mem2d, g, w_kv, k_norm_g, *, batch):
    d = XA_HEADS * XA_HD
    return pl.pallas_call(
        _mem_kv_kernel,
        grid=(batch,),
        in_specs=[pl.BlockSpec((N_MEM, D_MODEL), lambda b: (b, 0)),
                  pl.BlockSpec((1, D_MODEL), lambda b: (0, 0)),
                  pl.BlockSpec((D_MODEL, 2 * d), lambda b: (0, 0)),
                  pl.BlockSpec((1, XA_HD), lambda b: (0, 0))],
        out_specs=[pl.BlockSpec((N_MEM, d), lambda b: (b, 0)),
                   pl.BlockSpec((N_MEM, d), lambda b: (b, 0))],
        out_shape=[jax.ShapeDtypeStruct((batch * N_MEM, d), BF16),
                   jax.ShapeDtypeStruct((batch * N_MEM, d), BF16)],
        compiler_params=_params("parallel"),
        name="mem_kv",
    )(mem2d, g.astype(F32).reshape(1, D_MODEL), w_kv.astype(BF16), k_norm_g.astype(F32).reshape(1, XA_HD))


def _xattn_kernel(x_ref, g_ref, wq_ref, qg_ref, mk_ref, mv_ref, wo_ref, o_ref, att_ref):
    x = x_ref[...]
    h = _rms_rows(x, g_ref[...]).astype(BF16)
    qg = qg_ref[...]
    for hd in range(XA_HEADS):
        sl = slice(hd * XA_HD, (hd + 1) * XA_HD)
        q = _rms_rows(_dot(h, wq_ref[:, sl]), qg).astype(BF16)
        logits = _dot_nt(q, mk_ref[:, sl]) * (XA_HD ** -0.5)
        p = jnp.exp(logits - jnp.max(logits, axis=-1, keepdims=True))
        p = p / jnp.sum(p, axis=-1, keepdims=True)
        att_ref[:, sl] = _dot(p.astype(BF16), mv_ref[:, sl]).astype(att_ref.dtype)
    o_ref[...] = x + _dot(att_ref[...], wo_ref[...])


def _xattn(x2d, g, w_q, q_norm_g, mk, mv, w_o, *, seq, tm):
    t, d = x2d.shape
    nt = seq // tm
    const = lambda shape: pl.BlockSpec(shape, lambda i: (0, 0))
    return pl.pallas_call(
        _xattn_kernel,
        grid=(t // tm,),
        in_specs=[pl.BlockSpec((tm, d), lambda i: (i, 0)),
                  const((1, d)), const((d, d)), const((1, XA_HD)),
                  pl.BlockSpec((N_MEM, d), lambda i: (i // nt, 0)),
                  pl.BlockSpec((N_MEM, d), lambda i: (i // nt, 0)),
                  const((d, d))],
        out_specs=pl.BlockSpec((tm, d), lambda i: (i, 0)),
        out_shape=jax.ShapeDtypeStruct((t, d), F32),
        scratch_shapes=[pltpu.VMEM((tm, d), BF16)],
        compiler_params=_params("parallel"),
        name="xattn",
    )(x2d, g.astype(F32).reshape(1, d), w_q.astype(BF16), q_norm_g.astype(F32).reshape(1, XA_HD),
      mk, mv, w_o.astype(BF16))


def _ffn_kernel(x_ref, g_ref, win_ref, cw_ref, cb_ref, wout_ref, o_ref, carry_ref, *, nt, fc):
    tm = x_ref.shape[0]
    keep = CONV_W - 1

    @pl.when(pl.program_id(0) % nt == 0)
    def _():
        carry_ref[...] = jnp.zeros_like(carry_ref)

    x = x_ref[...]
    h = _rms_rows(x, g_ref[...]).astype(BF16)
    row = lax.broadcasted_iota(jnp.int32, (tm, fc), 0)
    acc = x
    for c in range(D_FF // fc):
        sl = slice(c * fc, (c + 1) * fc)
        gate = _dot(h, win_ref[:, sl])
        up = _dot(h, win_ref[:, D_FF + c * fc:D_FF + (c + 1) * fc])
        prev = carry_ref[c]
        g1 = jnp.where(row == 0, prev[7:8, :], pltpu.roll(gate, 1, axis=0))
        g2 = jnp.where(row == 0, prev[6:7, :], jnp.where(row == 1, prev[7:8, :], pltpu.roll(gate, 2, axis=0)))
        carry_ref[c] = gate[tm - 8:tm, :]
        assert keep == 2
        conv = g2 * cw_ref[0:1, sl] + g1 * cw_ref[1:2, sl] + gate * cw_ref[2:3, sl] + cb_ref[:, sl]
        act = (jax.nn.gelu(conv) * up).astype(BF16)
        acc = acc + _dot(act, wout_ref[sl, :])
    o_ref[...] = acc


def _ffn(x2d, g, w_in, conv_w, conv_b, w_out, *, seq, tm, fc=256):
    t, d = x2d.shape
    nt = seq // tm
    const = lambda shape: pl.BlockSpec(shape, lambda i: (0, 0))
    return pl.pallas_call(
        functools.partial(_ffn_kernel, nt=nt, fc=fc),
        grid=(t // tm,),
        in_specs=[pl.BlockSpec((tm, d), lambda i: (i, 0)),
                  const((1, d)), const((d, 2 * D_FF)), const((CONV_W, D_FF)), const((1, D_FF)),
                  const((D_FF, d))],
        out_specs=pl.BlockSpec((tm, d), lambda i: (i, 0)),
        out_shape=jax.ShapeDtypeStruct((t, d), F32),
        scratch_shapes=[pltpu.VMEM((D_FF // fc, 8, fc), F32)],
        compiler_params=_params("arbitrary"),
        name="conv_ffn",
    )(x2d, g.astype(F32).reshape(1, d), w_in.astype(BF16), conv_w.astype(F32),
      conv_b.astype(F32).reshape(1, D_FF), w_out.astype(BF16))


def kernel(x, mem, mix_norm_g, ev_w_in, ev_ret_norm_g, ev_hg_norm_g, hg_lb_logits, ev_w_out, od_w_in, od_q_norm_g, od_k_norm_g, rel_bias, od_w_out, xa_norm_g, xa_mem_norm_g, xa_w_q, xa_w_kv, xa_q_norm_g, xa_k_norm_g, xa_w_o, ffn_norm_g, ffn_w_in, ffn_conv_w, ffn_conv_b, ffn_w_out):
    batch, seq, d = x.shape
    depth = mix_norm_g.shape[0]
    x2d = x.reshape(batch * seq, d)
    mem2d = mem.reshape(batch * mem.shape[1], d)
    tm = 512
    for l in range(depth):
        if l % 2 == 0:
            e = l // 2
            proj = _rms_proj(x2d, mix_norm_g[l], ev_w_in[e], tm=tm)
            y_ret = _retention(proj, ev_ret_norm_g[e], batch=batch, seq=seq, tm=tm)
            y_hg = _hgrn(proj, hg_lb_logits, ev_hg_norm_g[e], layer=l, batch=batch, seq=seq, tm=2 * tm)
            x2d = _out_proj(x2d, [y_ret, y_hg], ev_w_out[e], tm=tm)
        else:
            o = l // 2
            qkv = _rms_proj(x2d, mix_norm_g[l], od_w_in[o], tm=tm, permute=True)
            att = _dilated_attention(qkv, od_q_norm_g[o], od_k_norm_g[o], rel_bias, batch=batch, seq=seq)
            x2d = _out_proj(x2d, [att], od_w_out[o], tm=tm, unpermute=True)
        mk, mv = _mem_kv(mem2d, xa_mem_norm_g[l], xa_w_kv[l], xa_k_norm_g[l], batch=batch)
        x2d = _xattn(x2d, xa_norm_g[l], xa_w_q[l], xa_q_norm_g[l], mk, mv, xa_w_o[l], seq=seq, tm=tm)
        x2d = _ffn(x2d, ffn_norm_g[l], ffn_w_in[l], ffn_conv_w[l], ffn_conv_b[l], ffn_w_out[l], seq=seq, tm=tm)
    return x2d.reshape(batch, seq, d)
```

```python
import functools
import math

import jax
import jax.numpy as jnp
import numpy as np
from jax import lax
from jax.experimental import pallas as pl
from jax.experimental.pallas import tpu as pltpu

F32 = jnp.float32
BF16 = jnp.bfloat16

D_MODEL = 1024
N_MEM = 256
EPS = 1e-6
HEAD_DIM = 128
RET_HEADS = 4
RET_CHUNK = 128
ROPE_BASE = 10000.0
HG_HEADS = 4
HG_CHUNK = 64
F_FLOOR = 1e-6
DSA_HEADS = 8
DSA_BRANCHES = ((128, 1), (512, 4), (2048, 16))
DSA_BLOCK = 128
DSA_UNROLL = 8
REL_BUCKETS = 32
REL_MAX_DIST = 2048
XA_HEADS = 4
XA_HD = 256
D_FF = 2816
CONV_W = 3
EV_GROUP = RET_HEADS * HEAD_DIM

VMEM_LIMIT_BYTES = 56 * 1024 * 1024
NEG_INF = float("-inf")

_NT = (((1,), (1,)), ((), ()))


def _params(*sem):
    return pltpu.CompilerParams(dimension_semantics=sem, vmem_limit_bytes=VMEM_LIMIT_BYTES)


def _rms_rows(x, g):
    return x * lax.rsqrt(jnp.mean(x * x, axis=-1, keepdims=True) + EPS) * g


def _dot(a, b):
    return jnp.dot(a, b, preferred_element_type=F32)


def _dot_nt(a, b):
    return lax.dot_general(a, b, _NT, preferred_element_type=F32)


PERM_BLOCK = 128
PERM_RESIDUES = 16
PERM_RUN = PERM_BLOCK // PERM_RESIDUES


def _block_permutation():
    rho = np.arange(PERM_BLOCK)
    src = PERM_RESIDUES * (rho % PERM_RUN) + rho // PERM_RUN
    p = np.zeros((PERM_BLOCK, PERM_BLOCK), np.float32)
    p[rho, src] = 1.0
    return p


def _permute_rows(p, x):
    blocks = [_dot(p, x[b * PERM_BLOCK:(b + 1) * PERM_BLOCK, :]).astype(BF16)
              for b in range(x.shape[0] // PERM_BLOCK)]
    return jnp.concatenate(blocks, axis=0)


def _rms_proj_kernel(*refs, col_chunk, permute):
    if permute:
        x_ref, g_ref, w_ref, p_ref, o_ref = refs
    else:
        x_ref, g_ref, w_ref, o_ref = refs
    h = _rms_rows(x_ref[...], g_ref[...]).astype(BF16)
    if permute:
        h = _permute_rows(p_ref[...], h)
    n = o_ref.shape[1]
    for c in range(n // col_chunk):
        sl = slice(c * col_chunk, (c + 1) * col_chunk)
        o_ref[:, sl] = _dot(h, w_ref[:, sl]).astype(o_ref.dtype)


def _rms_proj(x2d, g, w, *, tm, out_dtype=F32, col_chunk=512, permute=False):
    t, d = x2d.shape
    n = w.shape[1]
    in_specs = [
        pl.BlockSpec((tm, d), lambda i: (i, 0)),
        pl.BlockSpec((1, d), lambda i: (0, 0)),
        pl.BlockSpec((d, n), lambda i: (0, 0)),
    ]
    args = [x2d, g.reshape(1, d).astype(F32), w.astype(BF16)]
    if permute:
        in_specs.append(pl.BlockSpec((PERM_BLOCK, PERM_BLOCK), lambda i: (0, 0)))
        args.append(jnp.asarray(_block_permutation(), BF16))
    return pl.pallas_call(
        functools.partial(_rms_proj_kernel, col_chunk=col_chunk, permute=permute),
        grid=(t // tm,),
        in_specs=in_specs,
        out_specs=pl.BlockSpec((tm, n), lambda i: (i, 0)),
        out_shape=jax.ShapeDtypeStruct((t, n), out_dtype),
        compiler_params=_params("parallel"),
        name="rms_proj",
    )(*args)


def _out_proj_kernel(*refs, n_y, unpermute):
    x_ref = refs[0]
    y_refs = refs[1:1 + n_y]
    w_refs = refs[1 + n_y:1 + 2 * n_y]
    o_ref = refs[-1]
    acc = x_ref[...]
    for y_ref, w_ref in zip(y_refs, w_refs):
        y = y_ref[...]
        if unpermute:
            y = _permute_rows(refs[1 + 2 * n_y][...], y)
        acc = acc + _dot(y, w_ref[...])
    o_ref[...] = acc


def _out_proj(x2d, ys, w, *, tm, unpermute=False):
    t, d = x2d.shape
    n_y = len(ys)
    ws, off = [], 0
    for y in ys:
        ws.append(w[off:off + y.shape[1]].astype(BF16))
        off += y.shape[1]
    in_specs = [pl.BlockSpec((tm, d), lambda i: (i, 0))]
    in_specs += [pl.BlockSpec((tm, y.shape[1]), lambda i: (i, 0)) for y in ys]
    in_specs += [pl.BlockSpec(wk.shape, lambda i: (0, 0)) for wk in ws]
    args = [x2d, *ys, *ws]
    if unpermute:
        in_specs.append(pl.BlockSpec((PERM_BLOCK, PERM_BLOCK), lambda i: (0, 0)))
        args.append(jnp.asarray(_block_permutation().T, BF16))
    return pl.pallas_call(
        functools.partial(_out_proj_kernel, n_y=n_y, unpermute=unpermute),
        grid=(t // tm,),
        in_specs=in_specs,
        out_specs=pl.BlockSpec((tm, d), lambda i: (i, 0)),
        out_shape=jax.ShapeDtypeStruct((t, d), F32),
        compiler_params=_params("parallel"),
        name="out_proj",
    )(*args)


def _retention_consts(seq):
    h = jnp.arange(RET_HEADS, dtype=F32)
    log_gamma = jnp.log(1.0 - jnp.exp2(-5.0 - h))
    c = RET_CHUNK
    idx = jnp.arange(c, dtype=F32)
    diff = idx[:, None] - idx[None, :]
    decay = jnp.where(diff >= 0, jnp.exp(log_gamma[:, None, None] * jnp.maximum(diff, 0.0)), 0.0)
    xi = jnp.exp(log_gamma[:, None] * (idx + 1.0))
    zeta = jnp.exp(log_gamma[:, None] * (c - 1.0 - idx))
    chunk_decay = jnp.exp(log_gamma * c)
    ones = jnp.ones((RET_HEADS, c, HEAD_DIM), F32)
    half = HEAD_DIM // 2
    inv = 1.0 / (ROPE_BASE ** (jnp.arange(half, dtype=F32) / half))
    ang = jnp.arange(seq, dtype=F32)[:, None] * inv[None, :]
    cos, sin = jnp.cos(ang), jnp.sin(ang)
    return dict(
        decay=decay,
        xi=xi[:, :, None] * ones,
        zeta=zeta[:, :, None] * ones,
        chunk_decay=chunk_decay[:, None, None] * jnp.ones((RET_HEADS, 8, HEAD_DIM), F32),
        cos=jnp.concatenate([cos, cos], axis=-1),
        sin=jnp.concatenate([-sin, sin], axis=-1),
    )


def _retention_kernel(q_ref, k_ref, v_ref, g_ref, cos_ref, sin_ref, decay_ref, xi_ref, zeta_ref,
                      cd_ref, ng_ref, o_ref, state_ref):
    @pl.when(pl.program_id(2) == 0)
    def _():
        state_ref[...] = jnp.zeros_like(state_ref)

    c = RET_CHUNK
    half = HEAD_DIM // 2
    decay = decay_ref[...]
    xi = xi_ref[...]
    zeta = zeta_ref[...]
    cd = cd_ref[0:1, :]
    ng = ng_ref[...]
    for ci in range(q_ref.shape[0] // c):
        sl = slice(ci * c, (ci + 1) * c)
        cos = cos_ref[sl, :]
        sin = sin_ref[sl, :]
        q = q_ref[sl, :]
        k = k_ref[sl, :]
        q = q * cos + pltpu.roll(q, half, axis=1) * sin
        k = (k * cos + pltpu.roll(k, half, axis=1) * sin) * (HEAD_DIM ** -0.5)
        qb = q.astype(BF16)
        vb = v_ref[sl, :].astype(BF16)
        scores = _dot_nt(qb, k.astype(BF16)) * decay
        inner = _dot(scores.astype(BF16), vb)
        state = state_ref[...]
        cross = _dot(qb, state.astype(BF16)) * xi
        kv = _dot((k * zeta).T.astype(BF16), vb)
        state_ref[...] = kv + cd * state
        o = inner + cross
        mu = jnp.mean(o, axis=-1, keepdims=True)
        oc = o - mu
        var = jnp.mean(oc * oc, axis=-1, keepdims=True)
        y = oc * lax.rsqrt(var + EPS) * ng
        o_ref[sl, :] = (y * jax.nn.silu(g_ref[sl, :])).astype(o_ref.dtype)


def _retention(proj, norm_g, *, batch, seq, tm):
    consts = _retention_consts(seq)
    nt = seq // tm
    hh = RET_HEADS

    def col(group):
        return pl.BlockSpec((tm, HEAD_DIM), lambda b, h, i: (b * nt + i, group * hh + h))

    def per_head(rows):
        return pl.BlockSpec((None, rows, HEAD_DIM), lambda b, h, i: (h, 0, 0))

    pos = pl.BlockSpec((tm, HEAD_DIM), lambda b, h, i: (i, 0))
    return pl.pallas_call(
        _retention_kernel,
        grid=(batch, hh, nt),
        in_specs=[col(0), col(1), col(2), col(3), pos, pos,
                  per_head(RET_CHUNK), per_head(RET_CHUNK), per_head(RET_CHUNK), per_head(8), per_head(1)],
        out_specs=pl.BlockSpec((tm, HEAD_DIM), lambda b, h, i: (b * nt + i, h)),
        out_shape=jax.ShapeDtypeStruct((batch * seq, EV_GROUP), BF16),
        scratch_shapes=[pltpu.VMEM((HEAD_DIM, HEAD_DIM), F32)],
        compiler_params=_params("parallel", "parallel", "arbitrary"),
        name="retention",
    )(proj, proj, proj, proj, consts["cos"], consts["sin"], consts["decay"], consts["xi"],
      consts["zeta"], consts["chunk_decay"], norm_g.astype(F32).reshape(hh, 1, HEAD_DIM))


HG_LEVELS = (64, 32, 16, 8)


HG_GROUP = 4


def _hgrn_consts():
    c = HG_CHUNK
    t = np.arange(c)[:, None]
    s = np.arange(c)[None, :]
    tril = (s <= t).astype(np.float32)
    masks = []
    for lvl, bs in enumerate(HG_LEVELS):
        same = (t // bs) == (s // bs)
        if lvl < len(HG_LEVELS) - 1:
            masks.append(same & (t % bs >= bs // 2) & (s % bs < bs // 2))
        else:
            masks.append(same & (s <= t))
    return jnp.asarray(tril, BF16), jnp.asarray(np.stack(masks).astype(np.float32))


def _split3(x):
    hi = x.astype(BF16)
    rest = x - hi.astype(F32)
    mid = rest.astype(BF16)
    lo = (rest - mid.astype(F32)).astype(BF16)
    return hi, mid, lo


def _block_reference_rows(cum, bs, row):
    pieces = [jnp.broadcast_to(cum[b0 + row:b0 + row + 1, :], (bs, cum.shape[1]))
              for b0 in range(0, cum.shape[0], bs)]
    return pieces[0] if len(pieces) == 1 else jnp.concatenate(pieces, axis=0)


def _hgrn_kernel(q_ref, f_ref, i_ref, g_ref, lbl_ref, tril_ref, mask_ref, ng_ref, o_ref, state_ref, *,
                 layer):
    @pl.when(pl.program_id(2) == 0)
    def _():
        state_ref[...] = jnp.zeros_like(state_ref)

    c = HG_CHUNK
    logits = lbl_ref[...]
    e = jnp.exp(logits - jnp.max(logits, axis=0, keepdims=True))
    lb = jnp.sum(e[:layer + 1, :], axis=0, keepdims=True) / jnp.sum(e, axis=0, keepdims=True)
    tril = tril_ref[...]
    masks = [mask_ref[l] > 0.5 for l in range(len(HG_LEVELS))]
    ng = ng_ref[...]

    def group(gi, carry):
        chunks = range(HG_GROUP)
        sls = [pl.ds(pl.multiple_of((gi * HG_GROUP + u) * c, c), c) for u in chunks]
        fs = [lb + (1.0 - lb) * jax.nn.sigmoid(f_ref[sl, :]) for sl in sls]
        keys = [1.0 - f for f in fs]
        splits = [_split3(jnp.log(jnp.maximum(f, F_FLOOR))) for f in fs]
        cums = [_dot(tril, hi) + (_dot(tril, mid) + _dot(tril, lo)) for hi, mid, lo in splits]
        qs = [q_ref[sl, :] for sl in sls]
        scores = [jnp.zeros((c, c), F32) for _ in chunks]
        for lvl, bs in enumerate(HG_LEVELS):
            for u in chunks:
                a = cums[u] - _block_reference_rows(cums[u], bs, bs // 2 - 1)
                if lvl < len(HG_LEVELS) - 1:
                    decay = jnp.exp(-jnp.abs(a))
                    qa = qs[u] * decay
                    ka = keys[u] * decay
                else:
                    qa = qs[u] * jnp.exp(a)
                    ka = keys[u] * jnp.exp(-a)
                scores[u] = jnp.where(masks[lvl], _dot_nt(qa.astype(BF16), ka.astype(BF16)), scores[u])
        vs = [i_ref[sl, :] for sl in sls]
        lasts = [cum[c - 1:c, :] for cum in cums]
        intra = [_dot(scores[u].astype(BF16), vs[u].astype(BF16)) for u in chunks]
        kvs = [_dot(vs[u].T.astype(BF16), (keys[u] * jnp.exp(lasts[u] - cums[u])).astype(BF16)) for u in chunks]
        qcs = [(qs[u] * jnp.exp(cums[u])).astype(BF16) for u in chunks]
        state_t = state_ref[...]
        outs = []
        for u in chunks:
            outs.append(intra[u] + _dot_nt(qcs[u], state_t.astype(BF16)))
            state_t = jnp.exp(lasts[u]) * state_t + kvs[u]
        state_ref[...] = state_t
        for u in chunks:
            out = outs[u]
            y = out * lax.rsqrt(jnp.mean(out * out, axis=-1, keepdims=True) + EPS) * ng
            o_ref[sls[u], :] = (y * jax.nn.silu(g_ref[sls[u], :])).astype(o_ref.dtype)
        return carry

    assert (q_ref.shape[0] // c) % HG_GROUP == 0
    lax.fori_loop(0, q_ref.shape[0] // c // HG_GROUP, group, 0)


def _hgrn(proj, lb_logits, norm_g, *, layer, batch, seq, tm):
    cum_w, masks = _hgrn_consts()
    nt = seq // tm
    hh = HG_HEADS
    slots = lb_logits.shape[0]
    lbl = lb_logits.astype(F32).reshape(slots, hh, HEAD_DIM).transpose(1, 0, 2)

    def col(group):
        return pl.BlockSpec((tm, HEAD_DIM), lambda b, h, i: (b * nt + i, group * hh + h))

    return pl.pallas_call(
        functools.partial(_hgrn_kernel, layer=layer),
        grid=(batch, hh, nt),
        in_specs=[col(4), col(5), col(6), col(7),
                  pl.BlockSpec((None, slots, HEAD_DIM), lambda b, h, i: (h, 0, 0)),
                  pl.BlockSpec(cum_w.shape, lambda b, h, i: (0, 0)),
                  pl.BlockSpec(masks.shape, lambda b, h, i: (0, 0, 0)),
                  pl.BlockSpec((None, 1, HEAD_DIM), lambda b, h, i: (h, 0, 0))],
        out_specs=pl.BlockSpec((tm, HEAD_DIM), lambda b, h, i: (b * nt + i, h)),
        out_shape=jax.ShapeDtypeStruct((batch * seq, EV_GROUP), BF16),
        scratch_shapes=[pltpu.VMEM((HEAD_DIM, HEAD_DIM), F32)],
        compiler_params=_params("parallel", "parallel", "arbitrary"),
        name="hgrn2",
    )(proj, proj, proj, proj, lbl, cum_w, masks, norm_g.astype(F32).reshape(hh, 1, HEAD_DIM))


def _t5_bucket(dist):
    exact = REL_BUCKETS // 2
    d = jnp.maximum(dist, 0)
    log_ratio = jnp.log(jnp.maximum(d, 1).astype(F32) / exact) / math.log(REL_MAX_DIST / exact)
    large = jnp.minimum(exact + (log_ratio * (REL_BUCKETS - exact)).astype(jnp.int32), REL_BUCKETS - 1)
    return jnp.where(d < exact, d, large)


def _dsa_row_order(dil):
    rho = np.arange(DSA_BLOCK)
    run = PERM_RUN * dil
    c, j, a = rho // run, (rho % run) // PERM_RUN, rho % PERM_RUN
    return (PERM_BLOCK // dil) * j + (PERM_RESIDUES // dil) * a + c


def _dsa_bucket_table():
    blk = DSA_BLOCK
    tabs = []
    for window, dil in DSA_BRANCHES:
        order = _dsa_row_order(dil)
        qi = jnp.asarray(order)[:, None]
        ki = jnp.asarray(np.concatenate([order, blk + order]))[None, :]
        delta = qi + blk - ki
        n_back = window // dil
        valid = (delta >= 0) & (delta <= n_back)
        tabs.append(jnp.where(valid, _t5_bucket(delta * dil), -1).astype(jnp.int32))
    return jnp.stack(tabs)


def _dsa_kernel(rb_ref, q_ref, k_ref, v_ref, qg_ref, kg_ref, bucket_ref, o_ref,
                qn_ref, kp_ref, vp_ref, m_ref, den_ref, num_ref, bias_ref, *, seq, pad_blocks):
    blk = DSA_BLOCK
    nj = seq // PERM_BLOCK
    tiled = (nj, PERM_RESIDUES, PERM_RUN, HEAD_DIM)
    head = pl.program_id(0)
    qn_ref[...] = _rms_rows(q_ref[...], qg_ref[...]).reshape(tiled)
    kp_ref[pad_blocks:pad_blocks + nj] = _rms_rows(k_ref[...], kg_ref[...]).reshape(tiled)
    vp_ref[pad_blocks:pad_blocks + nj] = v_ref[...].reshape(tiled)

    @pl.when(pl.program_id(1) == 0)
    def _():
        zeros = jnp.zeros((pad_blocks,) + tiled[1:], F32)
        kp_ref[0:pad_blocks] = zeros
        vp_ref[0:pad_blocks] = zeros
        for br in range(len(DSA_BRANCHES)):
            bucket = bucket_ref[br]
            bias = jnp.full(bucket.shape, NEG_INF, F32)
            for b in range(REL_BUCKETS):
                bias = jnp.where(bucket == b, rb_ref[b, head], bias)
            bias_ref[br] = bias

    first_block_keys = lax.broadcasted_iota(jnp.int32, (blk, 2 * blk), 1) >= blk
    scale = HEAD_DIM ** -0.5

    def gather(ref, j0, r, dil, spans):
        if dil == 1:
            return ref[pl.ds(j0, spans)].reshape(spans * blk, HEAD_DIM)
        pieces = [ref[pl.ds(j0 + s * dil, dil), r + dil * c]
                  for s in range(spans) for c in range(PERM_RESIDUES // dil)]
        return jnp.concatenate(pieces, axis=0).reshape(spans * blk, HEAD_DIM)

    def scatter(ref, j0, r, dil, val):
        if dil == 1:
            ref[j0] = val.reshape(tiled[1:])
            return
        run = PERM_RUN * dil
        for c in range(PERM_RESIDUES // dil):
            ref[pl.ds(j0, dil), r + dil * c] = val[c * run:(c + 1) * run, :].reshape(dil, PERM_RUN, HEAD_DIM)

    for br, (window, dil) in enumerate(DSA_BRANCHES):
        nb = nj // dil
        assert dil <= pad_blocks

        def blocks(it, carry, br=br, dil=dil, nb=nb):
            for u in range(DSA_UNROLL):
                idx = it * DSA_UNROLL + u
                r = idx // nb
                n = idx % nb
                j0 = n * dil
                qb = gather(qn_ref, j0, r, dil, 1)
                kb = gather(kp_ref, pad_blocks + j0 - dil, r, dil, 2)
                vb = gather(vp_ref, pad_blocks + j0 - dil, r, dil, 2)
                logits = _dot_nt(qb.astype(BF16), kb.astype(BF16)) * scale + bias_ref[br]
                logits = jnp.where(jnp.logical_or(n > 0, first_block_keys), logits, NEG_INF)
                m = jnp.max(logits, axis=-1, keepdims=True)
                p = jnp.exp(logits - m)
                den = jnp.sum(p, axis=-1, keepdims=True)
                num = _dot(p.astype(BF16), vb.astype(BF16))
                scatter(m_ref.at[br], j0, r, dil, jnp.broadcast_to(m, (blk, HEAD_DIM)))
                scatter(den_ref.at[br], j0, r, dil, jnp.broadcast_to(den, (blk, HEAD_DIM)))
                scatter(num_ref.at[br], j0, r, dil, num)
            return carry

        assert nj % DSA_UNROLL == 0
        lax.fori_loop(0, nj // DSA_UNROLL, blocks, 0)

    def merge(j, carry):
        n_br = len(DSA_BRANCHES)
        ms = [m_ref[br, j] for br in range(n_br)]
        m_all = functools.reduce(jnp.maximum, ms)
        ws = [jnp.exp(mi - m_all) for mi in ms]
        den = functools.reduce(lambda x, y: x + y, [ws[br] * den_ref[br, j] for br in range(n_br)])
        num = functools.reduce(lambda x, y: x + y, [ws[br] * num_ref[br, j] for br in range(n_br)])
        rows = pl.ds(pl.multiple_of(j * blk, blk), blk)
        o_ref[rows, :] = (num / den).reshape(blk, HEAD_DIM).astype(o_ref.dtype)
        return carry

    lax.fori_loop(0, nj, merge, 0, unroll=2)


def _dilated_attention(qkv, q_norm_g, k_norm_g, rel_bias, *, batch, seq):
    hh = DSA_HEADS
    nj = seq // PERM_BLOCK
    pad_blocks = max(dil for _, dil in DSA_BRANCHES)
    bucket = _dsa_bucket_table()
    tiled = (PERM_RESIDUES, PERM_RUN, HEAD_DIM)

    def col(group):
        return pl.BlockSpec((seq, HEAD_DIM), lambda h, b: (b, group * hh + h))

    gain = pl.BlockSpec((1, HEAD_DIM), lambda h, b: (0, 0))
    return pl.pallas_call(
        functools.partial(_dsa_kernel, seq=seq, pad_blocks=pad_blocks),
        grid=(hh, batch),
        in_specs=[pl.BlockSpec(memory_space=pltpu.SMEM),
                  col(0), col(1), col(2), gain, gain,
                  pl.BlockSpec(bucket.shape, lambda h, b: (0, 0, 0))],
        out_specs=pl.BlockSpec((seq, HEAD_DIM), lambda h, b: (b, h)),
        out_shape=jax.ShapeDtypeStruct((batch * seq, hh * HEAD_DIM), BF16),
        scratch_shapes=[pltpu.VMEM((nj,) + tiled, F32),
                        pltpu.VMEM((pad_blocks + nj,) + tiled, F32),
                        pltpu.VMEM((pad_blocks + nj,) + tiled, F32),
                        pltpu.VMEM((len(DSA_BRANCHES), nj) + tiled, F32),
                        pltpu.VMEM((len(DSA_BRANCHES), nj) + tiled, F32),
                        pltpu.VMEM((len(DSA_BRANCHES), nj) + tiled, F32),
                        pltpu.VMEM((len(DSA_BRANCHES), DSA_BLOCK, 2 * DSA_BLOCK), F32)],
        compiler_params=_params("parallel", "arbitrary"),
        name="dilated_attention",
    )(rel_bias.astype(F32), qkv, qkv, qkv, q_norm_g.astype(F32).reshape(1, HEAD_DIM),
      k_norm_g.astype(F32).reshape(1, HEAD_DIM), bucket)


def _mem_kv_kernel(mem_ref, g_ref, w_ref, kg_ref, k_ref, v_ref):
    h = _rms_rows(mem_ref[...], g_ref[...]).astype(BF16)
    d = XA_HEADS * XA_HD
    kg = kg_ref[...]
    for hd in range(XA_HEADS):
        sl = slice(hd * XA_HD, (hd + 1) * XA_HD)
        k_ref[:, sl] = _rms_rows(_dot(h, w_ref[:, sl]), kg).astype(k_ref.dtype)
    v_ref[...] = _dot(h, w_ref[:, d:2 * d]).astype(v_ref.dtype)


def _mem_kv(mem2d, g, w_kv, k_norm_g, *, batch):
    d = XA_HEADS * XA_HD
    return pl.pallas_call(
        _mem_kv_kernel,
        grid=(batch,),
        in_specs=[pl.BlockSpec((N_MEM, D_MODEL), lambda b: (b, 0)),
                  pl.BlockSpec((1, D_MODEL), lambda b: (0, 0)),
                  pl.BlockSpec((D_MODEL, 2 * d), lambda b: (0, 0)),
                  pl.BlockSpec((1, XA_HD), lambda b: (0, 0))],
        out_specs=[pl.BlockSpec((N_MEM, d), lambda b: (b, 0)),
                   pl.BlockSpec((N_MEM, d), lambda b: (b, 0))],
        out_shape=[jax.ShapeDtypeStruct((batch * N_MEM, d), BF16),
                   jax.ShapeDtypeStruct((batch * N_MEM, d), BF16)],
        compiler_params=_params("parallel"),
        name="mem_kv",
    )(mem2d, g.astype(F32).reshape(1, D_MODEL), w_kv.astype(BF16), k_norm_g.astype(F32).reshape(1, XA_HD))


def _xattn_kernel(x_ref, g_ref, wq_ref, qg_ref, mk_ref, mv_ref, wo_ref, o_ref, att_ref, *, oc):
    h = _rms_rows(x_ref[...], g_ref[...]).astype(BF16)
    qg = qg_ref[...]
    sls = [slice(hd * XA_HD, (hd + 1) * XA_HD) for hd in range(XA_HEADS)]
    qs = [_dot(h, wq_ref[:, sl]) for sl in sls]
    qs = [_rms_rows(q, qg).astype(BF16) for q in qs]
    logits = [_dot_nt(q, mk_ref[:, sl]) * (XA_HD ** -0.5) for q, sl in zip(qs, sls)]
    ps = [jnp.exp(lg - jnp.max(lg, axis=-1, keepdims=True)) for lg in logits]
    ps = [(p / jnp.sum(p, axis=-1, keepdims=True)).astype(BF16) for p in ps]
    for p, sl in zip(ps, sls):
        att_ref[:, sl] = _dot(p, mv_ref[:, sl]).astype(att_ref.dtype)
    d = o_ref.shape[1]
    for n in range(d // oc):
        sl = slice(n * oc, (n + 1) * oc)
        o_ref[:, sl] = x_ref[:, sl] + _dot(att_ref[...], wo_ref[:, sl])


def _xattn(x2d, g, w_q, q_norm_g, mk, mv, w_o, *, seq, tm, oc=256):
    t, d = x2d.shape
    nt = seq // tm
    const = lambda shape: pl.BlockSpec(shape, lambda i: (0, 0))
    return pl.pallas_call(
        functools.partial(_xattn_kernel, oc=oc),
        grid=(t // tm,),
        in_specs=[pl.BlockSpec((tm, d), lambda i: (i, 0)),
                  const((1, d)), const((d, d)), const((1, XA_HD)),
                  pl.BlockSpec((N_MEM, d), lambda i: (i // nt, 0)),
                  pl.BlockSpec((N_MEM, d), lambda i: (i // nt, 0)),
                  const((d, d))],
        out_specs=pl.BlockSpec((tm, d), lambda i: (i, 0)),
        out_shape=jax.ShapeDtypeStruct((t, d), F32),
        scratch_shapes=[pltpu.VMEM((tm, d), BF16)],
        compiler_params=_params("parallel"),
        name="xattn",
    )(x2d, g.astype(F32).reshape(1, d), w_q.astype(BF16), q_norm_g.astype(F32).reshape(1, XA_HD),
      mk, mv, w_o.astype(BF16))


def _ffn_kernel(x_ref, g_ref, win_ref, cw_ref, cb_ref, wout_ref, o_ref, carry_ref, act_ref, *, nt, fc, oc):
    tm, d = x_ref.shape
    assert CONV_W == 3

    @pl.when(pl.program_id(0) % nt == 0)
    def _():
        carry_ref[...] = jnp.zeros_like(carry_ref)

    h = _rms_rows(x_ref[...], g_ref[...]).astype(BF16)
    row = lax.broadcasted_iota(jnp.int32, (8, fc), 0)
    for c in range(D_FF // fc):
        sl = slice(c * fc, (c + 1) * fc)
        gate = _dot(h, win_ref[:, sl])
        up = _dot(h, win_ref[:, D_FF + c * fc:D_FF + (c + 1) * fc])
        prev = carry_ref[c]
        carry_ref[c] = gate[tm - 8:tm, :]
        w0, w1, w2, bias = cw_ref[0:1, sl], cw_ref[1:2, sl], cw_ref[2:3, sl], cb_ref[:, sl]
        conv = pltpu.roll(gate, 2, axis=0) * w0 + pltpu.roll(gate, 1, axis=0) * w1 + gate * w2 + bias
        top = gate[0:8, :]
        t1 = jnp.where(row == 0, prev[7:8, :], pltpu.roll(top, 1, axis=0))
        t2 = jnp.where(row == 0, prev[6:7, :], jnp.where(row == 1, prev[7:8, :], pltpu.roll(top, 2, axis=0)))
        conv = jnp.concatenate([t2 * w0 + t1 * w1 + top * w2 + bias, conv[8:, :]], axis=0)
        act_ref[:, sl] = (jax.nn.gelu(conv) * up).astype(BF16)
    for n in range(d // oc):
        sl = slice(n * oc, (n + 1) * oc)
        o_ref[:, sl] = x_ref[:, sl] + _dot(act_ref[...], wout_ref[:, sl])


def _ffn(x2d, g, w_in, conv_w, conv_b, w_out, *, seq, tm, fc=256, oc=256):
    t, d = x2d.shape
    nt = seq // tm
    const = lambda shape: pl.BlockSpec(shape, lambda i: (0, 0))
    return pl.pallas_call(
        functools.partial(_ffn_kernel, nt=nt, fc=fc, oc=oc),
        grid=(t // tm,),
        in_specs=[pl.BlockSpec((tm, d), lambda i: (i, 0)),
                  const((1, d)), const((d, 2 * D_FF)), const((CONV_W, D_FF)), const((1, D_FF)),
                  const((D_FF, d))],
        out_specs=pl.BlockSpec((tm, d), lambda i: (i, 0)),
        out_shape=jax.ShapeDtypeStruct((t, d), F32),
        scratch_shapes=[pltpu.VMEM((D_FF // fc, 8, fc), F32), pltpu.VMEM((tm, D_FF), BF16)],
        compiler_params=_params("arbitrary"),
        name="conv_ffn",
    )(x2d, g.astype(F32).reshape(1, d), w_in.astype(BF16), conv_w.astype(F32),
      conv_b.astype(F32).reshape(1, D_FF), w_out.astype(BF16))


def kernel(x, mem, mix_norm_g, ev_w_in, ev_ret_norm_g, ev_hg_norm_g, hg_lb_logits, ev_w_out, od_w_in, od_q_norm_g, od_k_norm_g, rel_bias, od_w_out, xa_norm_g, xa_mem_norm_g, xa_w_q, xa_w_kv, xa_q_norm_g, xa_k_norm_g, xa_w_o, ffn_norm_g, ffn_w_in, ffn_conv_w, ffn_conv_b, ffn_w_out):
    batch, seq, d = x.shape
    depth = mix_norm_g.shape[0]
    x2d = x.reshape(batch * seq, d)
    mem2d = mem.reshape(batch * mem.shape[1], d)
    tm = 512
    for l in range(depth):
        if l % 2 == 0:
            e = l // 2
            proj = _rms_proj(x2d, mix_norm_g[l], ev_w_in[e], tm=tm)
            y_ret = _retention(proj, ev_ret_norm_g[e], batch=batch, seq=seq, tm=tm)
            y_hg = _hgrn(proj, hg_lb_logits, ev_hg_norm_g[e], layer=l, batch=batch, seq=seq, tm=2 * tm)
            x2d = _out_proj(x2d, [y_ret, y_hg], ev_w_out[e], tm=tm)
        else:
            o = l // 2
            qkv = _rms_proj(x2d, mix_norm_g[l], od_w_in[o], tm=tm, permute=True)
            att = _dilated_attention(qkv, od_q_norm_g[o], od_k_norm_g[o], rel_bias, batch=batch, seq=seq)
            x2d = _out_proj(x2d, [att], od_w_out[o], tm=tm, unpermute=True)
        mk, mv = _mem_kv(mem2d, xa_mem_norm_g[l], xa_w_kv[l], xa_k_norm_g[l], batch=batch)
        x2d = _xattn(x2d, xa_norm_g[l], xa_w_q[l], xa_q_norm_g[l], mk, mv, xa_w_o[l], seq=seq, tm=tm)
        x2d = _ffn(x2d, ffn_norm_g[l], ffn_w_in[l], ffn_conv_w[l], ffn_conv_b[l], ffn_w_out[l], seq=seq, tm=tm)
    return x2d.reshape(batch, seq, d)
```

```python
import functools
import math

import jax
import jax.numpy as jnp
import numpy as np
from jax import lax
from jax.experimental import pallas as pl
from jax.experimental.pallas import tpu as pltpu

F32 = jnp.float32
BF16 = jnp.bfloat16

D_MODEL = 1024
N_MEM = 256
EPS = 1e-6
HEAD_DIM = 128
RET_HEADS = 4
RET_CHUNK = 128
RET_GROUP = 4
ROPE_BASE = 10000.0
HG_HEADS = 4
HG_CHUNK = 64
F_FLOOR = 1e-6
DSA_HEADS = 8
DSA_BRANCHES = ((128, 1), (512, 4), (2048, 16))
DSA_BLOCK = 128
DSA_UNROLL = 8
REL_BUCKETS = 32
REL_MAX_DIST = 2048
XA_HEADS = 4
XA_HD = 256
D_FF = 2816
CONV_W = 3
EV_GROUP = RET_HEADS * HEAD_DIM

VMEM_LIMIT_BYTES = 56 * 1024 * 1024
NEG_INF = float("-inf")

_NT = (((1,), (1,)), ((), ()))


def _params(*sem):
    return pltpu.CompilerParams(dimension_semantics=sem, vmem_limit_bytes=VMEM_LIMIT_BYTES)


def _rms_rows(x, g):
    return x * lax.rsqrt(jnp.mean(x * x, axis=-1, keepdims=True) + EPS) * g


def _dot(a, b):
    return jnp.dot(a, b, preferred_element_type=F32)


def _dot_nt(a, b):
    return lax.dot_general(a, b, _NT, preferred_element_type=F32)


PERM_BLOCK = 128
PERM_RESIDUES = 16
PERM_RUN = PERM_BLOCK // PERM_RESIDUES


def _block_permutation():
    rho = np.arange(PERM_BLOCK)
    src = PERM_RESIDUES * (rho % PERM_RUN) + rho // PERM_RUN
    p = np.zeros((PERM_BLOCK, PERM_BLOCK), np.float32)
    p[rho, src] = 1.0
    return p


def _permute_rows(p, x):
    blocks = [_dot(p, x[b * PERM_BLOCK:(b + 1) * PERM_BLOCK, :]).astype(BF16)
              for b in range(x.shape[0] // PERM_BLOCK)]
    return jnp.concatenate(blocks, axis=0)


def _rms_proj_kernel(*refs, col_chunk, permute):
    if permute:
        x_ref, g_ref, w_ref, p_ref, o_ref = refs
    else:
        x_ref, g_ref, w_ref, o_ref = refs
    h = _rms_rows(x_ref[...], g_ref[...]).astype(BF16)
    if permute:
        h = _permute_rows(p_ref[...], h)
    n = o_ref.shape[1]
    for c in range(n // col_chunk):
        sl = slice(c * col_chunk, (c + 1) * col_chunk)
        o_ref[:, sl] = _dot(h, w_ref[:, sl]).astype(o_ref.dtype)


def _rms_proj(x2d, g, w, *, tm, out_dtype=F32, col_chunk=512, permute=False):
    t, d = x2d.shape
    n = w.shape[1]
    in_specs = [
        pl.BlockSpec((tm, d), lambda i: (i, 0)),
        pl.BlockSpec((1, d), lambda i: (0, 0)),
        pl.BlockSpec((d, n), lambda i: (0, 0)),
    ]
    args = [x2d, g.reshape(1, d).astype(F32), w.astype(BF16)]
    if permute:
        in_specs.append(pl.BlockSpec((PERM_BLOCK, PERM_BLOCK), lambda i: (0, 0)))
        args.append(jnp.asarray(_block_permutation(), BF16))
    return pl.pallas_call(
        functools.partial(_rms_proj_kernel, col_chunk=col_chunk, permute=permute),
        grid=(t // tm,),
        in_specs=in_specs,
        out_specs=pl.BlockSpec((tm, n), lambda i: (i, 0)),
        out_shape=jax.ShapeDtypeStruct((t, n), out_dtype),
        compiler_params=_params("parallel"),
        name="rms_proj",
    )(*args)


def _out_proj_kernel(*refs, n_y, unpermute):
    x_ref = refs[0]
    y_refs = refs[1:1 + n_y]
    w_refs = refs[1 + n_y:1 + 2 * n_y]
    o_ref = refs[-1]
    acc = x_ref[...]
    for y_ref, w_ref in zip(y_refs, w_refs):
        y = y_ref[...]
        if unpermute:
            y = _permute_rows(refs[1 + 2 * n_y][...], y)
        acc = acc + _dot(y, w_ref[...])
    o_ref[...] = acc


def _out_proj(x2d, ys, w, *, tm, unpermute=False):
    t, d = x2d.shape
    n_y = len(ys)
    ws, off = [], 0
    for y in ys:
        ws.append(w[off:off + y.shape[1]].astype(BF16))
        off += y.shape[1]
    in_specs = [pl.BlockSpec((tm, d), lambda i: (i, 0))]
    in_specs += [pl.BlockSpec((tm, y.shape[1]), lambda i: (i, 0)) for y in ys]
    in_specs += [pl.BlockSpec(wk.shape, lambda i: (0, 0)) for wk in ws]
    args = [x2d, *ys, *ws]
    if unpermute:
        in_specs.append(pl.BlockSpec((PERM_BLOCK, PERM_BLOCK), lambda i: (0, 0)))
        args.append(jnp.asarray(_block_permutation().T, BF16))
    return pl.pallas_call(
        functools.partial(_out_proj_kernel, n_y=n_y, unpermute=unpermute),
        grid=(t // tm,),
        in_specs=in_specs,
        out_specs=pl.BlockSpec((tm, d), lambda i: (i, 0)),
        out_shape=jax.ShapeDtypeStruct((t, d), F32),
        compiler_params=_params("parallel"),
        name="out_proj",
    )(*args)


def _retention_consts(seq):
    h = jnp.arange(RET_HEADS, dtype=F32)
    log_gamma = jnp.log(1.0 - jnp.exp2(-5.0 - h))
    c = RET_CHUNK
    idx = jnp.arange(c, dtype=F32)
    diff = idx[:, None] - idx[None, :]
    decay = jnp.where(diff >= 0, jnp.exp(log_gamma[:, None, None] * jnp.maximum(diff, 0.0)), 0.0)
    xi = jnp.exp(log_gamma[:, None] * (idx + 1.0))
    zeta = jnp.exp(log_gamma[:, None] * (c - 1.0 - idx))
    chunk_decay = jnp.exp(log_gamma * c)
    ones = jnp.ones((RET_HEADS, c, HEAD_DIM), F32)
    half = HEAD_DIM // 2
    inv = 1.0 / (ROPE_BASE ** (jnp.arange(half, dtype=F32) / half))
    ang = jnp.arange(seq, dtype=F32)[:, None] * inv[None, :]
    cos, sin = jnp.cos(ang), jnp.sin(ang)
    return dict(
        decay=decay,
        xi=xi[:, :, None] * ones,
        zeta=zeta[:, :, None] * ones,
        chunk_decay=chunk_decay[:, None, None] * jnp.ones((RET_HEADS, 8, HEAD_DIM), F32),
        cos=jnp.concatenate([cos, cos], axis=-1),
        sin=jnp.concatenate([-sin, sin], axis=-1),
    )


def _retention_kernel(q_ref, k_ref, v_ref, g_ref, cos_ref, sin_ref, decay_ref, xi_ref, zeta_ref,
                      cd_ref, ng_ref, o_ref, state_ref):
    @pl.when(pl.program_id(2) == 0)
    def _():
        state_ref[...] = jnp.zeros_like(state_ref)

    c = RET_CHUNK
    half = HEAD_DIM // 2
    decay = decay_ref[...]
    xi = xi_ref[...]
    zeta = zeta_ref[...]
    cd = cd_ref[0:1, :]
    ng = ng_ref[...]
    def rotary(x, cos, sin):
        return x * cos + pltpu.roll(x, half, axis=1) * sin

    def group(gi, carry):
        chunks = range(RET_GROUP)
        sls = [pl.ds(pl.multiple_of((gi * RET_GROUP + u) * c, c), c) for u in chunks]
        cs = [(cos_ref[sl, :], sin_ref[sl, :]) for sl in sls]
        qs = [rotary(q_ref[sl, :], cos, sin).astype(BF16) for sl, (cos, sin) in zip(sls, cs)]
        ks = [rotary(k_ref[sl, :], cos, sin) * (HEAD_DIM ** -0.5) for sl, (cos, sin) in zip(sls, cs)]
        vs = [v_ref[sl, :].astype(BF16) for sl in sls]
        scores = [(_dot_nt(q, k.astype(BF16)) * decay).astype(BF16) for q, k in zip(qs, ks)]
        inner = [_dot(s, v) for s, v in zip(scores, vs)]
        kvs = [_dot((k * zeta).T.astype(BF16), v) for k, v in zip(ks, vs)]
        state = state_ref[...]
        outs = []
        for u in chunks:
            outs.append(inner[u] + _dot(qs[u], state.astype(BF16)) * xi)
            state = kvs[u] + cd * state
        state_ref[...] = state
        for u in chunks:
            o = outs[u]
            mu = jnp.mean(o, axis=-1, keepdims=True)
            oc = o - mu
            var = jnp.mean(oc * oc, axis=-1, keepdims=True)
            y = oc * lax.rsqrt(var + EPS) * ng
            o_ref[sls[u], :] = (y * jax.nn.silu(g_ref[sls[u], :])).astype(o_ref.dtype)
        return carry

    assert (q_ref.shape[0] // c) % RET_GROUP == 0
    lax.fori_loop(0, q_ref.shape[0] // c // RET_GROUP, group, 0)


def _retention(proj, norm_g, *, batch, seq, tm):
    consts = _retention_consts(seq)
    nt = seq // tm
    hh = RET_HEADS

    def col(group):
        return pl.BlockSpec((tm, HEAD_DIM), lambda b, h, i: (b * nt + i, group * hh + h))

    def per_head(rows):
        return pl.BlockSpec((None, rows, HEAD_DIM), lambda b, h, i: (h, 0, 0))

    pos = pl.BlockSpec((tm, HEAD_DIM), lambda b, h, i: (i, 0))
    return pl.pallas_call(
        _retention_kernel,
        grid=(batch, hh, nt),
        in_specs=[col(0), col(1), col(2), col(3), pos, pos,
                  per_head(RET_CHUNK), per_head(RET_CHUNK), per_head(RET_CHUNK), per_head(8), per_head(1)],
        out_specs=pl.BlockSpec((tm, HEAD_DIM), lambda b, h, i: (b * nt + i, h)),
        out_shape=jax.ShapeDtypeStruct((batch * seq, EV_GROUP), BF16),
        scratch_shapes=[pltpu.VMEM((HEAD_DIM, HEAD_DIM), F32)],
        compiler_params=_params("parallel", "parallel", "arbitrary"),
        name="retention",
    )(proj, proj, proj, proj, consts["cos"], consts["sin"], consts["decay"], consts["xi"],
      consts["zeta"], consts["chunk_decay"], norm_g.astype(F32).reshape(hh, 1, HEAD_DIM))


HG_LEVELS = (64, 32, 16, 8)


HG_GROUP = 4


def _hgrn_consts():
    c = HG_CHUNK
    t = np.arange(c)[:, None]
    s = np.arange(c)[None, :]
    tril = (s <= t).astype(np.float32)
    masks = []
    for lvl, bs in enumerate(HG_LEVELS):
        same = (t // bs) == (s // bs)
        if lvl < len(HG_LEVELS) - 1:
            masks.append(same & (t % bs >= bs // 2) & (s % bs < bs // 2))
        else:
            masks.append(same & (s <= t))
    return jnp.asarray(tril, BF16), jnp.asarray(np.stack(masks).astype(np.float32))


def _split3(x):
    hi = x.astype(BF16)
    rest = x - hi.astype(F32)
    mid = rest.astype(BF16)
    lo = (rest - mid.astype(F32)).astype(BF16)
    return hi, mid, lo


def _block_reference_rows(cum, bs, row):
    pieces = [jnp.broadcast_to(cum[b0 + row:b0 + row + 1, :], (bs, cum.shape[1]))
              for b0 in range(0, cum.shape[0], bs)]
    return pieces[0] if len(pieces) == 1 else jnp.concatenate(pieces, axis=0)


def _hgrn_kernel(q_ref, f_ref, i_ref, g_ref, lbl_ref, tril_ref, mask_ref, ng_ref, o_ref, state_ref, *,
                 layer):
    @pl.when(pl.program_id(2) == 0)
    def _():
        state_ref[...] = jnp.zeros_like(state_ref)

    c = HG_CHUNK
    logits = lbl_ref[...]
    e = jnp.exp(logits - jnp.max(logits, axis=0, keepdims=True))
    lb = jnp.sum(e[:layer + 1, :], axis=0, keepdims=True) / jnp.sum(e, axis=0, keepdims=True)
    tril = tril_ref[...]
    masks = [mask_ref[l] > 0.5 for l in range(len(HG_LEVELS))]
    ng = ng_ref[...]

    def group(gi, carry):
        chunks = range(HG_GROUP)
        sls = [pl.ds(pl.multiple_of((gi * HG_GROUP + u) * c, c), c) for u in chunks]
        fs = [lb + (1.0 - lb) * jax.nn.sigmoid(f_ref[sl, :]) for sl in sls]
        keys = [1.0 - f for f in fs]
        splits = [_split3(jnp.log(jnp.maximum(f, F_FLOOR))) for f in fs]
        cums = [_dot(tril, hi) + (_dot(tril, mid) + _dot(tril, lo)) for hi, mid, lo in splits]
        qs = [q_ref[sl, :] for sl in sls]
        scores = [jnp.zeros((c, c), F32) for _ in chunks]
        for lvl, bs in enumerate(HG_LEVELS):
            for u in chunks:
                a = cums[u] - _block_reference_rows(cums[u], bs, bs // 2 - 1)
                if lvl < len(HG_LEVELS) - 1:
                    decay = jnp.exp(-jnp.abs(a))
                    qa = qs[u] * decay
                    ka = keys[u] * decay
                else:
                    qa = qs[u] * jnp.exp(a)
                    ka = keys[u] * jnp.exp(-a)
                scores[u] = jnp.where(masks[lvl], _dot_nt(qa.astype(BF16), ka.astype(BF16)), scores[u])
        vs = [i_ref[sl, :] for sl in sls]
        lasts = [cum[c - 1:c, :] for cum in cums]
        intra = [_dot(scores[u].astype(BF16), vs[u].astype(BF16)) for u in chunks]
        kvs = [_dot(vs[u].T.astype(BF16), (keys[u] * jnp.exp(lasts[u] - cums[u])).astype(BF16)) for u in chunks]
        qcs = [(qs[u] * jnp.exp(cums[u])).astype(BF16) for u in chunks]
        state_t = state_ref[...]
        outs = []
        for u in chunks:
            outs.append(intra[u] + _dot_nt(qcs[u], state_t.astype(BF16)))
            state_t = jnp.exp(lasts[u]) * state_t + kvs[u]
        state_ref[...] = state_t
        for u in chunks:
            out = outs[u]
            y = out * lax.rsqrt(jnp.mean(out * out, axis=-1, keepdims=True) + EPS) * ng
            o_ref[sls[u], :] = (y * jax.nn.silu(g_ref[sls[u], :])).astype(o_ref.dtype)
        return carry

    assert (q_ref.shape[0] // c) % HG_GROUP == 0
    lax.fori_loop(0, q_ref.shape[0] // c // HG_GROUP, group, 0)


def _hgrn(proj, lb_logits, norm_g, *, layer, batch, seq, tm):
    cum_w, masks = _hgrn_consts()
    nt = seq // tm
    hh = HG_HEADS
    slots = lb_logits.shape[0]
    lbl = lb_logits.astype(F32).reshape(slots, hh, HEAD_DIM).transpose(1, 0, 2)

    def col(group):
        return pl.BlockSpec((tm, HEAD_DIM), lambda b, h, i: (b * nt + i, group * hh + h))

    return pl.pallas_call(
        functools.partial(_hgrn_kernel, layer=layer),
        grid=(batch, hh, nt),
        in_specs=[col(4), col(5), col(6), col(7),
                  pl.BlockSpec((None, slots, HEAD_DIM), lambda b, h, i: (h, 0, 0)),
                  pl.BlockSpec(cum_w.shape, lambda b, h, i: (0, 0)),
                  pl.BlockSpec(masks.shape, lambda b, h, i: (0, 0, 0)),
                  pl.BlockSpec((None, 1, HEAD_DIM), lambda b, h, i: (h, 0, 0))],
        out_specs=pl.BlockSpec((tm, HEAD_DIM), lambda b, h, i: (b * nt + i, h)),
        out_shape=jax.ShapeDtypeStruct((batch * seq, EV_GROUP), BF16),
        scratch_shapes=[pltpu.VMEM((HEAD_DIM, HEAD_DIM), F32)],
        compiler_params=_params("parallel", "parallel", "arbitrary"),
        name="hgrn2",
    )(proj, proj, proj, proj, lbl, cum_w, masks, norm_g.astype(F32).reshape(hh, 1, HEAD_DIM))


def _t5_bucket(dist):
    exact = REL_BUCKETS // 2
    d = jnp.maximum(dist, 0)
    log_ratio = jnp.log(jnp.maximum(d, 1).astype(F32) / exact) / math.log(REL_MAX_DIST / exact)
    large = jnp.minimum(exact + (log_ratio * (REL_BUCKETS - exact)).astype(jnp.int32), REL_BUCKETS - 1)
    return jnp.where(d < exact, d, large)


def _dsa_row_order(dil):
    rho = np.arange(DSA_BLOCK)
    run = PERM_RUN * dil
    c, j, a = rho // run, (rho % run) // PERM_RUN, rho % PERM_RUN
    return (PERM_BLOCK // dil) * j + (PERM_RESIDUES // dil) * a + c


def _dsa_bucket_table():
    blk = DSA_BLOCK
    tabs = []
    for window, dil in DSA_BRANCHES:
        order = _dsa_row_order(dil)
        qi = jnp.asarray(order)[:, None]
        ki = jnp.asarray(np.concatenate([order, blk + order]))[None, :]
        delta = qi + blk - ki
        n_back = window // dil
        valid = (delta >= 0) & (delta <= n_back)
        tabs.append(jnp.where(valid, _t5_bucket(delta * dil), -1).astype(jnp.int32))
    return jnp.stack(tabs)


def _dsa_kernel(rb_ref, q_ref, k_ref, v_ref, qg_ref, kg_ref, bucket_ref, o_ref,
                qn_ref, kp_ref, vp_ref, m_ref, den_ref, num_ref, bias_ref, lg_ref, *, seq, pad_blocks):
    blk = DSA_BLOCK
    nj = seq // PERM_BLOCK
    tiled = (nj, PERM_RESIDUES, PERM_RUN, HEAD_DIM)
    head = pl.program_id(0)
    qn_ref[...] = _rms_rows(q_ref[...], qg_ref[...]).reshape(tiled)
    kp_ref[pad_blocks:pad_blocks + nj] = _rms_rows(k_ref[...], kg_ref[...]).reshape(tiled)
    vp_ref[pad_blocks:pad_blocks + nj] = v_ref[...].reshape(tiled)

    @pl.when(pl.program_id(1) == 0)
    def _():
        zeros = jnp.zeros((pad_blocks,) + tiled[1:], F32)
        kp_ref[0:pad_blocks] = zeros
        vp_ref[0:pad_blocks] = zeros
        first_block_keys = lax.broadcasted_iota(jnp.int32, (blk, 2 * blk), 1) >= blk
        for br in range(len(DSA_BRANCHES)):
            bucket = bucket_ref[br]
            bias = jnp.full(bucket.shape, NEG_INF, F32)
            for b in range(REL_BUCKETS):
                bias = jnp.where(bucket == b, rb_ref[b, head], bias)
            bias_ref[2 * br] = bias
            bias_ref[2 * br + 1] = jnp.where(first_block_keys, bias, NEG_INF)

    scale = HEAD_DIM ** -0.5

    def gather(ref, j0, r, dil, spans):
        if dil == 1:
            return ref[pl.ds(j0, spans)].reshape(spans * blk, HEAD_DIM)
        pieces = [ref[pl.ds(j0 + s * dil, dil), r + dil * c]
                  for s in range(spans) for c in range(PERM_RESIDUES // dil)]
        return jnp.concatenate(pieces, axis=0).reshape(spans * blk, HEAD_DIM)

    def scatter(ref, j0, r, dil, val):
        if dil == 1:
            ref[j0] = val.reshape(tiled[1:])
            return
        run = PERM_RUN * dil
        for c in range(PERM_RESIDUES // dil):
            ref[pl.ds(j0, dil), r + dil * c] = val[c * run:(c + 1) * run, :].reshape(dil, PERM_RUN, HEAD_DIM)

    trips = nj // DSA_UNROLL
    assert nj % DSA_UNROLL == 0 and all(dil <= pad_blocks for _, dil in DSA_BRANCHES)

    def block_coords(br, it, u):
        dil = DSA_BRANCHES[br][1]
        nb = nj // dil
        idx = it * DSA_UNROLL + u
        return idx // nb, idx % nb, dil

    def logits_stage(br, it, slot):
        for u in range(DSA_UNROLL):
            r, n, dil = block_coords(br, it, u)
            qb = gather(qn_ref, n * dil, r, dil, 1)
            kb = gather(kp_ref, pad_blocks + (n - 1) * dil, r, dil, 2)
            bias = bias_ref[2 * br + jnp.where(n == 0, 1, 0)]
            lg_ref[slot, u] = _dot_nt(qb.astype(BF16), kb.astype(BF16)) * scale + bias

    def softmax_stage(br, it, slot):
        blocks = range(DSA_UNROLL)
        coords = [block_coords(br, it, u) for u in blocks]
        logits = [lg_ref[slot, u] for u in blocks]
        ms = [jnp.max(lg, axis=-1, keepdims=True) for lg in logits]
        ps = [jnp.exp(lg - m) for lg, m in zip(logits, ms)]
        dens = [jnp.sum(p, axis=-1, keepdims=True) for p in ps]
        nums = [_dot(p.astype(BF16), gather(vp_ref, pad_blocks + (n - 1) * dil, r, dil, 2).astype(BF16))
                for p, (r, n, dil) in zip(ps, coords)]
        for (r, n, dil), m, den, num in zip(coords, ms, dens, nums):
            scatter(m_ref.at[br], n * dil, r, dil, jnp.broadcast_to(m, (blk, HEAD_DIM)))
            scatter(den_ref.at[br], n * dil, r, dil, jnp.broadcast_to(den, (blk, HEAD_DIM)))
            scatter(num_ref.at[br], n * dil, r, dil, num)

    n_br = len(DSA_BRANCHES)
    logits_stage(0, 0, 0)
    for br in range(n_br):
        first_slot = (br * trips) % 2

        def overlapped(it, carry, br=br, first_slot=first_slot):
            slot = (first_slot + it) % 2
            softmax_stage(br, it, slot)
            logits_stage(br, it + 1, 1 - slot)
            return carry

        lax.fori_loop(0, trips - 1, overlapped, 0)
        last_slot = (first_slot + trips - 1) % 2
        softmax_stage(br, trips - 1, last_slot)
        if br + 1 < n_br:
            logits_stage(br + 1, 0, 1 - last_slot)

    def merge(j, carry):
        n_br = len(DSA_BRANCHES)
        ms = [m_ref[br, j] for br in range(n_br)]
        m_all = functools.reduce(jnp.maximum, ms)
        ws = [jnp.exp(mi - m_all) for mi in ms]
        den = functools.reduce(lambda x, y: x + y, [ws[br] * den_ref[br, j] for br in range(n_br)])
        num = functools.reduce(lambda x, y: x + y, [ws[br] * num_ref[br, j] for br in range(n_br)])
        rows = pl.ds(pl.multiple_of(j * blk, blk), blk)
        o_ref[rows, :] = (num / den).reshape(blk, HEAD_DIM).astype(o_ref.dtype)
        return carry

    lax.fori_loop(0, nj, merge, 0, unroll=2)


def _dilated_attention(qkv, q_norm_g, k_norm_g, rel_bias, *, batch, seq):
    hh = DSA_HEADS
    nj = seq // PERM_BLOCK
    pad_blocks = max(dil for _, dil in DSA_BRANCHES)
    bucket = _dsa_bucket_table()
    tiled = (PERM_RESIDUES, PERM_RUN, HEAD_DIM)

    def col(group):
        return pl.BlockSpec((seq, HEAD_DIM), lambda h, b: (b, group * hh + h))

    gain = pl.BlockSpec((1, HEAD_DIM), lambda h, b: (0, 0))
    return pl.pallas_call(
        functools.partial(_dsa_kernel, seq=seq, pad_blocks=pad_blocks),
        grid=(hh, batch),
        in_specs=[pl.BlockSpec(memory_space=pltpu.SMEM),
                  col(0), col(1), col(2), gain, gain,
                  pl.BlockSpec(bucket.shape, lambda h, b: (0, 0, 0))],
        out_specs=pl.BlockSpec((seq, HEAD_DIM), lambda h, b: (b, h)),
        out_shape=jax.ShapeDtypeStruct((batch * seq, hh * HEAD_DIM), BF16),
        scratch_shapes=[pltpu.VMEM((nj,) + tiled, F32),
                        pltpu.VMEM((pad_blocks + nj,) + tiled, F32),
                        pltpu.VMEM((pad_blocks + nj,) + tiled, F32),
                        pltpu.VMEM((len(DSA_BRANCHES), nj) + tiled, F32),
                        pltpu.VMEM((len(DSA_BRANCHES), nj) + tiled, F32),
                        pltpu.VMEM((len(DSA_BRANCHES), nj) + tiled, F32),
                        pltpu.VMEM((2 * len(DSA_BRANCHES), DSA_BLOCK, 2 * DSA_BLOCK), F32),
                        pltpu.VMEM((2, DSA_UNROLL, DSA_BLOCK, 2 * DSA_BLOCK), F32)],
        compiler_params=_params("parallel", "arbitrary"),
        name="dilated_attention",
    )(rel_bias.astype(F32), qkv, qkv, qkv, q_norm_g.astype(F32).reshape(1, HEAD_DIM),
      k_norm_g.astype(F32).reshape(1, HEAD_DIM), bucket)


def _mem_kv_kernel(mem_ref, g_ref, w_ref, kg_ref, k_ref, v_ref):
    h = _rms_rows(mem_ref[...], g_ref[...]).astype(BF16)
    d = XA_HEADS * XA_HD
    kg = kg_ref[...]
    for hd in range(XA_HEADS):
        sl = slice(hd * XA_HD, (hd + 1) * XA_HD)
        k_ref[:, sl] = _rms_rows(_dot(h, w_ref[:, sl]), kg).astype(k_ref.dtype)
    v_ref[...] = _dot(h, w_ref[:, d:2 * d]).astype(v_ref.dtype)


def _mem_kv(mem2d, g, w_kv, k_norm_g, *, batch):
    d = XA_HEADS * XA_HD
    return pl.pallas_call(
        _mem_kv_kernel,
        grid=(batch,),
        in_specs=[pl.BlockSpec((N_MEM, D_MODEL), lambda b: (b, 0)),
                  pl.BlockSpec((1, D_MODEL), lambda b: (0, 0)),
                  pl.BlockSpec((D_MODEL, 2 * d), lambda b: (0, 0)),
                  pl.BlockSpec((1, XA_HD), lambda b: (0, 0))],
        out_specs=[pl.BlockSpec((N_MEM, d), lambda b: (b, 0)),
                   pl.BlockSpec((N_MEM, d), lambda b: (b, 0))],
        out_shape=[jax.ShapeDtypeStruct((batch * N_MEM, d), BF16),
                   jax.ShapeDtypeStruct((batch * N_MEM, d), BF16)],
        compiler_params=_params("parallel"),
        name="mem_kv",
    )(mem2d, g.astype(F32).reshape(1, D_MODEL), w_kv.astype(BF16), k_norm_g.astype(F32).reshape(1, XA_HD))


def _xattn_kernel(x_ref, g_ref, wq_ref, qg_ref, mk_ref, mv_ref, wo_ref, o_ref, att_ref, *, oc):
    h = _rms_rows(x_ref[...], g_ref[...]).astype(BF16)
    qg = qg_ref[...]
    sls = [slice(hd * XA_HD, (hd + 1) * XA_HD) for hd in range(XA_HEADS)]
    qs = [_dot(h, wq_ref[:, sl]) for sl in sls]
    qs = [_rms_rows(q, qg).astype(BF16) for q in qs]
    logits = [_dot_nt(q, mk_ref[:, sl]) * (XA_HD ** -0.5) for q, sl in zip(qs, sls)]
    ps = [jnp.exp(lg - jnp.max(lg, axis=-1, keepdims=True)) for lg in logits]
    ps = [(p / jnp.sum(p, axis=-1, keepdims=True)).astype(BF16) for p in ps]
    for p, sl in zip(ps, sls):
        att_ref[:, sl] = _dot(p, mv_ref[:, sl]).astype(att_ref.dtype)
    d = o_ref.shape[1]
    for n in range(d // oc):
        sl = slice(n * oc, (n + 1) * oc)
        o_ref[:, sl] = x_ref[:, sl] + _dot(att_ref[...], wo_ref[:, sl])


def _xattn(x2d, g, w_q, q_norm_g, mk, mv, w_o, *, seq, tm, oc=256):
    t, d = x2d.shape
    nt = seq // tm
    const = lambda shape: pl.BlockSpec(shape, lambda i: (0, 0))
    return pl.pallas_call(
        functools.partial(_xattn_kernel, oc=oc),
        grid=(t // tm,),
        in_specs=[pl.BlockSpec((tm, d), lambda i: (i, 0)),
                  const((1, d)), const((d, d)), const((1, XA_HD)),
                  pl.BlockSpec((N_MEM, d), lambda i: (i // nt, 0)),
                  pl.BlockSpec((N_MEM, d), lambda i: (i // nt, 0)),
                  const((d, d))],
        out_specs=pl.BlockSpec((tm, d), lambda i: (i, 0)),
        out_shape=jax.ShapeDtypeStruct((t, d), F32),
        scratch_shapes=[pltpu.VMEM((tm, d), BF16)],
        compiler_params=_params("parallel"),
        name="xattn",
    )(x2d, g.astype(F32).reshape(1, d), w_q.astype(BF16), q_norm_g.astype(F32).reshape(1, XA_HD),
      mk, mv, w_o.astype(BF16))


def _ffn_kernel(x_ref, g_ref, win_ref, cw_ref, cb_ref, wout_ref, o_ref, carry_ref, act_ref, *, nt, fc, oc):
    tm, d = x_ref.shape
    assert CONV_W == 3

    @pl.when(pl.program_id(0) % nt == 0)
    def _():
        carry_ref[...] = jnp.zeros_like(carry_ref)

    h = _rms_rows(x_ref[...], g_ref[...]).astype(BF16)
    row = lax.broadcasted_iota(jnp.int32, (8, fc), 0)
    for c in range(D_FF // fc):
        sl = slice(c * fc, (c + 1) * fc)
        gate = _dot(h, win_ref[:, sl])
        up = _dot(h, win_ref[:, D_FF + c * fc:D_FF + (c + 1) * fc])
        prev = carry_ref[c]
        carry_ref[c] = gate[tm - 8:tm, :]
        w0, w1, w2, bias = cw_ref[0:1, sl], cw_ref[1:2, sl], cw_ref[2:3, sl], cb_ref[:, sl]
        conv = pltpu.roll(gate, 2, axis=0) * w0 + pltpu.roll(gate, 1, axis=0) * w1 + gate * w2 + bias
        top = gate[0:8, :]
        t1 = jnp.where(row == 0, prev[7:8, :], pltpu.roll(top, 1, axis=0))
        t2 = jnp.where(row == 0, prev[6:7, :], jnp.where(row == 1, prev[7:8, :], pltpu.roll(top, 2, axis=0)))
        conv = jnp.concatenate([t2 * w0 + t1 * w1 + top * w2 + bias, conv[8:, :]], axis=0)
        act_ref[:, sl] = (jax.nn.gelu(conv) * up).astype(BF16)
    for n in range(d // oc):
        sl = slice(n * oc, (n + 1) * oc)
        o_ref[:, sl] = x_ref[:, sl] + _dot(act_ref[...], wout_ref[:, sl])


def _ffn(x2d, g, w_in, conv_w, conv_b, w_out, *, seq, tm, fc=256, oc=256):
    t, d = x2d.shape
    nt = seq // tm
    const = lambda shape: pl.BlockSpec(shape, lambda i: (0, 0))
    return pl.pallas_call(
        functools.partial(_ffn_kernel, nt=nt, fc=fc, oc=oc),
        grid=(t // tm,),
        in_specs=[pl.BlockSpec((tm, d), lambda i: (i, 0)),
                  const((1, d)), const((d, 2 * D_FF)), const((CONV_W, D_FF)), const((1, D_FF)),
                  const((D_FF, d))],
        out_specs=pl.BlockSpec((tm, d), lambda i: (i, 0)),
        out_shape=jax.ShapeDtypeStruct((t, d), F32),
        scratch_shapes=[pltpu.VMEM((D_FF // fc, 8, fc), F32), pltpu.VMEM((tm, D_FF), BF16)],
        compiler_params=_params("arbitrary"),
        name="conv_ffn",
    )(x2d, g.astype(F32).reshape(1, d), w_in.astype(BF16), conv_w.astype(F32),
      conv_b.astype(F32).reshape(1, D_FF), w_out.astype(BF16))


def kernel(x, mem, mix_norm_g, ev_w_in, ev_ret_norm_g, ev_hg_norm_g, hg_lb_logits, ev_w_out, od_w_in, od_q_norm_g, od_k_norm_g, rel_bias, od_w_out, xa_norm_g, xa_mem_norm_g, xa_w_q, xa_w_kv, xa_q_norm_g, xa_k_norm_g, xa_w_o, ffn_norm_g, ffn_w_in, ffn_conv_w, ffn_conv_b, ffn_w_out):
    batch, seq, d = x.shape
    depth = mix_norm_g.shape[0]
    x2d = x.reshape(batch * seq, d)
    mem2d = mem.reshape(batch * mem.shape[1], d)
    tm = 512
    for l in range(depth):
        if l % 2 == 0:
            e = l // 2
            proj = _rms_proj(x2d, mix_norm_g[l], ev_w_in[e], tm=tm)
            y_ret = _retention(proj, ev_ret_norm_g[e], batch=batch, seq=seq, tm=2 * tm)
            y_hg = _hgrn(proj, hg_lb_logits, ev_hg_norm_g[e], layer=l, batch=batch, seq=seq, tm=2 * tm)
            x2d = _out_proj(x2d, [y_ret, y_hg], ev_w_out[e], tm=tm)
        else:
            o = l // 2
            qkv = _rms_proj(x2d, mix_norm_g[l], od_w_in[o], tm=tm, permute=True)
            att = _dilated_attention(qkv, od_q_norm_g[o], od_k_norm_g[o], rel_bias, batch=batch, seq=seq)
            x2d = _out_proj(x2d, [att], od_w_out[o], tm=tm, unpermute=True)
        mk, mv = _mem_kv(mem2d, xa_mem_norm_g[l], xa_w_kv[l], xa_k_norm_g[l], batch=batch)
        x2d = _xattn(x2d, xa_norm_g[l], xa_w_q[l], xa_q_norm_g[l], mk, mv, xa_w_o[l], seq=seq, tm=tm)
        x2d = _ffn(x2d, ffn_norm_g[l], ffn_w_in[l], ffn_conv_w[l], ffn_conv_b[l], ffn_w_out[l], seq=seq, tm=tm)
    return x2d.reshape(batch, seq, d)
```

```python
import functools
import math

import jax
import jax.numpy as jnp
import numpy as np
from jax import lax
from jax.experimental import pallas as pl
from jax.experimental.pallas import tpu as pltpu

F32 = jnp.float32
BF16 = jnp.bfloat16

D_MODEL = 1024
N_MEM = 256
EPS = 1e-6
HEAD_DIM = 128
RET_HEADS = 4
RET_CHUNK = 128
RET_GROUP = 4
ROPE_BASE = 10000.0
HG_HEADS = 4
HG_CHUNK = 64
F_FLOOR = 1e-6
DSA_HEADS = 8
DSA_BRANCHES = ((128, 1), (512, 4), (2048, 16))
DSA_BLOCK = 128
DSA_UNROLL = 8
REL_BUCKETS = 32
REL_MAX_DIST = 2048
XA_HEADS = 4
XA_HD = 256
D_FF = 2816
CONV_W = 3
EV_GROUP = RET_HEADS * HEAD_DIM

VMEM_LIMIT_BYTES = 56 * 1024 * 1024
NEG_INF = float("-inf")

_NT = (((1,), (1,)), ((), ()))


def _params(*sem):
    return pltpu.CompilerParams(dimension_semantics=sem, vmem_limit_bytes=VMEM_LIMIT_BYTES)


def _rms_rows(x, g):
    return x * lax.rsqrt(jnp.mean(x * x, axis=-1, keepdims=True) + EPS) * g


def _dot(a, b):
    return jnp.dot(a, b, preferred_element_type=F32)


def _dot_nt(a, b):
    return lax.dot_general(a, b, _NT, preferred_element_type=F32)


PERM_BLOCK = 128
PERM_RESIDUES = 16
PERM_RUN = PERM_BLOCK // PERM_RESIDUES


def _block_permutation():
    rho = np.arange(PERM_BLOCK)
    src = PERM_RESIDUES * (rho % PERM_RUN) + rho // PERM_RUN
    p = np.zeros((PERM_BLOCK, PERM_BLOCK), np.float32)
    p[rho, src] = 1.0
    return p


def _permute_rows(p, x):
    blocks = [_dot(p, x[b * PERM_BLOCK:(b + 1) * PERM_BLOCK, :]).astype(BF16)
              for b in range(x.shape[0] // PERM_BLOCK)]
    return jnp.concatenate(blocks, axis=0)


def _rms_proj_kernel(*refs, col_chunk, permute):
    if permute:
        x_ref, g_ref, w_ref, p_ref, o_ref = refs
    else:
        x_ref, g_ref, w_ref, o_ref = refs
    h = _rms_rows(x_ref[...], g_ref[...]).astype(BF16)
    if permute:
        h = _permute_rows(p_ref[...], h)
    n = o_ref.shape[1]
    for c in range(n // col_chunk):
        sl = slice(c * col_chunk, (c + 1) * col_chunk)
        o_ref[:, sl] = _dot(h, w_ref[:, sl]).astype(o_ref.dtype)


def _rms_proj(x2d, g, w, *, tm, out_dtype=F32, col_chunk=512, permute=False):
    t, d = x2d.shape
    n = w.shape[1]
    in_specs = [
        pl.BlockSpec((tm, d), lambda i: (i, 0)),
        pl.BlockSpec((1, d), lambda i: (0, 0)),
        pl.BlockSpec((d, n), lambda i: (0, 0)),
    ]
    args = [x2d, g.reshape(1, d).astype(F32), w.astype(BF16)]
    if permute:
        in_specs.append(pl.BlockSpec((PERM_BLOCK, PERM_BLOCK), lambda i: (0, 0)))
        args.append(jnp.asarray(_block_permutation(), BF16))
    return pl.pallas_call(
        functools.partial(_rms_proj_kernel, col_chunk=col_chunk, permute=permute),
        grid=(t // tm,),
        in_specs=in_specs,
        out_specs=pl.BlockSpec((tm, n), lambda i: (i, 0)),
        out_shape=jax.ShapeDtypeStruct((t, n), out_dtype),
        compiler_params=_params("parallel"),
        name="rms_proj",
    )(*args)


def _retention_consts(seq):
    h = jnp.arange(RET_HEADS, dtype=F32)
    log_gamma = jnp.log(1.0 - jnp.exp2(-5.0 - h))
    c = RET_CHUNK
    idx = jnp.arange(c, dtype=F32)
    diff = idx[:, None] - idx[None, :]
    decay = jnp.where(diff >= 0, jnp.exp(log_gamma[:, None, None] * jnp.maximum(diff, 0.0)), 0.0)
    xi = jnp.exp(log_gamma[:, None] * (idx + 1.0))
    zeta = jnp.exp(log_gamma[:, None] * (c - 1.0 - idx))
    chunk_decay = jnp.exp(log_gamma * c)
    ones = jnp.ones((RET_HEADS, c, HEAD_DIM), F32)
    half = HEAD_DIM // 2
    inv = 1.0 / (ROPE_BASE ** (jnp.arange(half, dtype=F32) / half))
    ang = jnp.arange(seq, dtype=F32)[:, None] * inv[None, :]
    cos, sin = jnp.cos(ang), jnp.sin(ang)
    return dict(
        decay=decay,
        xi=xi[:, :, None] * ones,
        zeta=zeta[:, :, None] * ones,
        chunk_decay=chunk_decay[:, None, None] * jnp.ones((RET_HEADS, 8, HEAD_DIM), F32),
        cos=jnp.concatenate([cos, cos], axis=-1),
        sin=jnp.concatenate([-sin, sin], axis=-1),
    )


def _retention_kernel(q_ref, k_ref, v_ref, g_ref, cos_ref, sin_ref, decay_ref, xi_ref, zeta_ref,
                      cd_ref, ng_ref, o_ref, state_ref):
    @pl.when(pl.program_id(2) == 0)
    def _():
        state_ref[...] = jnp.zeros_like(state_ref)

    c = RET_CHUNK
    half = HEAD_DIM // 2
    decay = decay_ref[...]
    xi = xi_ref[...]
    zeta = zeta_ref[...]
    cd = cd_ref[0:1, :]
    ng = ng_ref[...]
    def rotary(x, cos, sin):
        return x * cos + pltpu.roll(x, half, axis=1) * sin

    def group(gi, carry):
        chunks = range(RET_GROUP)
        sls = [pl.ds(pl.multiple_of((gi * RET_GROUP + u) * c, c), c) for u in chunks]
        cs = [(cos_ref[sl, :], sin_ref[sl, :]) for sl in sls]
        qs = [rotary(q_ref[sl, :], cos, sin).astype(BF16) for sl, (cos, sin) in zip(sls, cs)]
        ks = [rotary(k_ref[sl, :], cos, sin) * (HEAD_DIM ** -0.5) for sl, (cos, sin) in zip(sls, cs)]
        vs = [v_ref[sl, :].astype(BF16) for sl in sls]
        scores = [(_dot_nt(q, k.astype(BF16)) * decay).astype(BF16) for q, k in zip(qs, ks)]
        inner = [_dot(s, v) for s, v in zip(scores, vs)]
        kvs = [_dot((k * zeta).T.astype(BF16), v) for k, v in zip(ks, vs)]
        state = state_ref[...]
        outs = []
        for u in chunks:
            outs.append(inner[u] + _dot(qs[u], state.astype(BF16)) * xi)
            state = kvs[u] + cd * state
        state_ref[...] = state
        for u in chunks:
            o = outs[u]
            mu = jnp.mean(o, axis=-1, keepdims=True)
            oc = o - mu
            var = jnp.mean(oc * oc, axis=-1, keepdims=True)
            y = oc * lax.rsqrt(var + EPS) * ng
            o_ref[sls[u], :] = (y * jax.nn.silu(g_ref[sls[u], :])).astype(o_ref.dtype)
        return carry

    assert (q_ref.shape[0] // c) % RET_GROUP == 0
    lax.fori_loop(0, q_ref.shape[0] // c // RET_GROUP, group, 0)


def _retention(proj, norm_g, *, batch, seq, tm):
    consts = _retention_consts(seq)
    nt = seq // tm
    hh = RET_HEADS

    def col(group):
        return pl.BlockSpec((tm, HEAD_DIM), lambda b, h, i: (b * nt + i, group * hh + h))

    def per_head(rows):
        return pl.BlockSpec((None, rows, HEAD_DIM), lambda b, h, i: (h, 0, 0))

    pos = pl.BlockSpec((tm, HEAD_DIM), lambda b, h, i: (i, 0))
    return pl.pallas_call(
        _retention_kernel,
        grid=(batch, hh, nt),
        in_specs=[col(0), col(1), col(2), col(3), pos, pos,
                  per_head(RET_CHUNK), per_head(RET_CHUNK), per_head(RET_CHUNK), per_head(8), per_head(1)],
        out_specs=pl.BlockSpec((tm, HEAD_DIM), lambda b, h, i: (b * nt + i, h)),
        out_shape=jax.ShapeDtypeStruct((batch * seq, EV_GROUP), BF16),
        scratch_shapes=[pltpu.VMEM((HEAD_DIM, HEAD_DIM), F32)],
        compiler_params=_params("parallel", "parallel", "arbitrary"),
        name="retention",
    )(proj, proj, proj, proj, consts["cos"], consts["sin"], consts["decay"], consts["xi"],
      consts["zeta"], consts["chunk_decay"], norm_g.astype(F32).reshape(hh, 1, HEAD_DIM))


HG_LEVELS = (64, 32, 16, 8)


HG_GROUP = 4


def _hgrn_consts():
    c = HG_CHUNK
    t = np.arange(c)[:, None]
    s = np.arange(c)[None, :]
    tril = (s <= t).astype(np.float32)
    masks = []
    for lvl, bs in enumerate(HG_LEVELS):
        same = (t // bs) == (s // bs)
        if lvl < len(HG_LEVELS) - 1:
            masks.append(same & (t % bs >= bs // 2) & (s % bs < bs // 2))
        else:
            masks.append(same & (s <= t))
    return jnp.asarray(tril, BF16), jnp.asarray(np.stack(masks).astype(np.float32))


def _split3(x):
    hi = x.astype(BF16)
    rest = x - hi.astype(F32)
    mid = rest.astype(BF16)
    lo = (rest - mid.astype(F32)).astype(BF16)
    return hi, mid, lo


def _block_reference_rows(cum, bs, row):
    pieces = [jnp.broadcast_to(cum[b0 + row:b0 + row + 1, :], (bs, cum.shape[1]))
              for b0 in range(0, cum.shape[0], bs)]
    return pieces[0] if len(pieces) == 1 else jnp.concatenate(pieces, axis=0)


def _hgrn_kernel(q_ref, f_ref, i_ref, g_ref, lbl_ref, tril_ref, mask_ref, ng_ref, o_ref, state_ref, *,
                 layer):
    @pl.when(pl.program_id(2) == 0)
    def _():
        state_ref[...] = jnp.zeros_like(state_ref)

    c = HG_CHUNK
    logits = lbl_ref[...]
    e = jnp.exp(logits - jnp.max(logits, axis=0, keepdims=True))
    lb = jnp.sum(e[:layer + 1, :], axis=0, keepdims=True) / jnp.sum(e, axis=0, keepdims=True)
    tril = tril_ref[...]
    masks = [mask_ref[l] > 0.5 for l in range(len(HG_LEVELS))]
    ng = ng_ref[...]

    def group(gi, carry):
        chunks = range(HG_GROUP)
        sls = [pl.ds(pl.multiple_of((gi * HG_GROUP + u) * c, c), c) for u in chunks]
        fs = [lb + (1.0 - lb) * jax.nn.sigmoid(f_ref[sl, :]) for sl in sls]
        keys = [1.0 - f for f in fs]
        splits = [_split3(jnp.log(jnp.maximum(f, F_FLOOR))) for f in fs]
        cums = [_dot(tril, hi) + (_dot(tril, mid) + _dot(tril, lo)) for hi, mid, lo in splits]
        qs = [q_ref[sl, :] for sl in sls]
        scores = [jnp.zeros((c, c), F32) for _ in chunks]
        for lvl, bs in enumerate(HG_LEVELS):
            for u in chunks:
                a = cums[u] - _block_reference_rows(cums[u], bs, bs // 2 - 1)
                if lvl < len(HG_LEVELS) - 1:
                    decay = jnp.exp(-jnp.abs(a))
                    qa = qs[u] * decay
                    ka = keys[u] * decay
                else:
                    qa = qs[u] * jnp.exp(a)
                    ka = keys[u] * jnp.exp(-a)
                scores[u] = jnp.where(masks[lvl], _dot_nt(qa.astype(BF16), ka.astype(BF16)), scores[u])
        vs = [i_ref[sl, :] for sl in sls]
        lasts = [cum[c - 1:c, :] for cum in cums]
        intra = [_dot(scores[u].astype(BF16), vs[u].astype(BF16)) for u in chunks]
        kvs = [_dot(vs[u].T.astype(BF16), (keys[u] * jnp.exp(lasts[u] - cums[u])).astype(BF16)) for u in chunks]
        qcs = [(qs[u] * jnp.exp(cums[u])).astype(BF16) for u in chunks]
        state_t = state_ref[...]
        outs = []
        for u in chunks:
            outs.append(intra[u] + _dot_nt(qcs[u], state_t.astype(BF16)))
            state_t = jnp.exp(lasts[u]) * state_t + kvs[u]
        state_ref[...] = state_t
        for u in chunks:
            out = outs[u]
            y = out * lax.rsqrt(jnp.mean(out * out, axis=-1, keepdims=True) + EPS) * ng
            o_ref[sls[u], :] = (y * jax.nn.silu(g_ref[sls[u], :])).astype(o_ref.dtype)
        return carry

    assert (q_ref.shape[0] // c) % HG_GROUP == 0
    lax.fori_loop(0, q_ref.shape[0] // c // HG_GROUP, group, 0)


def _hgrn(proj, lb_logits, norm_g, *, layer, batch, seq, tm):
    cum_w, masks = _hgrn_consts()
    nt = seq // tm
    hh = HG_HEADS
    slots = lb_logits.shape[0]
    lbl = lb_logits.astype(F32).reshape(slots, hh, HEAD_DIM).transpose(1, 0, 2)

    def col(group):
        return pl.BlockSpec((tm, HEAD_DIM), lambda b, h, i: (b * nt + i, group * hh + h))

    return pl.pallas_call(
        functools.partial(_hgrn_kernel, layer=layer),
        grid=(batch, hh, nt),
        in_specs=[col(4), col(5), col(6), col(7),
                  pl.BlockSpec((None, slots, HEAD_DIM), lambda b, h, i: (h, 0, 0)),
                  pl.BlockSpec(cum_w.shape, lambda b, h, i: (0, 0)),
                  pl.BlockSpec(masks.shape, lambda b, h, i: (0, 0, 0)),
                  pl.BlockSpec((None, 1, HEAD_DIM), lambda b, h, i: (h, 0, 0))],
        out_specs=pl.BlockSpec((tm, HEAD_DIM), lambda b, h, i: (b * nt + i, h)),
        out_shape=jax.ShapeDtypeStruct((batch * seq, EV_GROUP), BF16),
        scratch_shapes=[pltpu.VMEM((HEAD_DIM, HEAD_DIM), F32)],
        compiler_params=_params("parallel", "parallel", "arbitrary"),
        name="hgrn2",
    )(proj, proj, proj, proj, lbl, cum_w, masks, norm_g.astype(F32).reshape(hh, 1, HEAD_DIM))


def _t5_bucket(dist):
    exact = REL_BUCKETS // 2
    d = np.maximum(dist, 0)
    log_ratio = (np.log(np.maximum(d, 1).astype(np.float32) / np.float32(exact))
                 / np.float32(math.log(REL_MAX_DIST / exact)))
    large = np.minimum(exact + (log_ratio * np.float32(REL_BUCKETS - exact)).astype(np.int32), REL_BUCKETS - 1)
    return np.where(d < exact, d, large)


def _dsa_row_order(dil):
    rho = np.arange(DSA_BLOCK)
    run = PERM_RUN * dil
    c, j, a = rho // run, (rho % run) // PERM_RUN, rho % PERM_RUN
    return (PERM_BLOCK // dil) * j + (PERM_RESIDUES // dil) * a + c


def _dsa_bucket_table():
    blk = DSA_BLOCK
    tabs = []
    for window, dil in DSA_BRANCHES:
        order = _dsa_row_order(dil)
        qi = order[:, None]
        ki = np.concatenate([order, blk + order])[None, :]
        delta = qi + blk - ki
        n_back = window // dil
        valid = (delta >= 0) & (delta <= n_back)
        tabs.append(np.where(valid, _t5_bucket(delta * dil), -1).astype(np.int32))
    return jnp.asarray(np.stack(tabs))


def _dsa_kernel(rb_ref, q_ref, k_ref, v_ref, qg_ref, kg_ref, bucket_ref, o_ref,
                qn_ref, kp_ref, vp_ref, m_ref, den_ref, num_ref, bias_ref, lg_ref, *, seq, pad_blocks):
    blk = DSA_BLOCK
    nj = seq // PERM_BLOCK
    tiled = (nj, PERM_RESIDUES, PERM_RUN, HEAD_DIM)
    head = pl.program_id(0)
    qn_ref[...] = _rms_rows(q_ref[...], qg_ref[...]).reshape(tiled)
    kp_ref[pad_blocks:pad_blocks + nj] = _rms_rows(k_ref[...], kg_ref[...]).reshape(tiled)
    vp_ref[pad_blocks:pad_blocks + nj] = v_ref[...].reshape(tiled)

    @pl.when(pl.program_id(1) == 0)
    def _():
        zeros = jnp.zeros((pad_blocks,) + tiled[1:], F32)
        kp_ref[0:pad_blocks] = zeros
        vp_ref[0:pad_blocks] = zeros
        first_block_keys = lax.broadcasted_iota(jnp.int32, (blk, 2 * blk), 1) >= blk
        for br in range(len(DSA_BRANCHES)):
            bucket = bucket_ref[br]
            bias = jnp.full(bucket.shape, NEG_INF, F32)
            for b in range(REL_BUCKETS):
                bias = jnp.where(bucket == b, rb_ref[b, head], bias)
            bias_ref[2 * br] = bias
            bias_ref[2 * br + 1] = jnp.where(first_block_keys, bias, NEG_INF)

    scale = HEAD_DIM ** -0.5

    def gather(ref, j0, r, dil, spans):
        if dil == 1:
            return ref[pl.ds(j0, spans)].reshape(spans * blk, HEAD_DIM)
        pieces = [ref[pl.ds(j0 + s * dil, dil), r + dil * c]
                  for s in range(spans) for c in range(PERM_RESIDUES // dil)]
        return jnp.concatenate(pieces, axis=0).reshape(spans * blk, HEAD_DIM)

    def scatter(ref, j0, r, dil, val):
        if dil == 1:
            ref[j0] = val.reshape(tiled[1:])
            return
        run = PERM_RUN * dil
        for c in range(PERM_RESIDUES // dil):
            ref[pl.ds(j0, dil), r + dil * c] = val[c * run:(c + 1) * run, :].reshape(dil, PERM_RUN, HEAD_DIM)

    trips = nj // DSA_UNROLL
    assert nj % DSA_UNROLL == 0 and all(dil <= pad_blocks for _, dil in DSA_BRANCHES)

    def block_coords(br, it, u):
        dil = DSA_BRANCHES[br][1]
        nb = nj // dil
        idx = it * DSA_UNROLL + u
        return idx // nb, idx % nb, dil

    def logits_stage(br, it, slot):
        for u in range(DSA_UNROLL):
            r, n, dil = block_coords(br, it, u)
            qb = gather(qn_ref, n * dil, r, dil, 1)
            kb = gather(kp_ref, pad_blocks + (n - 1) * dil, r, dil, 2)
            bias = bias_ref[2 * br + jnp.where(n == 0, 1, 0)]
            lg_ref[slot, u] = _dot_nt(qb.astype(BF16), kb.astype(BF16)) * scale + bias

    def softmax_stage(br, it, slot):
        blocks = range(DSA_UNROLL)
        coords = [block_coords(br, it, u) for u in blocks]
        logits = [lg_ref[slot, u] for u in blocks]
        ms = [jnp.max(lg, axis=-1, keepdims=True) for lg in logits]
        ps = [jnp.exp(lg - m) for lg, m in zip(logits, ms)]
        dens = [jnp.sum(p, axis=-1, keepdims=True) for p in ps]
        nums = [_dot(p.astype(BF16), gather(vp_ref, pad_blocks + (n - 1) * dil, r, dil, 2).astype(BF16))
                for p, (r, n, dil) in zip(ps, coords)]
        for (r, n, dil), m, den, num in zip(coords, ms, dens, nums):
            scatter(m_ref.at[br], n * dil, r, dil, jnp.broadcast_to(m, (blk, HEAD_DIM)))
            scatter(den_ref.at[br], n * dil, r, dil, jnp.broadcast_to(den, (blk, HEAD_DIM)))
            scatter(num_ref.at[br], n * dil, r, dil, num)

    n_br = len(DSA_BRANCHES)
    logits_stage(0, 0, 0)
    for br in range(n_br):
        first_slot = (br * trips) % 2

        def overlapped(it, carry, br=br, first_slot=first_slot):
            slot = (first_slot + it) % 2
            softmax_stage(br, it, slot)
            logits_stage(br, it + 1, 1 - slot)
            return carry

        lax.fori_loop(0, trips - 1, overlapped, 0)
        last_slot = (first_slot + trips - 1) % 2
        softmax_stage(br, trips - 1, last_slot)
        if br + 1 < n_br:
            logits_stage(br + 1, 0, 1 - last_slot)

    def merge(j, carry):
        n_br = len(DSA_BRANCHES)
        ms = [m_ref[br, j] for br in range(n_br)]
        m_all = functools.reduce(jnp.maximum, ms)
        ws = [jnp.exp(mi - m_all) for mi in ms]
        den = functools.reduce(lambda x, y: x + y, [ws[br] * den_ref[br, j] for br in range(n_br)])
        num = functools.reduce(lambda x, y: x + y, [ws[br] * num_ref[br, j] for br in range(n_br)])
        rows = pl.ds(pl.multiple_of(j * blk, blk), blk)
        o_ref[rows, :] = (num / den).reshape(blk, HEAD_DIM).astype(o_ref.dtype)
        return carry

    lax.fori_loop(0, nj, merge, 0, unroll=2)


def _dilated_attention(qkv, q_norm_g, k_norm_g, rel_bias, *, batch, seq):
    hh = DSA_HEADS
    nj = seq // PERM_BLOCK
    pad_blocks = max(dil for _, dil in DSA_BRANCHES)
    bucket = _dsa_bucket_table()
    tiled = (PERM_RESIDUES, PERM_RUN, HEAD_DIM)

    def col(group):
        return pl.BlockSpec((seq, HEAD_DIM), lambda h, b: (b, group * hh + h))

    gain = pl.BlockSpec((1, HEAD_DIM), lambda h, b: (0, 0))
    return pl.pallas_call(
        functools.partial(_dsa_kernel, seq=seq, pad_blocks=pad_blocks),
        grid=(hh, batch),
        in_specs=[pl.BlockSpec(memory_space=pltpu.SMEM),
                  col(0), col(1), col(2), gain, gain,
                  pl.BlockSpec(bucket.shape, lambda h, b: (0, 0, 0))],
        out_specs=pl.BlockSpec((seq, HEAD_DIM), lambda h, b: (b, h)),
        out_shape=jax.ShapeDtypeStruct((batch * seq, hh * HEAD_DIM), BF16),
        scratch_shapes=[pltpu.VMEM((nj,) + tiled, F32),
                        pltpu.VMEM((pad_blocks + nj,) + tiled, F32),
                        pltpu.VMEM((pad_blocks + nj,) + tiled, F32),
                        pltpu.VMEM((len(DSA_BRANCHES), nj) + tiled, F32),
                        pltpu.VMEM((len(DSA_BRANCHES), nj) + tiled, F32),
                        pltpu.VMEM((len(DSA_BRANCHES), nj) + tiled, F32),
                        pltpu.VMEM((2 * len(DSA_BRANCHES), DSA_BLOCK, 2 * DSA_BLOCK), F32),
                        pltpu.VMEM((2, DSA_UNROLL, DSA_BLOCK, 2 * DSA_BLOCK), F32)],
        compiler_params=_params("parallel", "arbitrary"),
        name="dilated_attention",
    )(rel_bias.astype(F32), qkv, qkv, qkv, q_norm_g.astype(F32).reshape(1, HEAD_DIM),
      k_norm_g.astype(F32).reshape(1, HEAD_DIM), bucket)


def _mem_kv_kernel(mem_ref, g_ref, w_ref, kg_ref, k_ref, v_ref):
    h = _rms_rows(mem_ref[...], g_ref[...]).astype(BF16)
    d = XA_HEADS * XA_HD
    kg = kg_ref[...]
    for hd in range(XA_HEADS):
        sl = slice(hd * XA_HD, (hd + 1) * XA_HD)
        k_ref[:, sl] = _rms_rows(_dot(h, w_ref[:, sl]), kg).astype(k_ref.dtype)
    v_ref[...] = _dot(h, w_ref[:, d:2 * d]).astype(v_ref.dtype)


def _mem_kv(mem2d, g, w_kv, k_norm_g, *, batch):
    d = XA_HEADS * XA_HD
    return pl.pallas_call(
        _mem_kv_kernel,
        grid=(batch,),
        in_specs=[pl.BlockSpec((N_MEM, D_MODEL), lambda b: (b, 0)),
                  pl.BlockSpec((1, D_MODEL), lambda b: (0, 0)),
                  pl.BlockSpec((D_MODEL, 2 * d), lambda b: (0, 0)),
                  pl.BlockSpec((1, XA_HD), lambda b: (0, 0))],
        out_specs=[pl.BlockSpec((N_MEM, d), lambda b: (b, 0)),
                   pl.BlockSpec((N_MEM, d), lambda b: (b, 0))],
        out_shape=[jax.ShapeDtypeStruct((batch * N_MEM, d), BF16),
                   jax.ShapeDtypeStruct((batch * N_MEM, d), BF16)],
        compiler_params=_params("parallel"),
        name="mem_kv",
    )(mem2d, g.astype(F32).reshape(1, D_MODEL), w_kv.astype(BF16), k_norm_g.astype(F32).reshape(1, XA_HD))


def _xattn_kernel(*refs, n_y, unpermute, oc):
    xin_ref = refs[0]
    y_refs = refs[1:1 + n_y]
    wmix_refs = refs[1 + n_y:1 + 2 * n_y]
    rest = refs[1 + 2 * n_y:]
    if unpermute:
        p_ref, rest = rest[0], rest[1:]
    g_ref, wq_ref, qg_ref, mk_ref, mv_ref, wo_ref, o_ref, x_ref, att_ref = rest
    d = o_ref.shape[1]
    ys = [y_ref[...] for y_ref in y_refs]
    if unpermute:
        ys = [_permute_rows(p_ref[...], y) for y in ys]
    for n in range(d // oc):
        sl = slice(n * oc, (n + 1) * oc)
        acc = xin_ref[:, sl]
        for y, w_ref in zip(ys, wmix_refs):
            acc = acc + _dot(y, w_ref[:, sl])
        x_ref[:, sl] = acc

    h = _rms_rows(x_ref[...], g_ref[...]).astype(BF16)
    qg = qg_ref[...]
    sls = [slice(hd * XA_HD, (hd + 1) * XA_HD) for hd in range(XA_HEADS)]
    qs = [_dot(h, wq_ref[:, sl]) for sl in sls]
    qs = [_rms_rows(q, qg).astype(BF16) for q in qs]
    logits = [_dot_nt(q, mk_ref[:, sl]) * (XA_HD ** -0.5) for q, sl in zip(qs, sls)]
    ps = [jnp.exp(lg - jnp.max(lg, axis=-1, keepdims=True)) for lg in logits]
    ps = [(p / jnp.sum(p, axis=-1, keepdims=True)).astype(BF16) for p in ps]
    for p, sl in zip(ps, sls):
        att_ref[:, sl] = _dot(p, mv_ref[:, sl]).astype(att_ref.dtype)
    for n in range(d // oc):
        sl = slice(n * oc, (n + 1) * oc)
        o_ref[:, sl] = x_ref[:, sl] + _dot(att_ref[...], wo_ref[:, sl])


def _mix_out_xattn(x2d, ys, w_mix, g, w_q, q_norm_g, mk, mv, w_o, *, seq, tm, unpermute=False, oc=256):
    t, d = x2d.shape
    nt = seq // tm
    n_y = len(ys)
    ws, off = [], 0
    for y in ys:
        ws.append(w_mix[off:off + y.shape[1]].astype(BF16))
        off += y.shape[1]
    const = lambda shape: pl.BlockSpec(shape, lambda i: (0, 0))
    in_specs = [pl.BlockSpec((tm, d), lambda i: (i, 0))]
    in_specs += [pl.BlockSpec((tm, y.shape[1]), lambda i: (i, 0)) for y in ys]
    in_specs += [const(wk.shape) for wk in ws]
    args = [x2d, *ys, *ws]
    if unpermute:
        in_specs.append(const((PERM_BLOCK, PERM_BLOCK)))
        args.append(jnp.asarray(_block_permutation().T, BF16))
    in_specs += [const((1, d)), const((d, d)), const((1, XA_HD)),
                 pl.BlockSpec((N_MEM, d), lambda i: (i // nt, 0)),
                 pl.BlockSpec((N_MEM, d), lambda i: (i // nt, 0)),
                 const((d, d))]
    args += [g.astype(F32).reshape(1, d), w_q.astype(BF16), q_norm_g.astype(F32).reshape(1, XA_HD),
             mk, mv, w_o.astype(BF16)]
    return pl.pallas_call(
        functools.partial(_xattn_kernel, n_y=n_y, unpermute=unpermute, oc=oc),
        grid=(t // tm,),
        in_specs=in_specs,
        out_specs=pl.BlockSpec((tm, d), lambda i: (i, 0)),
        out_shape=jax.ShapeDtypeStruct((t, d), F32),
        scratch_shapes=[pltpu.VMEM((tm, d), F32), pltpu.VMEM((tm, d), BF16)],
        compiler_params=_params("parallel"),
        name="mix_out_xattn",
    )(*args)


def _ffn_kernel(x_ref, g_ref, win_ref, cw_ref, cb_ref, wout_ref, o_ref, carry_ref, act_ref, *, nt, fc, oc):
    tm, d = x_ref.shape
    assert CONV_W == 3

    @pl.when(pl.program_id(0) % nt == 0)
    def _():
        carry_ref[...] = jnp.zeros_like(carry_ref)

    h = _rms_rows(x_ref[...], g_ref[...]).astype(BF16)
    row = lax.broadcasted_iota(jnp.int32, (8, fc), 0)
    for c in range(D_FF // fc):
        sl = slice(c * fc, (c + 1) * fc)
        gate = _dot(h, win_ref[:, sl])
        up = _dot(h, win_ref[:, D_FF + c * fc:D_FF + (c + 1) * fc])
        prev = carry_ref[c]
        carry_ref[c] = gate[tm - 8:tm, :]
        w0, w1, w2, bias = cw_ref[0:1, sl], cw_ref[1:2, sl], cw_ref[2:3, sl], cb_ref[:, sl]
        conv = pltpu.roll(gate, 2, axis=0) * w0 + pltpu.roll(gate, 1, axis=0) * w1 + gate * w2 + bias
        top = gate[0:8, :]
        t1 = jnp.where(row == 0, prev[7:8, :], pltpu.roll(top, 1, axis=0))
        t2 = jnp.where(row == 0, prev[6:7, :], jnp.where(row == 1, prev[7:8, :], pltpu.roll(top, 2, axis=0)))
        conv = jnp.concatenate([t2 * w0 + t1 * w1 + top * w2 + bias, conv[8:, :]], axis=0)
        act_ref[:, sl] = (jax.nn.gelu(conv) * up).astype(BF16)
    for n in range(d // oc):
        sl = slice(n * oc, (n + 1) * oc)
        o_ref[:, sl] = x_ref[:, sl] + _dot(act_ref[...], wout_ref[:, sl])


def _ffn(x2d, g, w_in, conv_w, conv_b, w_out, *, seq, tm, fc=256, oc=256):
    t, d = x2d.shape
    nt = seq // tm
    const = lambda shape: pl.BlockSpec(shape, lambda i: (0, 0))
    return pl.pallas_call(
        functools.partial(_ffn_kernel, nt=nt, fc=fc, oc=oc),
        grid=(t // tm,),
        in_specs=[pl.BlockSpec((tm, d), lambda i: (i, 0)),
                  const((1, d)), const((d, 2 * D_FF)), const((CONV_W, D_FF)), const((1, D_FF)),
                  const((D_FF, d))],
        out_specs=pl.BlockSpec((tm, d), lambda i: (i, 0)),
        out_shape=jax.ShapeDtypeStruct((t, d), F32),
        scratch_shapes=[pltpu.VMEM((D_FF // fc, 8, fc), F32), pltpu.VMEM((tm, D_FF), BF16)],
        compiler_params=_params("arbitrary"),
        name="conv_ffn",
    )(x2d, g.astype(F32).reshape(1, d), w_in.astype(BF16), conv_w.astype(F32),
      conv_b.astype(F32).reshape(1, D_FF), w_out.astype(BF16))


def kernel(x, mem, mix_norm_g, ev_w_in, ev_ret_norm_g, ev_hg_norm_g, hg_lb_logits, ev_w_out, od_w_in, od_q_norm_g, od_k_norm_g, rel_bias, od_w_out, xa_norm_g, xa_mem_norm_g, xa_w_q, xa_w_kv, xa_q_norm_g, xa_k_norm_g, xa_w_o, ffn_norm_g, ffn_w_in, ffn_conv_w, ffn_conv_b, ffn_w_out):
    batch, seq, d = x.shape
    depth = mix_norm_g.shape[0]
    x2d = x.reshape(batch * seq, d)
    mem2d = mem.reshape(batch * mem.shape[1], d)
    tm = 512
    for l in range(depth):
        if l % 2 == 0:
            e = l // 2
            proj = _rms_proj(x2d, mix_norm_g[l], ev_w_in[e], tm=tm)
            y_ret = _retention(proj, ev_ret_norm_g[e], batch=batch, seq=seq, tm=2 * tm)
            y_hg = _hgrn(proj, hg_lb_logits, ev_hg_norm_g[e], layer=l, batch=batch, seq=seq, tm=2 * tm)
            ys, w_mix, permuted = [y_ret, y_hg], ev_w_out[e], False
        else:
            o = l // 2
            qkv = _rms_proj(x2d, mix_norm_g[l], od_w_in[o], tm=tm, permute=True)
            att = _dilated_attention(qkv, od_q_norm_g[o], od_k_norm_g[o], rel_bias, batch=batch, seq=seq)
            ys, w_mix, permuted = [att], od_w_out[o], True
        mk, mv = _mem_kv(mem2d, xa_mem_norm_g[l], xa_w_kv[l], xa_k_norm_g[l], batch=batch)
        x2d = _mix_out_xattn(x2d, ys, w_mix, xa_norm_g[l], xa_w_q[l], xa_q_norm_g[l], mk, mv, xa_w_o[l],
                             seq=seq, tm=tm, unpermute=permuted)
        x2d = _ffn(x2d, ffn_norm_g[l], ffn_w_in[l], ffn_conv_w[l], ffn_conv_b[l], ffn_w_out[l], seq=seq, tm=tm)
    return x2d.reshape(batch, seq, d)
```

```python
import functools
import math

import jax
import jax.numpy as jnp
import numpy as np
from jax import lax
from jax.experimental import pallas as pl
from jax.experimental.pallas import tpu as pltpu

F32 = jnp.float32
BF16 = jnp.bfloat16

D_MODEL = 1024
N_MEM = 256
EPS = 1e-6
HEAD_DIM = 128
RET_HEADS = 4
RET_CHUNK = 128
RET_GROUP = 8
ROPE_BASE = 10000.0
HG_HEADS = 4
HG_CHUNK = 64
F_FLOOR = 1e-6
DSA_HEADS = 8
DSA_BRANCHES = ((128, 1), (512, 4), (2048, 16))
DSA_BLOCK = 128
DSA_UNROLL = 8
REL_BUCKETS = 32
REL_MAX_DIST = 2048
XA_HEADS = 4
XA_HD = 256
D_FF = 2816
CONV_W = 3
EV_GROUP = RET_HEADS * HEAD_DIM

VMEM_LIMIT_BYTES = 56 * 1024 * 1024
NEG_INF = float("-inf")

_NT = (((1,), (1,)), ((), ()))


def _params(*sem):
    return pltpu.CompilerParams(dimension_semantics=sem, vmem_limit_bytes=VMEM_LIMIT_BYTES)


def _rms_rows(x, g):
    return x * lax.rsqrt(jnp.mean(x * x, axis=-1, keepdims=True) + EPS) * g


def _dot(a, b):
    return jnp.dot(a, b, preferred_element_type=F32)


def _dot_nt(a, b):
    return lax.dot_general(a, b, _NT, preferred_element_type=F32)


PERM_BLOCK = 128
PERM_RESIDUES = 16
PERM_RUN = PERM_BLOCK // PERM_RESIDUES


def _block_permutation():
    rho = np.arange(PERM_BLOCK)
    src = PERM_RESIDUES * (rho % PERM_RUN) + rho // PERM_RUN
    p = np.zeros((PERM_BLOCK, PERM_BLOCK), np.float32)
    p[rho, src] = 1.0
    return p


def _permute_rows(p, x):
    blocks = [_dot(p, x[b * PERM_BLOCK:(b + 1) * PERM_BLOCK, :]).astype(BF16)
              for b in range(x.shape[0] // PERM_BLOCK)]
    return jnp.concatenate(blocks, axis=0)


def _rms_proj_kernel(*refs, col_chunk, permute, n_head_norm):
    x_ref, g_ref, w_ref = refs[:3]
    o_ref = refs[-1]
    extras = list(refs[3:-1])
    p_ref = extras.pop(0) if permute else None
    hg_ref = extras.pop(0) if n_head_norm else None
    h = _rms_rows(x_ref[...], g_ref[...]).astype(BF16)
    if permute:
        h = _permute_rows(p_ref[...], h)
    n = o_ref.shape[1]
    for c in range(n // col_chunk):
        sl = slice(c * col_chunk, (c + 1) * col_chunk)
        y = _dot(h, w_ref[:, sl])
        if c * col_chunk < n_head_norm:
            assert (c + 1) * col_chunk <= n_head_norm
            heads = [_rms_rows(y[:, k * HEAD_DIM:(k + 1) * HEAD_DIM],
                               hg_ref[:, c * col_chunk + k * HEAD_DIM:c * col_chunk + (k + 1) * HEAD_DIM])
                     for k in range(col_chunk // HEAD_DIM)]
            y = jnp.concatenate(heads, axis=1)
        o_ref[:, sl] = y.astype(o_ref.dtype)


def _rms_proj(x2d, g, w, *, tm, out_dtype=F32, col_chunk=512, permute=False, head_gains=None):
    t, d = x2d.shape
    n = w.shape[1]
    in_specs = [
        pl.BlockSpec((tm, d), lambda i: (i, 0)),
        pl.BlockSpec((1, d), lambda i: (0, 0)),
        pl.BlockSpec((d, n), lambda i: (0, 0)),
    ]
    args = [x2d, g.reshape(1, d).astype(F32), w.astype(BF16)]
    if permute:
        in_specs.append(pl.BlockSpec((PERM_BLOCK, PERM_BLOCK), lambda i: (0, 0)))
        args.append(jnp.asarray(_block_permutation(), BF16))
    n_head_norm = 0 if head_gains is None else head_gains.shape[0]
    if n_head_norm:
        in_specs.append(pl.BlockSpec((1, n_head_norm), lambda i: (0, 0)))
        args.append(head_gains.astype(F32).reshape(1, n_head_norm))
    return pl.pallas_call(
        functools.partial(_rms_proj_kernel, col_chunk=col_chunk, permute=permute, n_head_norm=n_head_norm),
        grid=(t // tm,),
        in_specs=in_specs,
        out_specs=pl.BlockSpec((tm, n), lambda i: (i, 0)),
        out_shape=jax.ShapeDtypeStruct((t, n), out_dtype),
        compiler_params=_params("parallel"),
        name="rms_proj",
    )(*args)


def _retention_consts(seq):
    h = jnp.arange(RET_HEADS, dtype=F32)
    log_gamma = jnp.log(1.0 - jnp.exp2(-5.0 - h))
    c = RET_CHUNK
    idx = jnp.arange(c, dtype=F32)
    diff = idx[:, None] - idx[None, :]
    decay = jnp.where(diff >= 0, jnp.exp(log_gamma[:, None, None] * jnp.maximum(diff, 0.0)), 0.0)
    xi = jnp.exp(log_gamma[:, None] * (idx + 1.0))
    zeta = jnp.exp(log_gamma[:, None] * (c - 1.0 - idx))
    chunk_decay = jnp.exp(log_gamma * c)
    ones = jnp.ones((RET_HEADS, c, HEAD_DIM), F32)
    half = HEAD_DIM // 2
    inv = 1.0 / (ROPE_BASE ** (jnp.arange(half, dtype=F32) / half))
    ang = jnp.arange(seq, dtype=F32)[:, None] * inv[None, :]
    cos, sin = jnp.cos(ang), jnp.sin(ang)
    return dict(
        decay=decay,
        xi=xi[:, :, None] * ones,
        zeta=zeta[:, :, None] * ones,
        chunk_decay=chunk_decay[:, None, None] * jnp.ones((RET_HEADS, 8, HEAD_DIM), F32),
        cos=jnp.concatenate([cos, cos], axis=-1),
        sin=jnp.concatenate([-sin, sin], axis=-1),
    )


def _retention_kernel(q_ref, k_ref, v_ref, g_ref, cos_ref, sin_ref, decay_ref, xi_ref, zeta_ref,
                      cd_ref, ng_ref, o_ref, state_ref):
    @pl.when(pl.program_id(2) == 0)
    def _():
        state_ref[...] = jnp.zeros_like(state_ref)

    c = RET_CHUNK
    half = HEAD_DIM // 2
    decay = decay_ref[...]
    xi = xi_ref[...]
    zeta = zeta_ref[...]
    cd = cd_ref[0:1, :]
    ng = ng_ref[...]
    def rotary(x, cos, sin):
        return x * cos + pltpu.roll(x, half, axis=1) * sin

    def group(gi, carry):
        chunks = range(RET_GROUP)
        sls = [pl.ds(pl.multiple_of((gi * RET_GROUP + u) * c, c), c) for u in chunks]
        cs = [(cos_ref[sl, :], sin_ref[sl, :]) for sl in sls]
        qs = [rotary(q_ref[sl, :], cos, sin).astype(BF16) for sl, (cos, sin) in zip(sls, cs)]
        ks = [rotary(k_ref[sl, :], cos, sin) * (HEAD_DIM ** -0.5) for sl, (cos, sin) in zip(sls, cs)]
        vs = [v_ref[sl, :].astype(BF16) for sl in sls]
        scores = [(_dot_nt(q, k.astype(BF16)) * decay).astype(BF16) for q, k in zip(qs, ks)]
        inner = [_dot(s, v) for s, v in zip(scores, vs)]
        kvs = [_dot((k * zeta).T.astype(BF16), v) for k, v in zip(ks, vs)]
        state = state_ref[...]
        outs = []
        for u in chunks:
            outs.append(inner[u] + _dot(qs[u], state.astype(BF16)) * xi)
            state = kvs[u] + cd * state
        state_ref[...] = state
        for u in chunks:
            o = outs[u]
            mu = jnp.mean(o, axis=-1, keepdims=True)
            oc = o - mu
            var = jnp.mean(oc * oc, axis=-1, keepdims=True)
            y = oc * lax.rsqrt(var + EPS) * ng
            o_ref[sls[u], :] = (y * jax.nn.silu(g_ref[sls[u], :])).astype(o_ref.dtype)
        return carry

    assert (q_ref.shape[0] // c) % RET_GROUP == 0
    lax.fori_loop(0, q_ref.shape[0] // c // RET_GROUP, group, 0)


def _retention(proj, norm_g, *, batch, seq, tm):
    consts = _retention_consts(seq)
    nt = seq // tm
    hh = RET_HEADS

    def col(group):
        return pl.BlockSpec((tm, HEAD_DIM), lambda b, h, i: (b * nt + i, group * hh + h))

    def per_head(rows):
        return pl.BlockSpec((None, rows, HEAD_DIM), lambda b, h, i: (h, 0, 0))

    pos = pl.BlockSpec((tm, HEAD_DIM), lambda b, h, i: (i, 0))
    return pl.pallas_call(
        _retention_kernel,
        grid=(batch, hh, nt),
        in_specs=[col(0), col(1), col(2), col(3), pos, pos,
                  per_head(RET_CHUNK), per_head(RET_CHUNK), per_head(RET_CHUNK), per_head(8), per_head(1)],
        out_specs=pl.BlockSpec((tm, HEAD_DIM), lambda b, h, i: (b * nt + i, h)),
        out_shape=jax.ShapeDtypeStruct((batch * seq, EV_GROUP), BF16),
        scratch_shapes=[pltpu.VMEM((HEAD_DIM, HEAD_DIM), F32)],
        compiler_params=_params("parallel", "parallel", "arbitrary"),
        name="retention",
    )(proj, proj, proj, proj, consts["cos"], consts["sin"], consts["decay"], consts["xi"],
      consts["zeta"], consts["chunk_decay"], norm_g.astype(F32).reshape(hh, 1, HEAD_DIM))


HG_LEVELS = (64, 32, 16, 8)


HG_GROUP = 8


def _hgrn_consts():
    c = HG_CHUNK
    t = np.arange(c)[:, None]
    s = np.arange(c)[None, :]
    tril = (s <= t).astype(np.float32)
    masks = []
    for lvl, bs in enumerate(HG_LEVELS):
        same = (t // bs) == (s // bs)
        if lvl < len(HG_LEVELS) - 1:
            masks.append(same & (t % bs >= bs // 2) & (s % bs < bs // 2))
        else:
            masks.append(same & (s <= t))
    return jnp.asarray(tril, BF16), jnp.asarray(np.stack(masks).astype(np.float32))


def _split3(x):
    hi = x.astype(BF16)
    rest = x - hi.astype(F32)
    mid = rest.astype(BF16)
    lo = (rest - mid.astype(F32)).astype(BF16)
    return hi, mid, lo


def _block_reference_rows(cum, bs, row):
    pieces = [jnp.broadcast_to(cum[b0 + row:b0 + row + 1, :], (bs, cum.shape[1]))
              for b0 in range(0, cum.shape[0], bs)]
    return pieces[0] if len(pieces) == 1 else jnp.concatenate(pieces, axis=0)


def _hgrn_kernel(q_ref, f_ref, i_ref, g_ref, lbl_ref, tril_ref, mask_ref, ng_ref, o_ref, state_ref, *,
                 layer):
    @pl.when(pl.program_id(2) == 0)
    def _():
        state_ref[...] = jnp.zeros_like(state_ref)

    c = HG_CHUNK
    logits = lbl_ref[...]
    e = jnp.exp(logits - jnp.max(logits, axis=0, keepdims=True))
    lb = jnp.sum(e[:layer + 1, :], axis=0, keepdims=True) / jnp.sum(e, axis=0, keepdims=True)
    tril = tril_ref[...]
    masks = [mask_ref[l] > 0.5 for l in range(len(HG_LEVELS))]
    ng = ng_ref[...]

    def group(gi, carry):
        chunks = range(HG_GROUP)
        sls = [pl.ds(pl.multiple_of((gi * HG_GROUP + u) * c, c), c) for u in chunks]
        fs = [lb + (1.0 - lb) * jax.nn.sigmoid(f_ref[sl, :]) for sl in sls]
        keys = [1.0 - f for f in fs]
        splits = [_split3(jnp.log(jnp.maximum(f, F_FLOOR))) for f in fs]
        cums = [_dot(tril, hi) + (_dot(tril, mid) + _dot(tril, lo)) for hi, mid, lo in splits]
        qs = [q_ref[sl, :] for sl in sls]
        scores = [jnp.zeros((c, c), F32) for _ in chunks]
        for lvl, bs in enumerate(HG_LEVELS):
            for u in chunks:
                a = cums[u] - _block_reference_rows(cums[u], bs, bs // 2 - 1)
                if lvl < len(HG_LEVELS) - 1:
                    decay = jnp.exp(-jnp.abs(a))
                    qa = qs[u] * decay
                    ka = keys[u] * decay
                else:
                    qa = qs[u] * jnp.exp(a)
                    ka = keys[u] * jnp.exp(-a)
                scores[u] = jnp.where(masks[lvl], _dot_nt(qa.astype(BF16), ka.astype(BF16)), scores[u])
        vs = [i_ref[sl, :] for sl in sls]
        lasts = [cum[c - 1:c, :] for cum in cums]
        intra = [_dot(scores[u].astype(BF16), vs[u].astype(BF16)) for u in chunks]
        kvs = [_dot(vs[u].T.astype(BF16), (keys[u] * jnp.exp(lasts[u] - cums[u])).astype(BF16)) for u in chunks]
        qcs = [(qs[u] * jnp.exp(cums[u])).astype(BF16) for u in chunks]
        state_t = state_ref[...]
        outs = []
        for u in chunks:
            outs.append(intra[u] + _dot_nt(qcs[u], state_t.astype(BF16)))
            state_t = jnp.exp(lasts[u]) * state_t + kvs[u]
        state_ref[...] = state_t
        for u in chunks:
            out = outs[u]
            y = out * lax.rsqrt(jnp.mean(out * out, axis=-1, keepdims=True) + EPS) * ng
            o_ref[sls[u], :] = (y * jax.nn.silu(g_ref[sls[u], :])).astype(o_ref.dtype)
        return carry

    assert (q_ref.shape[0] // c) % HG_GROUP == 0
    lax.fori_loop(0, q_ref.shape[0] // c // HG_GROUP, group, 0)


def _hgrn(proj, lb_logits, norm_g, *, layer, batch, seq, tm):
    cum_w, masks = _hgrn_consts()
    nt = seq // tm
    hh = HG_HEADS
    slots = lb_logits.shape[0]
    lbl = lb_logits.astype(F32).reshape(slots, hh, HEAD_DIM).transpose(1, 0, 2)

    def col(group):
        return pl.BlockSpec((tm, HEAD_DIM), lambda b, h, i: (b * nt + i, group * hh + h))

    return pl.pallas_call(
        functools.partial(_hgrn_kernel, layer=layer),
        grid=(batch, hh, nt),
        in_specs=[col(4), col(5), col(6), col(7),
                  pl.BlockSpec((None, slots, HEAD_DIM), lambda b, h, i: (h, 0, 0)),
                  pl.BlockSpec(cum_w.shape, lambda b, h, i: (0, 0)),
                  pl.BlockSpec(masks.shape, lambda b, h, i: (0, 0, 0)),
                  pl.BlockSpec((None, 1, HEAD_DIM), lambda b, h, i: (h, 0, 0))],
        out_specs=pl.BlockSpec((tm, HEAD_DIM), lambda b, h, i: (b * nt + i, h)),
        out_shape=jax.ShapeDtypeStruct((batch * seq, EV_GROUP), BF16),
        scratch_shapes=[pltpu.VMEM((HEAD_DIM, HEAD_DIM), F32)],
        compiler_params=_params("parallel", "parallel", "arbitrary"),
        name="hgrn2",
    )(proj, proj, proj, proj, lbl, cum_w, masks, norm_g.astype(F32).reshape(hh, 1, HEAD_DIM))


def _t5_bucket(dist):
    exact = REL_BUCKETS // 2
    d = np.maximum(dist, 0)
    log_ratio = (np.log(np.maximum(d, 1).astype(np.float32) / np.float32(exact))
                 / np.float32(math.log(REL_MAX_DIST / exact)))
    large = np.minimum(exact + (log_ratio * np.float32(REL_BUCKETS - exact)).astype(np.int32), REL_BUCKETS - 1)
    return np.where(d < exact, d, large)


def _dsa_row_order(dil):
    rho = np.arange(DSA_BLOCK)
    run = PERM_RUN * dil
    c, j, a = rho // run, (rho % run) // PERM_RUN, rho % PERM_RUN
    return (PERM_BLOCK // dil) * j + (PERM_RESIDUES // dil) * a + c


def _dsa_bucket_table():
    blk = DSA_BLOCK
    tabs = []
    for window, dil in DSA_BRANCHES:
        order = _dsa_row_order(dil)
        qi = order[:, None]
        ki = np.concatenate([order, blk + order])[None, :]
        delta = qi + blk - ki
        n_back = window // dil
        valid = (delta >= 0) & (delta <= n_back)
        tabs.append(np.where(valid, _t5_bucket(delta * dil), -1).astype(np.int32))
    return jnp.asarray(np.stack(tabs))


def _dsa_kernel(rb_ref, q_ref, k_ref, v_ref, bucket_ref, o_ref,
                qn_ref, kp_ref, vp_ref, m_ref, den_ref, num_ref, bias_ref, lg_ref, *, seq, pad_blocks):
    blk = DSA_BLOCK
    nj = seq // PERM_BLOCK
    tiled = (nj, PERM_RESIDUES, PERM_RUN, HEAD_DIM)
    head = pl.program_id(0)
    qn_ref[...] = q_ref[...].reshape(tiled)
    kp_ref[pad_blocks:pad_blocks + nj] = k_ref[...].reshape(tiled)
    vp_ref[pad_blocks:pad_blocks + nj] = v_ref[...].reshape(tiled)

    @pl.when(pl.program_id(1) == 0)
    def _():
        zeros = jnp.zeros((pad_blocks,) + tiled[1:], F32)
        kp_ref[0:pad_blocks] = zeros
        vp_ref[0:pad_blocks] = zeros
        first_block_keys = lax.broadcasted_iota(jnp.int32, (blk, 2 * blk), 1) >= blk
        for br in range(len(DSA_BRANCHES)):
            bucket = bucket_ref[br]
            bias = jnp.full(bucket.shape, NEG_INF, F32)
            for b in range(REL_BUCKETS):
                bias = jnp.where(bucket == b, rb_ref[b, head], bias)
            bias_ref[2 * br] = bias
            bias_ref[2 * br + 1] = jnp.where(first_block_keys, bias, NEG_INF)

    scale = HEAD_DIM ** -0.5

    def gather(ref, j0, r, dil, spans):
        if dil == 1:
            return ref[pl.ds(j0, spans)].reshape(spans * blk, HEAD_DIM)
        pieces = [ref[pl.ds(j0 + s * dil, dil), r + dil * c]
                  for s in range(spans) for c in range(PERM_RESIDUES // dil)]
        return jnp.concatenate(pieces, axis=0).reshape(spans * blk, HEAD_DIM)

    def scatter(ref, j0, r, dil, val):
        if dil == 1:
            ref[j0] = val.reshape(tiled[1:])
            return
        run = PERM_RUN * dil
        for c in range(PERM_RESIDUES // dil):
            ref[pl.ds(j0, dil), r + dil * c] = val[c * run:(c + 1) * run, :].reshape(dil, PERM_RUN, HEAD_DIM)

    trips = nj // DSA_UNROLL
    assert nj % DSA_UNROLL == 0 and all(dil <= pad_blocks for _, dil in DSA_BRANCHES)

    def block_coords(br, it, u):
        dil = DSA_BRANCHES[br][1]
        nb = nj // dil
        idx = it * DSA_UNROLL + u
        return idx // nb, idx % nb, dil

    def logits_stage(br, it, slot):
        for u in range(DSA_UNROLL):
            r, n, dil = block_coords(br, it, u)
            qb = gather(qn_ref, n * dil, r, dil, 1)
            kb = gather(kp_ref, pad_blocks + (n - 1) * dil, r, dil, 2)
            bias = bias_ref[2 * br + jnp.where(n == 0, 1, 0)]
            lg_ref[slot, u] = _dot_nt(qb.astype(BF16), kb.astype(BF16)) * scale + bias

    def softmax_stage(br, it, slot):
        blocks = range(DSA_UNROLL)
        coords = [block_coords(br, it, u) for u in blocks]
        logits = [lg_ref[slot, u] for u in blocks]
        ms = [jnp.max(lg, axis=-1, keepdims=True) for lg in logits]
        ps = [jnp.exp(lg - m) for lg, m in zip(logits, ms)]
        dens = [jnp.sum(p, axis=-1, keepdims=True) for p in ps]
        nums = [_dot(p.astype(BF16), gather(vp_ref, pad_blocks + (n - 1) * dil, r, dil, 2).astype(BF16))
                for p, (r, n, dil) in zip(ps, coords)]
        for (r, n, dil), m, den, num in zip(coords, ms, dens, nums):
            scatter(m_ref.at[br], n * dil, r, dil, jnp.broadcast_to(m, (blk, HEAD_DIM)))
            scatter(den_ref.at[br], n * dil, r, dil, jnp.broadcast_to(den, (blk, HEAD_DIM)))
            scatter(num_ref.at[br], n * dil, r, dil, num)

    n_br = len(DSA_BRANCHES)
    logits_stage(0, 0, 0)
    for br in range(n_br):
        first_slot = (br * trips) % 2

        def overlapped(it, carry, br=br, first_slot=first_slot):
            slot = (first_slot + it) % 2
            softmax_stage(br, it, slot)
            logits_stage(br, it + 1, 1 - slot)
            return carry

        lax.fori_loop(0, trips - 1, overlapped, 0)
        last_slot = (first_slot + trips - 1) % 2
        softmax_stage(br, trips - 1, last_slot)
        if br + 1 < n_br:
            logits_stage(br + 1, 0, 1 - last_slot)

    def merge(j, carry):
        n_br = len(DSA_BRANCHES)
        ms = [m_ref[br, j] for br in range(n_br)]
        m_all = functools.reduce(jnp.maximum, ms)
        ws = [jnp.exp(mi - m_all) for mi in ms]
        den = functools.reduce(lambda x, y: x + y, [ws[br] * den_ref[br, j] for br in range(n_br)])
        num = functools.reduce(lambda x, y: x + y, [ws[br] * num_ref[br, j] for br in range(n_br)])
        rows = pl.ds(pl.multiple_of(j * blk, blk), blk)
        o_ref[rows, :] = (num / den).reshape(blk, HEAD_DIM).astype(o_ref.dtype)
        return carry

    lax.fori_loop(0, nj, merge, 0, unroll=2)


def _dilated_attention(qkv, rel_bias, *, batch, seq):
    hh = DSA_HEADS
    nj = seq // PERM_BLOCK
    pad_blocks = max(dil for _, dil in DSA_BRANCHES)
    bucket = _dsa_bucket_table()
    tiled = (PERM_RESIDUES, PERM_RUN, HEAD_DIM)

    def col(group):
        return pl.BlockSpec((seq, HEAD_DIM), lambda h, b: (b, group * hh + h))

    return pl.pallas_call(
        functools.partial(_dsa_kernel, seq=seq, pad_blocks=pad_blocks),
        grid=(hh, batch),
        in_specs=[pl.BlockSpec(memory_space=pltpu.SMEM),
                  col(0), col(1), col(2),
                  pl.BlockSpec(bucket.shape, lambda h, b: (0, 0, 0))],
        out_specs=pl.BlockSpec((seq, HEAD_DIM), lambda h, b: (b, h)),
        out_shape=jax.ShapeDtypeStruct((batch * seq, hh * HEAD_DIM), BF16),
        scratch_shapes=[pltpu.VMEM((nj,) + tiled, F32),
                        pltpu.VMEM((pad_blocks + nj,) + tiled, F32),
                        pltpu.VMEM((pad_blocks + nj,) + tiled, F32),
                        pltpu.VMEM((len(DSA_BRANCHES), nj) + tiled, F32),
                        pltpu.VMEM((len(DSA_BRANCHES), nj) + tiled, F32),
                        pltpu.VMEM((len(DSA_BRANCHES), nj) + tiled, F32),
                        pltpu.VMEM((2 * len(DSA_BRANCHES), DSA_BLOCK, 2 * DSA_BLOCK), F32),
                        pltpu.VMEM((2, DSA_UNROLL, DSA_BLOCK, 2 * DSA_BLOCK), F32)],
        compiler_params=_params("parallel", "arbitrary"),
        name="dilated_attention",
    )(rel_bias.astype(F32), qkv, qkv, qkv, bucket)


def _mem_kv_kernel(mem_ref, g_ref, w_ref, kg_ref, k_ref, v_ref):
    h = _rms_rows(mem_ref[...], g_ref[...]).astype(BF16)
    d = XA_HEADS * XA_HD
    kg = kg_ref[...]
    for hd in range(XA_HEADS):
        sl = slice(hd * XA_HD, (hd + 1) * XA_HD)
        k_ref[:, sl] = _rms_rows(_dot(h, w_ref[:, sl]), kg).astype(k_ref.dtype)
    v_ref[...] = _dot(h, w_ref[:, d:2 * d]).astype(v_ref.dtype)


def _mem_kv(mem2d, g, w_kv, k_norm_g, *, batch):
    d = XA_HEADS * XA_HD
    return pl.pallas_call(
        _mem_kv_kernel,
        grid=(batch,),
        in_specs=[pl.BlockSpec((N_MEM, D_MODEL), lambda b: (b, 0)),
                  pl.BlockSpec((1, D_MODEL), lambda b: (0, 0)),
                  pl.BlockSpec((D_MODEL, 2 * d), lambda b: (0, 0)),
                  pl.BlockSpec((1, XA_HD), lambda b: (0, 0))],
        out_specs=[pl.BlockSpec((N_MEM, d), lambda b: (b, 0)),
                   pl.BlockSpec((N_MEM, d), lambda b: (b, 0))],
        out_shape=[jax.ShapeDtypeStruct((batch * N_MEM, d), BF16),
                   jax.ShapeDtypeStruct((batch * N_MEM, d), BF16)],
        compiler_params=_params("parallel"),
        name="mem_kv",
    )(mem2d, g.astype(F32).reshape(1, D_MODEL), w_kv.astype(BF16), k_norm_g.astype(F32).reshape(1, XA_HD))


def _xattn_kernel(*refs, n_y, unpermute, oc):
    xin_ref = refs[0]
    y_refs = refs[1:1 + n_y]
    wmix_refs = refs[1 + n_y:1 + 2 * n_y]
    rest = refs[1 + 2 * n_y:]
    if unpermute:
        p_ref, rest = rest[0], rest[1:]
    g_ref, wq_ref, qg_ref, mk_ref, mv_ref, wo_ref, o_ref, x_ref, att_ref = rest
    d = o_ref.shape[1]
    ys = [y_ref[...] for y_ref in y_refs]
    if unpermute:
        ys = [_permute_rows(p_ref[...], y) for y in ys]
    for n in range(d // oc):
        sl = slice(n * oc, (n + 1) * oc)
        acc = xin_ref[:, sl]
        for y, w_ref in zip(ys, wmix_refs):
            acc = acc + _dot(y, w_ref[:, sl])
        x_ref[:, sl] = acc

    h = _rms_rows(x_ref[...], g_ref[...]).astype(BF16)
    qg = qg_ref[...]
    sls = [slice(hd * XA_HD, (hd + 1) * XA_HD) for hd in range(XA_HEADS)]
    qs = [_dot(h, wq_ref[:, sl]) for sl in sls]
    qs = [_rms_rows(q, qg).astype(BF16) for q in qs]
    logits = [_dot_nt(q, mk_ref[:, sl]) * (XA_HD ** -0.5) for q, sl in zip(qs, sls)]
    ps = [jnp.exp(lg - jnp.max(lg, axis=-1, keepdims=True)) for lg in logits]
    ps = [(p / jnp.sum(p, axis=-1, keepdims=True)).astype(BF16) for p in ps]
    for p, sl in zip(ps, sls):
        att_ref[:, sl] = _dot(p, mv_ref[:, sl]).astype(att_ref.dtype)
    for n in range(d // oc):
        sl = slice(n * oc, (n + 1) * oc)
        o_ref[:, sl] = x_ref[:, sl] + _dot(att_ref[...], wo_ref[:, sl])


def _mix_out_xattn(x2d, ys, w_mix, g, w_q, q_norm_g, mk, mv, w_o, *, seq, tm, unpermute=False, oc=256):
    t, d = x2d.shape
    nt = seq // tm
    n_y = len(ys)
    ws, off = [], 0
    for y in ys:
        ws.append(w_mix[off:off + y.shape[1]].astype(BF16))
        off += y.shape[1]
    const = lambda shape: pl.BlockSpec(shape, lambda i: (0, 0))
    in_specs = [pl.BlockSpec((tm, d), lambda i: (i, 0))]
    in_specs += [pl.BlockSpec((tm, y.shape[1]), lambda i: (i, 0)) for y in ys]
    in_specs += [const(wk.shape) for wk in ws]
    args = [x2d, *ys, *ws]
    if unpermute:
        in_specs.append(const((PERM_BLOCK, PERM_BLOCK)))
        args.append(jnp.asarray(_block_permutation().T, BF16))
    in_specs += [const((1, d)), const((d, d)), const((1, XA_HD)),
                 pl.BlockSpec((N_MEM, d), lambda i: (i // nt, 0)),
                 pl.BlockSpec((N_MEM, d), lambda i: (i // nt, 0)),
                 const((d, d))]
    args += [g.astype(F32).reshape(1, d), w_q.astype(BF16), q_norm_g.astype(F32).reshape(1, XA_HD),
             mk, mv, w_o.astype(BF16)]
    return pl.pallas_call(
        functools.partial(_xattn_kernel, n_y=n_y, unpermute=unpermute, oc=oc),
        grid=(t // tm,),
        in_specs=in_specs,
        out_specs=pl.BlockSpec((tm, d), lambda i: (i, 0)),
        out_shape=jax.ShapeDtypeStruct((t, d), F32),
        scratch_shapes=[pltpu.VMEM((tm, d), F32), pltpu.VMEM((tm, d), BF16)],
        compiler_params=_params("parallel"),
        name="mix_out_xattn",
    )(*args)


def _ffn_kernel(x_ref, g_ref, win_ref, cw_ref, cb_ref, wout_ref, o_ref, carry_ref, act_ref, *, nt, fc, oc):
    tm, d = x_ref.shape
    assert CONV_W == 3

    @pl.when(pl.program_id(0) % nt == 0)
    def _():
        carry_ref[...] = jnp.zeros_like(carry_ref)

    h = _rms_rows(x_ref[...], g_ref[...]).astype(BF16)
    row = lax.broadcasted_iota(jnp.int32, (8, fc), 0)
    for c in range(D_FF // fc):
        sl = slice(c * fc, (c + 1) * fc)
        gate = _dot(h, win_ref[:, sl])
        up = _dot(h, win_ref[:, D_FF + c * fc:D_FF + (c + 1) * fc])
        prev = carry_ref[c]
        carry_ref[c] = gate[tm - 8:tm, :]
        w0, w1, w2, bias = cw_ref[0:1, sl], cw_ref[1:2, sl], cw_ref[2:3, sl], cb_ref[:, sl]
        conv = pltpu.roll(gate, 2, axis=0) * w0 + pltpu.roll(gate, 1, axis=0) * w1 + gate * w2 + bias
        top = gate[0:8, :]
        t1 = jnp.where(row == 0, prev[7:8, :], pltpu.roll(top, 1, axis=0))
        t2 = jnp.where(row == 0, prev[6:7, :], jnp.where(row == 1, prev[7:8, :], pltpu.roll(top, 2, axis=0)))
        conv = jnp.concatenate([t2 * w0 + t1 * w1 + top * w2 + bias, conv[8:, :]], axis=0)
        act_ref[:, sl] = (jax.nn.gelu(conv) * up).astype(BF16)
    for n in range(d // oc):
        sl = slice(n * oc, (n + 1) * oc)
        o_ref[:, sl] = x_ref[:, sl] + _dot(act_ref[...], wout_ref[:, sl])


def _ffn(x2d, g, w_in, conv_w, conv_b, w_out, *, seq, tm, fc=256, oc=256):
    t, d = x2d.shape
    nt = seq // tm
    const = lambda shape: pl.BlockSpec(shape, lambda i: (0, 0))
    return pl.pallas_call(
        functools.partial(_ffn_kernel, nt=nt, fc=fc, oc=oc),
        grid=(t // tm,),
        in_specs=[pl.BlockSpec((tm, d), lambda i: (i, 0)),
                  const((1, d)), const((d, 2 * D_FF)), const((CONV_W, D_FF)), const((1, D_FF)),
                  const((D_FF, d))],
        out_specs=pl.BlockSpec((tm, d), lambda i: (i, 0)),
        out_shape=jax.ShapeDtypeStruct((t, d), F32),
        scratch_shapes=[pltpu.VMEM((D_FF // fc, 8, fc), F32), pltpu.VMEM((tm, D_FF), BF16)],
        compiler_params=_params("arbitrary"),
        name="conv_ffn",
    )(x2d, g.astype(F32).reshape(1, d), w_in.astype(BF16), conv_w.astype(F32),
      conv_b.astype(F32).reshape(1, D_FF), w_out.astype(BF16))


def kernel(x, mem, mix_norm_g, ev_w_in, ev_ret_norm_g, ev_hg_norm_g, hg_lb_logits, ev_w_out, od_w_in, od_q_norm_g, od_k_norm_g, rel_bias, od_w_out, xa_norm_g, xa_mem_norm_g, xa_w_q, xa_w_kv, xa_q_norm_g, xa_k_norm_g, xa_w_o, ffn_norm_g, ffn_w_in, ffn_conv_w, ffn_conv_b, ffn_w_out):
    batch, seq, d = x.shape
    depth = mix_norm_g.shape[0]
    x2d = x.reshape(batch * seq, d)
    mem2d = mem.reshape(batch * mem.shape[1], d)
    tm = 512
    for l in range(depth):
        if l % 2 == 0:
            e = l // 2
            proj = _rms_proj(x2d, mix_norm_g[l], ev_w_in[e], tm=tm)
            y_ret = _retention(proj, ev_ret_norm_g[e], batch=batch, seq=seq, tm=2 * tm)
            y_hg = _hgrn(proj, hg_lb_logits, ev_hg_norm_g[e], layer=l, batch=batch, seq=seq, tm=2 * tm)
            ys, w_mix, permuted = [y_ret, y_hg], ev_w_out[e], False
        else:
            o = l // 2
            qk_gains = jnp.concatenate([jnp.tile(od_q_norm_g[o], DSA_HEADS), jnp.tile(od_k_norm_g[o], DSA_HEADS)])
            qkv = _rms_proj(x2d, mix_norm_g[l], od_w_in[o], tm=tm, permute=True, head_gains=qk_gains)
            att = _dilated_attention(qkv, rel_bias, batch=batch, seq=seq)
            ys, w_mix, permuted = [att], od_w_out[o], True
        mk, mv = _mem_kv(mem2d, xa_mem_norm_g[l], xa_w_kv[l], xa_k_norm_g[l], batch=batch)
        x2d = _mix_out_xattn(x2d, ys, w_mix, xa_norm_g[l], xa_w_q[l], xa_q_norm_g[l], mk, mv, xa_w_o[l],
                             seq=seq, tm=tm, unpermute=permuted)
        x2d = _ffn(x2d, ffn_norm_g[l], ffn_w_in[l], ffn_conv_w[l], ffn_conv_b[l], ffn_w_out[l], seq=seq, tm=tm)
    return x2d.reshape(batch, seq, d)
```

```python
import functools
import math

import jax
import jax.numpy as jnp
import numpy as np
from jax import lax
from jax.experimental import pallas as pl
from jax.experimental.pallas import tpu as pltpu

F32 = jnp.float32
BF16 = jnp.bfloat16

D_MODEL = 1024
N_MEM = 256
EPS = 1e-6
HEAD_DIM = 128
RET_HEADS = 4
RET_CHUNK = 128
RET_GROUP = 8
ROPE_BASE = 10000.0
HG_HEADS = 4
HG_CHUNK = 64
F_FLOOR = 1e-6
DSA_HEADS = 8
DSA_BRANCHES = ((128, 1), (512, 4), (2048, 16))
DSA_BLOCK = 128
DSA_UNROLL = 8
REL_BUCKETS = 32
REL_MAX_DIST = 2048
XA_HEADS = 4
XA_HD = 256
D_FF = 2816
CONV_W = 3
EV_GROUP = RET_HEADS * HEAD_DIM

VMEM_LIMIT_BYTES = 56 * 1024 * 1024
NEG_INF = float("-inf")

_NT = (((1,), (1,)), ((), ()))


def _params(*sem):
    return pltpu.CompilerParams(dimension_semantics=sem, vmem_limit_bytes=VMEM_LIMIT_BYTES)


def _rms_rows(x, g):
    return x * lax.rsqrt(jnp.mean(x * x, axis=-1, keepdims=True) + EPS) * g


def _dot(a, b):
    return jnp.dot(a, b, preferred_element_type=F32)


def _dot_nt(a, b):
    return lax.dot_general(a, b, _NT, preferred_element_type=F32)


PERM_BLOCK = 128
PERM_RESIDUES = 16
PERM_RUN = PERM_BLOCK // PERM_RESIDUES


def _block_permutation():
    rho = np.arange(PERM_BLOCK)
    src = PERM_RESIDUES * (rho % PERM_RUN) + rho // PERM_RUN
    p = np.zeros((PERM_BLOCK, PERM_BLOCK), np.float32)
    p[rho, src] = 1.0
    return p


def _permute_rows(p, x):
    blocks = [_dot(p, x[b * PERM_BLOCK:(b + 1) * PERM_BLOCK, :]).astype(BF16)
              for b in range(x.shape[0] // PERM_BLOCK)]
    return jnp.concatenate(blocks, axis=0)


def _rms_proj_kernel(*refs, col_chunk, permute, n_head_norm):
    x_ref, g_ref, w_ref = refs[:3]
    o_ref = refs[-1]
    extras = list(refs[3:-1])
    p_ref = extras.pop(0) if permute else None
    hg_ref = extras.pop(0) if n_head_norm else None
    h = _rms_rows(x_ref[...], g_ref[...]).astype(BF16)
    if permute:
        h = _permute_rows(p_ref[...], h)
    n = o_ref.shape[1]
    for c in range(n // col_chunk):
        sl = slice(c * col_chunk, (c + 1) * col_chunk)
        y = _dot(h, w_ref[:, sl])
        if c * col_chunk < n_head_norm:
            assert (c + 1) * col_chunk <= n_head_norm
            heads = [_rms_rows(y[:, k * HEAD_DIM:(k + 1) * HEAD_DIM],
                               hg_ref[:, c * col_chunk + k * HEAD_DIM:c * col_chunk + (k + 1) * HEAD_DIM])
                     for k in range(col_chunk // HEAD_DIM)]
            y = jnp.concatenate(heads, axis=1)
        o_ref[:, sl] = y.astype(o_ref.dtype)


def _layer_weight(w, layer):
    return pl.BlockSpec((None,) + w.shape[1:], lambda i: (layer, 0, 0))


def _rms_proj(x2d, g, w, layer, *, tm, out_dtype=F32, col_chunk=512, permute=False, head_gains=None):
    t, d = x2d.shape
    n = w.shape[2]
    in_specs = [
        pl.BlockSpec((tm, d), lambda i: (i, 0)),
        pl.BlockSpec((1, d), lambda i: (0, 0)),
        _layer_weight(w, layer),
    ]
    args = [x2d, g.reshape(1, d).astype(F32), w]
    if permute:
        in_specs.append(pl.BlockSpec((PERM_BLOCK, PERM_BLOCK), lambda i: (0, 0)))
        args.append(jnp.asarray(_block_permutation(), BF16))
    n_head_norm = 0 if head_gains is None else head_gains.shape[0]
    if n_head_norm:
        in_specs.append(pl.BlockSpec((1, n_head_norm), lambda i: (0, 0)))
        args.append(head_gains.astype(F32).reshape(1, n_head_norm))
    return pl.pallas_call(
        functools.partial(_rms_proj_kernel, col_chunk=col_chunk, permute=permute, n_head_norm=n_head_norm),
        grid=(t // tm,),
        in_specs=in_specs,
        out_specs=pl.BlockSpec((tm, n), lambda i: (i, 0)),
        out_shape=jax.ShapeDtypeStruct((t, n), out_dtype),
        compiler_params=_params("parallel"),
        name="rms_proj",
    )(*args)


def _retention_consts(seq):
    h = jnp.arange(RET_HEADS, dtype=F32)
    log_gamma = jnp.log(1.0 - jnp.exp2(-5.0 - h))
    c = RET_CHUNK
    idx = jnp.arange(c, dtype=F32)
    diff = idx[:, None] - idx[None, :]
    decay = jnp.where(diff >= 0, jnp.exp(log_gamma[:, None, None] * jnp.maximum(diff, 0.0)), 0.0)
    xi = jnp.exp(log_gamma[:, None] * (idx + 1.0))
    zeta = jnp.exp(log_gamma[:, None] * (c - 1.0 - idx))
    chunk_decay = jnp.exp(log_gamma * c)
    ones = jnp.ones((RET_HEADS, c, HEAD_DIM), F32)
    half = HEAD_DIM // 2
    inv = 1.0 / (ROPE_BASE ** (jnp.arange(half, dtype=F32) / half))
    ang = jnp.arange(seq, dtype=F32)[:, None] * inv[None, :]
    cos, sin = jnp.cos(ang), jnp.sin(ang)
    return dict(
        decay=decay,
        xi=xi[:, :, None] * ones,
        zeta=zeta[:, :, None] * ones,
        chunk_decay=chunk_decay[:, None, None] * jnp.ones((RET_HEADS, 8, HEAD_DIM), F32),
        cos=jnp.concatenate([cos, cos], axis=-1),
        sin=jnp.concatenate([-sin, sin], axis=-1),
    )


def _retention_kernel(q_ref, k_ref, v_ref, g_ref, cos_ref, sin_ref, decay_ref, xi_ref, zeta_ref,
                      cd_ref, ng_ref, o_ref, state_ref):
    @pl.when(pl.program_id(2) == 0)
    def _():
        state_ref[...] = jnp.zeros_like(state_ref)

    c = RET_CHUNK
    half = HEAD_DIM // 2
    decay = decay_ref[...]
    xi = xi_ref[...]
    zeta = zeta_ref[...]
    cd = cd_ref[0:1, :]
    ng = ng_ref[...]
    def rotary(x, cos, sin):
        return x * cos + pltpu.roll(x, half, axis=1) * sin

    def group(gi, carry):
        chunks = range(RET_GROUP)
        sls = [pl.ds(pl.multiple_of((gi * RET_GROUP + u) * c, c), c) for u in chunks]
        cs = [(cos_ref[sl, :], sin_ref[sl, :]) for sl in sls]
        qs = [rotary(q_ref[sl, :], cos, sin).astype(BF16) for sl, (cos, sin) in zip(sls, cs)]
        ks = [rotary(k_ref[sl, :], cos, sin) * (HEAD_DIM ** -0.5) for sl, (cos, sin) in zip(sls, cs)]
        vs = [v_ref[sl, :].astype(BF16) for sl in sls]
        scores = [(_dot_nt(q, k.astype(BF16)) * decay).astype(BF16) for q, k in zip(qs, ks)]
        inner = [_dot(s, v) for s, v in zip(scores, vs)]
        kvs = [_dot((k * zeta).T.astype(BF16), v) for k, v in zip(ks, vs)]
        state = state_ref[...]
        outs = []
        for u in chunks:
            outs.append(inner[u] + _dot(qs[u], state.astype(BF16)) * xi)
            state = kvs[u] + cd * state
        state_ref[...] = state
        for u in chunks:
            o = outs[u]
            mu = jnp.mean(o, axis=-1, keepdims=True)
            oc = o - mu
            var = jnp.mean(oc * oc, axis=-1, keepdims=True)
            y = oc * lax.rsqrt(var + EPS) * ng
            o_ref[sls[u], :] = (y * jax.nn.silu(g_ref[sls[u], :])).astype(o_ref.dtype)
        return carry

    assert (q_ref.shape[0] // c) % RET_GROUP == 0
    lax.fori_loop(0, q_ref.shape[0] // c // RET_GROUP, group, 0)


def _retention(proj, norm_g, *, batch, seq, tm):
    consts = _retention_consts(seq)
    nt = seq // tm
    hh = RET_HEADS

    def col(group):
        return pl.BlockSpec((tm, HEAD_DIM), lambda b, h, i: (b * nt + i, group * hh + h))

    def per_head(rows):
        return pl.BlockSpec((None, rows, HEAD_DIM), lambda b, h, i: (h, 0, 0))

    pos = pl.BlockSpec((tm, HEAD_DIM), lambda b, h, i: (i, 0))
    return pl.pallas_call(
        _retention_kernel,
        grid=(batch, hh, nt),
        in_specs=[col(0), col(1), col(2), col(3), pos, pos,
                  per_head(RET_CHUNK), per_head(RET_CHUNK), per_head(RET_CHUNK), per_head(8), per_head(1)],
        out_specs=pl.BlockSpec((tm, HEAD_DIM), lambda b, h, i: (b * nt + i, h)),
        out_shape=jax.ShapeDtypeStruct((batch * seq, EV_GROUP), BF16),
        scratch_shapes=[pltpu.VMEM((HEAD_DIM, HEAD_DIM), F32)],
        compiler_params=_params("parallel", "parallel", "arbitrary"),
        name="retention",
    )(proj, proj, proj, proj, consts["cos"], consts["sin"], consts["decay"], consts["xi"],
      consts["zeta"], consts["chunk_decay"], norm_g.astype(F32).reshape(hh, 1, HEAD_DIM))


HG_LEVELS = (64, 32, 16, 8)


HG_GROUP = 8


def _hgrn_consts():
    c = HG_CHUNK
    t = np.arange(c)[:, None]
    s = np.arange(c)[None, :]
    tril = (s <= t).astype(np.float32)
    masks = []
    for lvl, bs in enumerate(HG_LEVELS):
        same = (t // bs) == (s // bs)
        if lvl < len(HG_LEVELS) - 1:
            masks.append(same & (t % bs >= bs // 2) & (s % bs < bs // 2))
        else:
            masks.append(same & (s <= t))
    return jnp.asarray(tril, BF16), jnp.asarray(np.stack(masks).astype(np.float32))


def _split3(x):
    hi = x.astype(BF16)
    rest = x - hi.astype(F32)
    mid = rest.astype(BF16)
    lo = (rest - mid.astype(F32)).astype(BF16)
    return hi, mid, lo


def _block_reference_rows(cum, bs, row):
    pieces = [jnp.broadcast_to(cum[b0 + row:b0 + row + 1, :], (bs, cum.shape[1]))
              for b0 in range(0, cum.shape[0], bs)]
    return pieces[0] if len(pieces) == 1 else jnp.concatenate(pieces, axis=0)


def _hgrn_kernel(q_ref, f_ref, i_ref, g_ref, lbl_ref, tril_ref, mask_ref, ng_ref, o_ref, state_ref, *,
                 layer):
    @pl.when(pl.program_id(2) == 0)
    def _():
        state_ref[...] = jnp.zeros_like(state_ref)

    c = HG_CHUNK
    logits = lbl_ref[...]
    e = jnp.exp(logits - jnp.max(logits, axis=0, keepdims=True))
    lb = jnp.sum(e[:layer + 1, :], axis=0, keepdims=True) / jnp.sum(e, axis=0, keepdims=True)
    tril = tril_ref[...]
    masks = [mask_ref[l] > 0.5 for l in range(len(HG_LEVELS))]
    ng = ng_ref[...]

    def group(gi, carry):
        chunks = range(HG_GROUP)
        sls = [pl.ds(pl.multiple_of((gi * HG_GROUP + u) * c, c), c) for u in chunks]
        fs = [lb + (1.0 - lb) * jax.nn.sigmoid(f_ref[sl, :]) for sl in sls]
        keys = [1.0 - f for f in fs]
        splits = [_split3(jnp.log(jnp.maximum(f, F_FLOOR))) for f in fs]
        cums = [_dot(tril, hi) + (_dot(tril, mid) + _dot(tril, lo)) for hi, mid, lo in splits]
        qs = [q_ref[sl, :] for sl in sls]
        scores = [jnp.zeros((c, c), F32) for _ in chunks]
        for lvl, bs in enumerate(HG_LEVELS):
            for u in chunks:
                a = cums[u] - _block_reference_rows(cums[u], bs, bs // 2 - 1)
                if lvl < len(HG_LEVELS) - 1:
                    decay = jnp.exp(-jnp.abs(a))
                    qa = qs[u] * decay
                    ka = keys[u] * decay
                else:
                    qa = qs[u] * jnp.exp(a)
                    ka = keys[u] * jnp.exp(-a)
                scores[u] = jnp.where(masks[lvl], _dot_nt(qa.astype(BF16), ka.astype(BF16)), scores[u])
        vs = [i_ref[sl, :] for sl in sls]
        lasts = [cum[c - 1:c, :] for cum in cums]
        intra = [_dot(scores[u].astype(BF16), vs[u].astype(BF16)) for u in chunks]
        kvs = [_dot(vs[u].T.astype(BF16), (keys[u] * jnp.exp(lasts[u] - cums[u])).astype(BF16)) for u in chunks]
        qcs = [(qs[u] * jnp.exp(cums[u])).astype(BF16) for u in chunks]
        state_t = state_ref[...]
        outs = []
        for u in chunks:
            outs.append(intra[u] + _dot_nt(qcs[u], state_t.astype(BF16)))
            state_t = jnp.exp(lasts[u]) * state_t + kvs[u]
        state_ref[...] = state_t
        for u in chunks:
            out = outs[u]
            y = out * lax.rsqrt(jnp.mean(out * out, axis=-1, keepdims=True) + EPS) * ng
            o_ref[sls[u], :] = (y * jax.nn.silu(g_ref[sls[u], :])).astype(o_ref.dtype)
        return carry

    assert (q_ref.shape[0] // c) % HG_GROUP == 0
    lax.fori_loop(0, q_ref.shape[0] // c // HG_GROUP, group, 0)


def _hgrn(proj, lb_logits, norm_g, *, layer, batch, seq, tm):
    cum_w, masks = _hgrn_consts()
    nt = seq // tm
    hh = HG_HEADS
    slots = lb_logits.shape[0]
    lbl = lb_logits.astype(F32).reshape(slots, hh, HEAD_DIM).transpose(1, 0, 2)

    def col(group):
        return pl.BlockSpec((tm, HEAD_DIM), lambda b, h, i: (b * nt + i, group * hh + h))

    return pl.pallas_call(
        functools.partial(_hgrn_kernel, layer=layer),
        grid=(batch, hh, nt),
        in_specs=[col(4), col(5), col(6), col(7),
                  pl.BlockSpec((None, slots, HEAD_DIM), lambda b, h, i: (h, 0, 0)),
                  pl.BlockSpec(cum_w.shape, lambda b, h, i: (0, 0)),
                  pl.BlockSpec(masks.shape, lambda b, h, i: (0, 0, 0)),
                  pl.BlockSpec((None, 1, HEAD_DIM), lambda b, h, i: (h, 0, 0))],
        out_specs=pl.BlockSpec((tm, HEAD_DIM), lambda b, h, i: (b * nt + i, h)),
        out_shape=jax.ShapeDtypeStruct((batch * seq, EV_GROUP), BF16),
        scratch_shapes=[pltpu.VMEM((HEAD_DIM, HEAD_DIM), F32)],
        compiler_params=_params("parallel", "parallel", "arbitrary"),
        name="hgrn2",
    )(proj, proj, proj, proj, lbl, cum_w, masks, norm_g.astype(F32).reshape(hh, 1, HEAD_DIM))


def _t5_bucket(dist):
    exact = REL_BUCKETS // 2
    d = np.maximum(dist, 0)
    log_ratio = (np.log(np.maximum(d, 1).astype(np.float32) / np.float32(exact))
                 / np.float32(math.log(REL_MAX_DIST / exact)))
    large = np.minimum(exact + (log_ratio * np.float32(REL_BUCKETS - exact)).astype(np.int32), REL_BUCKETS - 1)
    return np.where(d < exact, d, large)


def _dsa_row_order(dil):
    rho = np.arange(DSA_BLOCK)
    run = PERM_RUN * dil
    c, j, a = rho // run, (rho % run) // PERM_RUN, rho % PERM_RUN
    return (PERM_BLOCK // dil) * j + (PERM_RESIDUES // dil) * a + c


def _dsa_bucket_table():
    blk = DSA_BLOCK
    tabs = []
    for window, dil in DSA_BRANCHES:
        order = _dsa_row_order(dil)
        qi = order[:, None]
        ki = np.concatenate([order, blk + order])[None, :]
        delta = qi + blk - ki
        n_back = window // dil
        valid = (delta >= 0) & (delta <= n_back)
        tabs.append(np.where(valid, _t5_bucket(delta * dil), -1).astype(np.int32))
    return jnp.asarray(np.stack(tabs))


def _dsa_kernel(rb_ref, q_ref, k_ref, v_ref, bucket_ref, o_ref,
                qn_ref, kp_ref, vp_ref, m_ref, den_ref, num_ref, bias_ref, lg_ref, *, seq, pad_blocks):
    blk = DSA_BLOCK
    nj = seq // PERM_BLOCK
    tiled = (nj, PERM_RESIDUES, PERM_RUN, HEAD_DIM)
    head = pl.program_id(0)
    qn_ref[...] = q_ref[...].reshape(tiled)
    kp_ref[pad_blocks:pad_blocks + nj] = k_ref[...].reshape(tiled)
    vp_ref[pad_blocks:pad_blocks + nj] = v_ref[...].reshape(tiled)

    @pl.when(pl.program_id(1) == 0)
    def _():
        zeros = jnp.zeros((pad_blocks,) + tiled[1:], F32)
        kp_ref[0:pad_blocks] = zeros
        vp_ref[0:pad_blocks] = zeros
        first_block_keys = lax.broadcasted_iota(jnp.int32, (blk, 2 * blk), 1) >= blk
        for br in range(len(DSA_BRANCHES)):
            bucket = bucket_ref[br]
            bias = jnp.full(bucket.shape, NEG_INF, F32)
            for b in range(REL_BUCKETS):
                bias = jnp.where(bucket == b, rb_ref[b, head], bias)
            bias_ref[2 * br] = bias
            bias_ref[2 * br + 1] = jnp.where(first_block_keys, bias, NEG_INF)

    scale = HEAD_DIM ** -0.5

    def gather(ref, j0, r, dil, spans):
        if dil == 1:
            return ref[pl.ds(j0, spans)].reshape(spans * blk, HEAD_DIM)
        pieces = [ref[pl.ds(j0 + s * dil, dil), r + dil * c]
                  for s in range(spans) for c in range(PERM_RESIDUES // dil)]
        return jnp.concatenate(pieces, axis=0).reshape(spans * blk, HEAD_DIM)

    def scatter(ref, j0, r, dil, val):
        if dil == 1:
            ref[j0] = val.reshape(tiled[1:])
            return
        run = PERM_RUN * dil
        for c in range(PERM_RESIDUES // dil):
            ref[pl.ds(j0, dil), r + dil * c] = val[c * run:(c + 1) * run, :].reshape(dil, PERM_RUN, HEAD_DIM)

    trips = nj // DSA_UNROLL
    assert nj % DSA_UNROLL == 0 and all(dil <= pad_blocks for _, dil in DSA_BRANCHES)

    def block_coords(br, it, u):
        dil = DSA_BRANCHES[br][1]
        nb = nj // dil
        idx = it * DSA_UNROLL + u
        return idx // nb, idx % nb, dil

    def logits_stage(br, it, slot):
        for u in range(DSA_UNROLL):
            r, n, dil = block_coords(br, it, u)
            qb = gather(qn_ref, n * dil, r, dil, 1)
            kb = gather(kp_ref, pad_blocks + (n - 1) * dil, r, dil, 2)
            bias = bias_ref[2 * br + jnp.where(n == 0, 1, 0)]
            lg_ref[slot, u] = _dot_nt(qb.astype(BF16), kb.astype(BF16)) * scale + bias

    def softmax_stage(br, it, slot):
        blocks = range(DSA_UNROLL)
        coords = [block_coords(br, it, u) for u in blocks]
        logits = [lg_ref[slot, u] for u in blocks]
        ms = [jnp.max(lg, axis=-1, keepdims=True) for lg in logits]
        ps = [jnp.exp(lg - m) for lg, m in zip(logits, ms)]
        dens = [jnp.sum(p, axis=-1, keepdims=True) for p in ps]
        nums = [_dot(p.astype(BF16), gather(vp_ref, pad_blocks + (n - 1) * dil, r, dil, 2).astype(BF16))
                for p, (r, n, dil) in zip(ps, coords)]
        for (r, n, dil), m, den, num in zip(coords, ms, dens, nums):
            scatter(m_ref.at[br], n * dil, r, dil, jnp.broadcast_to(m, (blk, HEAD_DIM)))
            scatter(den_ref.at[br], n * dil, r, dil, jnp.broadcast_to(den, (blk, HEAD_DIM)))
            scatter(num_ref.at[br], n * dil, r, dil, num)

    n_br = len(DSA_BRANCHES)
    logits_stage(0, 0, 0)
    for br in range(n_br):
        first_slot = (br * trips) % 2

        def overlapped(it, carry, br=br, first_slot=first_slot):
            slot = (first_slot + it) % 2
            softmax_stage(br, it, slot)
            logits_stage(br, it + 1, 1 - slot)
            return carry

        lax.fori_loop(0, trips - 1, overlapped, 0)
        last_slot = (first_slot + trips - 1) % 2
        softmax_stage(br, trips - 1, last_slot)
        if br + 1 < n_br:
            logits_stage(br + 1, 0, 1 - last_slot)

    def merge(j, carry):
        n_br = len(DSA_BRANCHES)
        ms = [m_ref[br, j] for br in range(n_br)]
        m_all = functools.reduce(jnp.maximum, ms)
        ws = [jnp.exp(mi - m_all) for mi in ms]
        den = functools.reduce(lambda x, y: x + y, [ws[br] * den_ref[br, j] for br in range(n_br)])
        num = functools.reduce(lambda x, y: x + y, [ws[br] * num_ref[br, j] for br in range(n_br)])
        rows = pl.ds(pl.multiple_of(j * blk, blk), blk)
        o_ref[rows, :] = (num / den).reshape(blk, HEAD_DIM).astype(o_ref.dtype)
        return carry

    lax.fori_loop(0, nj, merge, 0, unroll=2)


def _dilated_attention(qkv, rel_bias, *, batch, seq):
    hh = DSA_HEADS
    nj = seq // PERM_BLOCK
    pad_blocks = max(dil for _, dil in DSA_BRANCHES)
    bucket = _dsa_bucket_table()
    tiled = (PERM_RESIDUES, PERM_RUN, HEAD_DIM)

    def col(group):
        return pl.BlockSpec((seq, HEAD_DIM), lambda h, b: (b, group * hh + h))

    return pl.pallas_call(
        functools.partial(_dsa_kernel, seq=seq, pad_blocks=pad_blocks),
        grid=(hh, batch),
        in_specs=[pl.BlockSpec(memory_space=pltpu.SMEM),
                  col(0), col(1), col(2),
                  pl.BlockSpec(bucket.shape, lambda h, b: (0, 0, 0))],
        out_specs=pl.BlockSpec((seq, HEAD_DIM), lambda h, b: (b, h)),
        out_shape=jax.ShapeDtypeStruct((batch * seq, hh * HEAD_DIM), BF16),
        scratch_shapes=[pltpu.VMEM((nj,) + tiled, F32),
                        pltpu.VMEM((pad_blocks + nj,) + tiled, F32),
                        pltpu.VMEM((pad_blocks + nj,) + tiled, F32),
                        pltpu.VMEM((len(DSA_BRANCHES), nj) + tiled, F32),
                        pltpu.VMEM((len(DSA_BRANCHES), nj) + tiled, F32),
                        pltpu.VMEM((len(DSA_BRANCHES), nj) + tiled, F32),
                        pltpu.VMEM((2 * len(DSA_BRANCHES), DSA_BLOCK, 2 * DSA_BLOCK), F32),
                        pltpu.VMEM((2, DSA_UNROLL, DSA_BLOCK, 2 * DSA_BLOCK), F32)],
        compiler_params=_params("parallel", "arbitrary"),
        name="dilated_attention",
    )(rel_bias.astype(F32), qkv, qkv, qkv, bucket)


def _mem_kv_kernel(mem_ref, g_ref, w_ref, kg_ref, k_ref, v_ref):
    h = _rms_rows(mem_ref[...], g_ref[...]).astype(BF16)
    d = XA_HEADS * XA_HD
    kg = kg_ref[...]
    for hd in range(XA_HEADS):
        sl = slice(hd * XA_HD, (hd + 1) * XA_HD)
        k_ref[:, sl] = _rms_rows(_dot(h, w_ref[:, sl]), kg).astype(k_ref.dtype)
    v_ref[...] = _dot(h, w_ref[:, d:2 * d]).astype(v_ref.dtype)


def _mem_kv(mem2d, g, w_kv, layer, k_norm_g, *, batch):
    d = XA_HEADS * XA_HD
    return pl.pallas_call(
        _mem_kv_kernel,
        grid=(batch,),
        in_specs=[pl.BlockSpec((N_MEM, D_MODEL), lambda b: (b, 0)),
                  pl.BlockSpec((1, D_MODEL), lambda b: (0, 0)),
                  _layer_weight(w_kv, layer),
                  pl.BlockSpec((1, XA_HD), lambda b: (0, 0))],
        out_specs=[pl.BlockSpec((N_MEM, d), lambda b: (b, 0)),
                   pl.BlockSpec((N_MEM, d), lambda b: (b, 0))],
        out_shape=[jax.ShapeDtypeStruct((batch * N_MEM, d), BF16),
                   jax.ShapeDtypeStruct((batch * N_MEM, d), BF16)],
        compiler_params=_params("parallel"),
        name="mem_kv",
    )(mem2d, g.astype(F32).reshape(1, D_MODEL), w_kv, k_norm_g.astype(F32).reshape(1, XA_HD))


def _xattn_kernel(*refs, n_y, unpermute, oc):
    xin_ref = refs[0]
    y_refs = refs[1:1 + n_y]
    wmix_ref = refs[1 + n_y]
    rest = refs[2 + n_y:]
    if unpermute:
        p_ref, rest = rest[0], rest[1:]
    g_ref, wq_ref, qg_ref, mk_ref, mv_ref, wo_ref, o_ref, x_ref, att_ref = rest
    d = o_ref.shape[1]
    ys = [y_ref[...] for y_ref in y_refs]
    if unpermute:
        ys = [_permute_rows(p_ref[...], y) for y in ys]
    for n in range(d // oc):
        sl = slice(n * oc, (n + 1) * oc)
        acc = xin_ref[:, sl]
        off = 0
        for y in ys:
            acc = acc + _dot(y, wmix_ref[off:off + y.shape[1], sl])
            off += y.shape[1]
        x_ref[:, sl] = acc

    h = _rms_rows(x_ref[...], g_ref[...]).astype(BF16)
    qg = qg_ref[...]
    sls = [slice(hd * XA_HD, (hd + 1) * XA_HD) for hd in range(XA_HEADS)]
    qs = [_dot(h, wq_ref[:, sl]) for sl in sls]
    qs = [_rms_rows(q, qg).astype(BF16) for q in qs]
    logits = [_dot_nt(q, mk_ref[:, sl]) * (XA_HD ** -0.5) for q, sl in zip(qs, sls)]
    ps = [jnp.exp(lg - jnp.max(lg, axis=-1, keepdims=True)) for lg in logits]
    ps = [(p / jnp.sum(p, axis=-1, keepdims=True)).astype(BF16) for p in ps]
    for p, sl in zip(ps, sls):
        att_ref[:, sl] = _dot(p, mv_ref[:, sl]).astype(att_ref.dtype)
    for n in range(d // oc):
        sl = slice(n * oc, (n + 1) * oc)
        o_ref[:, sl] = x_ref[:, sl] + _dot(att_ref[...], wo_ref[:, sl])


def _mix_out_xattn(x2d, ys, w_mix, mix_layer, g, w_q, q_norm_g, mk, mv, w_o, layer, *, seq, tm,
                   unpermute=False, oc=256):
    t, d = x2d.shape
    nt = seq // tm
    n_y = len(ys)
    assert sum(y.shape[1] for y in ys) == w_mix.shape[1]
    const = lambda shape: pl.BlockSpec(shape, lambda i: (0, 0))
    in_specs = [pl.BlockSpec((tm, d), lambda i: (i, 0))]
    in_specs += [pl.BlockSpec((tm, y.shape[1]), lambda i: (i, 0)) for y in ys]
    in_specs.append(_layer_weight(w_mix, mix_layer))
    args = [x2d, *ys, w_mix]
    if unpermute:
        in_specs.append(const((PERM_BLOCK, PERM_BLOCK)))
        args.append(jnp.asarray(_block_permutation().T, BF16))
    in_specs += [const((1, d)), _layer_weight(w_q, layer), const((1, XA_HD)),
                 pl.BlockSpec((N_MEM, d), lambda i: (i // nt, 0)),
                 pl.BlockSpec((N_MEM, d), lambda i: (i // nt, 0)),
                 _layer_weight(w_o, layer)]
    args += [g.astype(F32).reshape(1, d), w_q, q_norm_g.astype(F32).reshape(1, XA_HD), mk, mv, w_o]
    return pl.pallas_call(
        functools.partial(_xattn_kernel, n_y=n_y, unpermute=unpermute, oc=oc),
        grid=(t // tm,),
        in_specs=in_specs,
        out_specs=pl.BlockSpec((tm, d), lambda i: (i, 0)),
        out_shape=jax.ShapeDtypeStruct((t, d), F32),
        scratch_shapes=[pltpu.VMEM((tm, d), F32), pltpu.VMEM((tm, d), BF16)],
        compiler_params=_params("parallel"),
        name="mix_out_xattn",
    )(*args)


def _ffn_kernel(x_ref, g_ref, win_ref, cw_ref, cb_ref, wout_ref, o_ref, carry_ref, act_ref, *, nt, fc, oc):
    tm, d = x_ref.shape
    assert CONV_W == 3

    @pl.when(pl.program_id(0) % nt == 0)
    def _():
        carry_ref[...] = jnp.zeros_like(carry_ref)

    h = _rms_rows(x_ref[...], g_ref[...]).astype(BF16)
    row = lax.broadcasted_iota(jnp.int32, (8, fc), 0)
    for c in range(D_FF // fc):
        sl = slice(c * fc, (c + 1) * fc)
        gate = _dot(h, win_ref[:, sl])
        up = _dot(h, win_ref[:, D_FF + c * fc:D_FF + (c + 1) * fc])
        prev = carry_ref[c]
        carry_ref[c] = gate[tm - 8:tm, :]
        w0, w1, w2, bias = cw_ref[0:1, sl], cw_ref[1:2, sl], cw_ref[2:3, sl], cb_ref[:, sl]
        conv = pltpu.roll(gate, 2, axis=0) * w0 + pltpu.roll(gate, 1, axis=0) * w1 + gate * w2 + bias
        top = gate[0:8, :]
        t1 = jnp.where(row == 0, prev[7:8, :], pltpu.roll(top, 1, axis=0))
        t2 = jnp.where(row == 0, prev[6:7, :], jnp.where(row == 1, prev[7:8, :], pltpu.roll(top, 2, axis=0)))
        conv = jnp.concatenate([t2 * w0 + t1 * w1 + top * w2 + bias, conv[8:, :]], axis=0)
        act_ref[:, sl] = (jax.nn.gelu(conv) * up).astype(BF16)
    for n in range(d // oc):
        sl = slice(n * oc, (n + 1) * oc)
        o_ref[:, sl] = x_ref[:, sl] + _dot(act_ref[...], wout_ref[:, sl])


def _ffn(x2d, g, w_in, conv_w, conv_b, w_out, layer, *, seq, tm, fc=256, oc=256):
    t, d = x2d.shape
    nt = seq // tm
    const = lambda shape: pl.BlockSpec(shape, lambda i: (0, 0))
    return pl.pallas_call(
        functools.partial(_ffn_kernel, nt=nt, fc=fc, oc=oc),
        grid=(t // tm,),
        in_specs=[pl.BlockSpec((tm, d), lambda i: (i, 0)),
                  const((1, d)), _layer_weight(w_in, layer), const((CONV_W, D_FF)), const((1, D_FF)),
                  _layer_weight(w_out, layer)],
        out_specs=pl.BlockSpec((tm, d), lambda i: (i, 0)),
        out_shape=jax.ShapeDtypeStruct((t, d), F32),
        scratch_shapes=[pltpu.VMEM((D_FF // fc, 8, fc), F32), pltpu.VMEM((tm, D_FF), BF16)],
        compiler_params=_params("arbitrary"),
        name="conv_ffn",
    )(x2d, g.astype(F32).reshape(1, d), w_in, conv_w.astype(F32), conv_b.astype(F32).reshape(1, D_FF), w_out)


def kernel(x, mem, mix_norm_g, ev_w_in, ev_ret_norm_g, ev_hg_norm_g, hg_lb_logits, ev_w_out, od_w_in, od_q_norm_g, od_k_norm_g, rel_bias, od_w_out, xa_norm_g, xa_mem_norm_g, xa_w_q, xa_w_kv, xa_q_norm_g, xa_k_norm_g, xa_w_o, ffn_norm_g, ffn_w_in, ffn_conv_w, ffn_conv_b, ffn_w_out):
    batch, seq, d = x.shape
    depth = mix_norm_g.shape[0]
    x2d = x.reshape(batch * seq, d)
    mem2d = mem.reshape(batch * mem.shape[1], d)
    tm = 512
    ev_w_in, ev_w_out, od_w_in, od_w_out, xa_w_q, xa_w_kv, xa_w_o, ffn_w_in, ffn_w_out = (
        w.astype(BF16) for w in (ev_w_in, ev_w_out, od_w_in, od_w_out, xa_w_q, xa_w_kv, xa_w_o, ffn_w_in, ffn_w_out))
    for l in range(depth):
        if l % 2 == 0:
            e = l // 2
            proj = _rms_proj(x2d, mix_norm_g[l], ev_w_in, e, tm=tm)
            y_ret = _retention(proj, ev_ret_norm_g[e], batch=batch, seq=seq, tm=2 * tm)
            y_hg = _hgrn(proj, hg_lb_logits, ev_hg_norm_g[e], layer=l, batch=batch, seq=seq, tm=2 * tm)
            ys, w_mix, mix_layer, permuted = [y_ret, y_hg], ev_w_out, e, False
        else:
            o = l // 2
            qk_gains = jnp.concatenate([jnp.tile(od_q_norm_g[o], DSA_HEADS), jnp.tile(od_k_norm_g[o], DSA_HEADS)])
            qkv = _rms_proj(x2d, mix_norm_g[l], od_w_in, o, tm=tm, permute=True, head_gains=qk_gains)
            att = _dilated_attention(qkv, rel_bias, batch=batch, seq=seq)
            ys, w_mix, mix_layer, permuted = [att], od_w_out, o, True
        mk, mv = _mem_kv(mem2d, xa_mem_norm_g[l], xa_w_kv, l, xa_k_norm_g[l], batch=batch)
        x2d = _mix_out_xattn(x2d, ys, w_mix, mix_layer, xa_norm_g[l], xa_w_q, xa_q_norm_g[l], mk, mv, xa_w_o, l,
                             seq=seq, tm=tm, unpermute=permuted)
        x2d = _ffn(x2d, ffn_norm_g[l], ffn_w_in, ffn_conv_w[l], ffn_conv_b[l], ffn_w_out, l, seq=seq, tm=tm)
    return x2d.reshape(batch, seq, d)
```

```python
import functools
import math

import jax
import jax.numpy as jnp
import numpy as np
from jax import lax
from jax.experimental import pallas as pl
from jax.experimental.pallas import tpu as pltpu

F32 = jnp.float32
BF16 = jnp.bfloat16

D_MODEL = 1024
N_MEM = 256
EPS = 1e-6
HEAD_DIM = 128
RET_HEADS = 4
RET_CHUNK = 128
RET_GROUP = 8
ROPE_BASE = 10000.0
HG_HEADS = 4
HG_CHUNK = 64
F_FLOOR = 1e-6
DSA_HEADS = 8
DSA_BRANCHES = ((128, 1), (512, 4), (2048, 16))
DSA_BLOCK = 128
DSA_UNROLL = 8
REL_BUCKETS = 32
REL_MAX_DIST = 2048
XA_HEADS = 4
XA_HD = 256
D_FF = 2816
CONV_W = 3
EV_GROUP = RET_HEADS * HEAD_DIM

VMEM_LIMIT_BYTES = 56 * 1024 * 1024
NEG_INF = float("-inf")

_NT = (((1,), (1,)), ((), ()))


def _params(*sem):
    return pltpu.CompilerParams(dimension_semantics=sem, vmem_limit_bytes=VMEM_LIMIT_BYTES)


def _rms_rows(x, g):
    return x * lax.rsqrt(jnp.mean(x * x, axis=-1, keepdims=True) + EPS) * g


def _dot(a, b):
    return jnp.dot(a, b, preferred_element_type=F32)


def _dot_nt(a, b):
    return lax.dot_general(a, b, _NT, preferred_element_type=F32)


PERM_BLOCK = 128
PERM_RESIDUES = 16
PERM_RUN = PERM_BLOCK // PERM_RESIDUES


def _block_permutation():
    rho = np.arange(PERM_BLOCK)
    src = PERM_RESIDUES * (rho % PERM_RUN) + rho // PERM_RUN
    p = np.zeros((PERM_BLOCK, PERM_BLOCK), np.float32)
    p[rho, src] = 1.0
    return p


def _permute_rows(p, x):
    blocks = [_dot(p, x[b * PERM_BLOCK:(b + 1) * PERM_BLOCK, :]).astype(BF16)
              for b in range(x.shape[0] // PERM_BLOCK)]
    return jnp.concatenate(blocks, axis=0)


def _rms_proj_kernel(*refs, col_chunk, permute, n_head_norm, n_rotary):
    x_ref, g_ref, w_ref = refs[:3]
    o_ref = refs[-1]
    extras = list(refs[3:-1])
    p_ref = extras.pop(0) if permute else None
    hg_ref = extras.pop(0) if n_head_norm else None
    cos_ref, sin_ref = (extras.pop(0), extras.pop(0)) if n_rotary else (None, None)
    h = _rms_rows(x_ref[...], g_ref[...]).astype(BF16)
    if permute:
        h = _permute_rows(p_ref[...], h)
    n = o_ref.shape[1]
    for c in range(n // col_chunk):
        sl = slice(c * col_chunk, (c + 1) * col_chunk)
        y = _dot(h, w_ref[:, sl])
        if c * col_chunk < n_head_norm:
            assert (c + 1) * col_chunk <= n_head_norm
            heads = [_rms_rows(y[:, k * HEAD_DIM:(k + 1) * HEAD_DIM],
                               hg_ref[:, c * col_chunk + k * HEAD_DIM:c * col_chunk + (k + 1) * HEAD_DIM])
                     for k in range(col_chunk // HEAD_DIM)]
            y = jnp.concatenate(heads, axis=1)
        if c * col_chunk < n_rotary:
            assert (c + 1) * col_chunk <= n_rotary and (n_rotary // 2) % col_chunk == 0
            cos, sin = cos_ref[...], sin_ref[...]
            heads = []
            for k in range(col_chunk // HEAD_DIM):
                yh = y[:, k * HEAD_DIM:(k + 1) * HEAD_DIM]
                yh = yh * cos + pltpu.roll(yh, HEAD_DIM // 2, axis=1) * sin
                heads.append(yh * (HEAD_DIM ** -0.5) if c * col_chunk >= n_rotary // 2 else yh)
            y = jnp.concatenate(heads, axis=1)
        o_ref[:, sl] = y.astype(o_ref.dtype)


def _layer_weight(w, layer):
    return pl.BlockSpec((None,) + w.shape[1:], lambda i: (layer, 0, 0))


def _rms_proj(x2d, g, w, layer, *, tm, out_dtype=F32, col_chunk=512, permute=False, head_gains=None,
              rotary=None):
    t, d = x2d.shape
    n = w.shape[2]
    in_specs = [
        pl.BlockSpec((tm, d), lambda i: (i, 0)),
        pl.BlockSpec((1, d), lambda i: (0, 0)),
        _layer_weight(w, layer),
    ]
    args = [x2d, g.reshape(1, d).astype(F32), w]
    if permute:
        in_specs.append(pl.BlockSpec((PERM_BLOCK, PERM_BLOCK), lambda i: (0, 0)))
        args.append(jnp.asarray(_block_permutation(), BF16))
    n_head_norm = 0 if head_gains is None else head_gains.shape[0]
    if n_head_norm:
        in_specs.append(pl.BlockSpec((1, n_head_norm), lambda i: (0, 0)))
        args.append(head_gains.astype(F32).reshape(1, n_head_norm))
    n_rotary = 0
    if rotary is not None:
        cos, sin, n_rotary = rotary
        nt = cos.shape[0] // tm
        in_specs += [pl.BlockSpec((tm, HEAD_DIM), lambda i: (i % nt, 0))] * 2
        args += [cos, sin]
    return pl.pallas_call(
        functools.partial(_rms_proj_kernel, col_chunk=col_chunk, permute=permute, n_head_norm=n_head_norm,
                          n_rotary=n_rotary),
        grid=(t // tm,),
        in_specs=in_specs,
        out_specs=pl.BlockSpec((tm, n), lambda i: (i, 0)),
        out_shape=jax.ShapeDtypeStruct((t, n), out_dtype),
        compiler_params=_params("parallel"),
        name="rms_proj",
    )(*args)


def _retention_consts(seq):
    h = jnp.arange(RET_HEADS, dtype=F32)
    log_gamma = jnp.log(1.0 - jnp.exp2(-5.0 - h))
    c = RET_CHUNK
    idx = jnp.arange(c, dtype=F32)
    diff = idx[:, None] - idx[None, :]
    decay = jnp.where(diff >= 0, jnp.exp(log_gamma[:, None, None] * jnp.maximum(diff, 0.0)), 0.0)
    xi = jnp.exp(log_gamma[:, None] * (idx + 1.0))
    zeta = jnp.exp(log_gamma[:, None] * (c - 1.0 - idx))
    chunk_decay = jnp.exp(log_gamma * c)
    ones = jnp.ones((RET_HEADS, c, HEAD_DIM), F32)
    half = HEAD_DIM // 2
    inv = 1.0 / (ROPE_BASE ** (jnp.arange(half, dtype=F32) / half))
    ang = jnp.arange(seq, dtype=F32)[:, None] * inv[None, :]
    cos, sin = jnp.cos(ang), jnp.sin(ang)
    return dict(
        decay=decay,
        xi=xi[:, :, None] * ones,
        zeta=zeta[:, :, None] * ones,
        chunk_decay=chunk_decay[:, None, None] * jnp.ones((RET_HEADS, 8, HEAD_DIM), F32),
        cos=jnp.concatenate([cos, cos], axis=-1),
        sin=jnp.concatenate([-sin, sin], axis=-1),
    )


def _retention_kernel(q_ref, k_ref, v_ref, g_ref, decay_ref, xi_ref, zeta_ref, cd_ref, ng_ref, o_ref, state_ref):
    @pl.when(pl.program_id(2) == 0)
    def _():
        state_ref[...] = jnp.zeros_like(state_ref)

    c = RET_CHUNK
    decay = decay_ref[...]
    xi = xi_ref[...]
    zeta = zeta_ref[...]
    cd = cd_ref[0:1, :]
    ng = ng_ref[...]

    def group(gi, carry):
        chunks = range(RET_GROUP)
        sls = [pl.ds(pl.multiple_of((gi * RET_GROUP + u) * c, c), c) for u in chunks]
        qs = [q_ref[sl, :].astype(BF16) for sl in sls]
        ks = [k_ref[sl, :] for sl in sls]
        vs = [v_ref[sl, :].astype(BF16) for sl in sls]
        scores = [(_dot_nt(q, k.astype(BF16)) * decay).astype(BF16) for q, k in zip(qs, ks)]
        inner = [_dot(s, v) for s, v in zip(scores, vs)]
        kvs = [_dot((k * zeta).T.astype(BF16), v) for k, v in zip(ks, vs)]
        state = state_ref[...]
        outs = []
        for u in chunks:
            outs.append(inner[u] + _dot(qs[u], state.astype(BF16)) * xi)
            state = kvs[u] + cd * state
        state_ref[...] = state
        for u in chunks:
            o = outs[u]
            mu = jnp.mean(o, axis=-1, keepdims=True)
            oc = o - mu
            var = jnp.mean(oc * oc, axis=-1, keepdims=True)
            y = oc * lax.rsqrt(var + EPS) * ng
            o_ref[sls[u], :] = (y * jax.nn.silu(g_ref[sls[u], :])).astype(o_ref.dtype)
        return carry

    assert (q_ref.shape[0] // c) % RET_GROUP == 0
    lax.fori_loop(0, q_ref.shape[0] // c // RET_GROUP, group, 0)


def _retention(proj, consts, norm_g, *, batch, seq, tm):
    nt = seq // tm
    hh = RET_HEADS

    def col(group):
        return pl.BlockSpec((tm, HEAD_DIM), lambda b, h, i: (b * nt + i, group * hh + h))

    def per_head(rows):
        return pl.BlockSpec((None, rows, HEAD_DIM), lambda b, h, i: (h, 0, 0))

    return pl.pallas_call(
        _retention_kernel,
        grid=(batch, hh, nt),
        in_specs=[col(0), col(1), col(2), col(3),
                  per_head(RET_CHUNK), per_head(RET_CHUNK), per_head(RET_CHUNK), per_head(8), per_head(1)],
        out_specs=pl.BlockSpec((tm, HEAD_DIM), lambda b, h, i: (b * nt + i, h)),
        out_shape=jax.ShapeDtypeStruct((batch * seq, EV_GROUP), BF16),
        scratch_shapes=[pltpu.VMEM((HEAD_DIM, HEAD_DIM), F32)],
        compiler_params=_params("parallel", "parallel", "arbitrary"),
        name="retention",
    )(proj, proj, proj, proj, consts["decay"], consts["xi"], consts["zeta"], consts["chunk_decay"],
      norm_g.astype(F32).reshape(hh, 1, HEAD_DIM))


HG_LEVELS = (64, 32, 16, 8)


HG_GROUP = 8


def _hgrn_consts():
    c = HG_CHUNK
    t = np.arange(c)[:, None]
    s = np.arange(c)[None, :]
    tril = (s <= t).astype(np.float32)
    masks = []
    for lvl, bs in enumerate(HG_LEVELS):
        same = (t // bs) == (s // bs)
        if lvl < len(HG_LEVELS) - 1:
            masks.append(same & (t % bs >= bs // 2) & (s % bs < bs // 2))
        else:
            masks.append(same & (s <= t))
    return jnp.asarray(tril, BF16), jnp.asarray(np.stack(masks).astype(np.float32))


def _split3(x):
    hi = x.astype(BF16)
    rest = x - hi.astype(F32)
    mid = rest.astype(BF16)
    lo = (rest - mid.astype(F32)).astype(BF16)
    return hi, mid, lo


def _block_reference_rows(cum, bs, row):
    pieces = [jnp.broadcast_to(cum[b0 + row:b0 + row + 1, :], (bs, cum.shape[1]))
              for b0 in range(0, cum.shape[0], bs)]
    return pieces[0] if len(pieces) == 1 else jnp.concatenate(pieces, axis=0)


def _hgrn_kernel(q_ref, f_ref, i_ref, g_ref, lbl_ref, tril_ref, mask_ref, ng_ref, o_ref, state_ref, *,
                 layer):
    @pl.when(pl.program_id(2) == 0)
    def _():
        state_ref[...] = jnp.zeros_like(state_ref)

    c = HG_CHUNK
    logits = lbl_ref[...]
    e = jnp.exp(logits - jnp.max(logits, axis=0, keepdims=True))
    lb = jnp.sum(e[:layer + 1, :], axis=0, keepdims=True) / jnp.sum(e, axis=0, keepdims=True)
    tril = tril_ref[...]
    masks = [mask_ref[l] > 0.5 for l in range(len(HG_LEVELS))]
    ng = ng_ref[...]

    def group(gi, carry):
        chunks = range(HG_GROUP)
        sls = [pl.ds(pl.multiple_of((gi * HG_GROUP + u) * c, c), c) for u in chunks]
        fs = [lb + (1.0 - lb) * jax.nn.sigmoid(f_ref[sl, :]) for sl in sls]
        keys = [1.0 - f for f in fs]
        splits = [_split3(jnp.log(jnp.maximum(f, F_FLOOR))) for f in fs]
        cums = [_dot(tril, hi) + (_dot(tril, mid) + _dot(tril, lo)) for hi, mid, lo in splits]
        qs = [q_ref[sl, :] for sl in sls]
        scores = [jnp.zeros((c, c), F32) for _ in chunks]
        for lvl, bs in enumerate(HG_LEVELS):
            for u in chunks:
                a = cums[u] - _block_reference_rows(cums[u], bs, bs // 2 - 1)
                if lvl < len(HG_LEVELS) - 1:
                    decay = jnp.exp(-jnp.abs(a))
                    qa = qs[u] * decay
                    ka = keys[u] * decay
                else:
                    qa = qs[u] * jnp.exp(a)
                    ka = keys[u] * jnp.exp(-a)
                scores[u] = jnp.where(masks[lvl], _dot_nt(qa.astype(BF16), ka.astype(BF16)), scores[u])
        vs = [i_ref[sl, :] for sl in sls]
        lasts = [cum[c - 1:c, :] for cum in cums]
        intra = [_dot(scores[u].astype(BF16), vs[u].astype(BF16)) for u in chunks]
        kvs = [_dot(vs[u].T.astype(BF16), (keys[u] * jnp.exp(lasts[u] - cums[u])).astype(BF16)) for u in chunks]
        qcs = [(qs[u] * jnp.exp(cums[u])).astype(BF16) for u in chunks]
        state_t = state_ref[...]
        outs = []
        for u in chunks:
            outs.append(intra[u] + _dot_nt(qcs[u], state_t.astype(BF16)))
            state_t = jnp.exp(lasts[u]) * state_t + kvs[u]
        state_ref[...] = state_t
        for u in chunks:
            out = outs[u]
            y = out * lax.rsqrt(jnp.mean(out * out, axis=-1, keepdims=True) + EPS) * ng
            o_ref[sls[u], :] = (y * jax.nn.silu(g_ref[sls[u], :])).astype(o_ref.dtype)
        return carry

    assert (q_ref.shape[0] // c) % HG_GROUP == 0
    lax.fori_loop(0, q_ref.shape[0] // c // HG_GROUP, group, 0)


def _hgrn(proj, lb_logits, norm_g, *, layer, batch, seq, tm):
    cum_w, masks = _hgrn_consts()
    nt = seq // tm
    hh = HG_HEADS
    slots = lb_logits.shape[0]
    lbl = lb_logits.astype(F32).reshape(slots, hh, HEAD_DIM).transpose(1, 0, 2)

    def col(group):
        return pl.BlockSpec((tm, HEAD_DIM), lambda b, h, i: (b * nt + i, group * hh + h))

    return pl.pallas_call(
        functools.partial(_hgrn_kernel, layer=layer),
        grid=(batch, hh, nt),
        in_specs=[col(4), col(5), col(6), col(7),
                  pl.BlockSpec((None, slots, HEAD_DIM), lambda b, h, i: (h, 0, 0)),
                  pl.BlockSpec(cum_w.shape, lambda b, h, i: (0, 0)),
                  pl.BlockSpec(masks.shape, lambda b, h, i: (0, 0, 0)),
                  pl.BlockSpec((None, 1, HEAD_DIM), lambda b, h, i: (h, 0, 0))],
        out_specs=pl.BlockSpec((tm, HEAD_DIM), lambda b, h, i: (b * nt + i, h)),
        out_shape=jax.ShapeDtypeStruct((batch * seq, EV_GROUP), BF16),
        scratch_shapes=[pltpu.VMEM((HEAD_DIM, HEAD_DIM), F32)],
        compiler_params=_params("parallel", "parallel", "arbitrary"),
        name="hgrn2",
    )(proj, proj, proj, proj, lbl, cum_w, masks, norm_g.astype(F32).reshape(hh, 1, HEAD_DIM))


def _t5_bucket(dist):
    exact = REL_BUCKETS // 2
    d = np.maximum(dist, 0)
    log_ratio = (np.log(np.maximum(d, 1).astype(np.float32) / np.float32(exact))
                 / np.float32(math.log(REL_MAX_DIST / exact)))
    large = np.minimum(exact + (log_ratio * np.float32(REL_BUCKETS - exact)).astype(np.int32), REL_BUCKETS - 1)
    return np.where(d < exact, d, large)


def _dsa_row_order(dil):
    rho = np.arange(DSA_BLOCK)
    run = PERM_RUN * dil
    c, j, a = rho // run, (rho % run) // PERM_RUN, rho % PERM_RUN
    return (PERM_BLOCK // dil) * j + (PERM_RESIDUES // dil) * a + c


def _dsa_bucket_table():
    blk = DSA_BLOCK
    tabs = []
    for window, dil in DSA_BRANCHES:
        order = _dsa_row_order(dil)
        qi = order[:, None]
        ki = np.concatenate([order, blk + order])[None, :]
        delta = qi + blk - ki
        n_back = window // dil
        valid = (delta >= 0) & (delta <= n_back)
        tabs.append(np.where(valid, _t5_bucket(delta * dil), -1).astype(np.int32))
    return jnp.asarray(np.stack(tabs))


def _dsa_kernel(rb_ref, q_ref, k_ref, v_ref, bucket_ref, o_ref,
                qn_ref, kp_ref, vp_ref, m_ref, den_ref, num_ref, bias_ref, lg_ref, *, seq, pad_blocks):
    blk = DSA_BLOCK
    nj = seq // PERM_BLOCK
    tiled = (nj, PERM_RESIDUES, PERM_RUN, HEAD_DIM)
    head = pl.program_id(0)
    qn_ref[...] = q_ref[...].reshape(tiled)
    kp_ref[pad_blocks:pad_blocks + nj] = k_ref[...].reshape(tiled)
    vp_ref[pad_blocks:pad_blocks + nj] = v_ref[...].reshape(tiled)

    @pl.when(pl.program_id(1) == 0)
    def _():
        zeros = jnp.zeros((pad_blocks,) + tiled[1:], F32)
        kp_ref[0:pad_blocks] = zeros
        vp_ref[0:pad_blocks] = zeros
        first_block_keys = lax.broadcasted_iota(jnp.int32, (blk, 2 * blk), 1) >= blk
        for br in range(len(DSA_BRANCHES)):
            bucket = bucket_ref[br]
            bias = jnp.full(bucket.shape, NEG_INF, F32)
            for b in range(REL_BUCKETS):
                bias = jnp.where(bucket == b, rb_ref[b, head], bias)
            bias_ref[2 * br] = bias
            bias_ref[2 * br + 1] = jnp.where(first_block_keys, bias, NEG_INF)

    scale = HEAD_DIM ** -0.5

    def gather(ref, j0, r, dil, spans):
        if dil == 1:
            return ref[pl.ds(j0, spans)].reshape(spans * blk, HEAD_DIM)
        pieces = [ref[pl.ds(j0 + s * dil, dil), r + dil * c]
                  for s in range(spans) for c in range(PERM_RESIDUES // dil)]
        return jnp.concatenate(pieces, axis=0).reshape(spans * blk, HEAD_DIM)

    def scatter(ref, j0, r, dil, val):
        if dil == 1:
            ref[j0] = val.reshape(tiled[1:])
            return
        run = PERM_RUN * dil
        for c in range(PERM_RESIDUES // dil):
            ref[pl.ds(j0, dil), r + dil * c] = val[c * run:(c + 1) * run, :].reshape(dil, PERM_RUN, HEAD_DIM)

    trips = nj // DSA_UNROLL
    assert nj % DSA_UNROLL == 0 and all(dil <= pad_blocks for _, dil in DSA_BRANCHES)

    def block_coords(br, it, u):
        dil = DSA_BRANCHES[br][1]
        nb = nj // dil
        idx = it * DSA_UNROLL + u
        return idx // nb, idx % nb, dil

    def logits_stage(br, it, slot):
        for u in range(DSA_UNROLL):
            r, n, dil = block_coords(br, it, u)
            qb = gather(qn_ref, n * dil, r, dil, 1)
            kb = gather(kp_ref, pad_blocks + (n - 1) * dil, r, dil, 2)
            bias = bias_ref[2 * br + jnp.where(n == 0, 1, 0)]
            lg_ref[slot, u] = _dot_nt(qb.astype(BF16), kb.astype(BF16)) * scale + bias

    def softmax_stage(br, it, slot):
        blocks = range(DSA_UNROLL)
        coords = [block_coords(br, it, u) for u in blocks]
        logits = [lg_ref[slot, u] for u in blocks]
        ms = [jnp.max(lg, axis=-1, keepdims=True) for lg in logits]
        ps = [jnp.exp(lg - m) for lg, m in zip(logits, ms)]
        dens = [jnp.sum(p, axis=-1, keepdims=True) for p in ps]
        nums = [_dot(p.astype(BF16), gather(vp_ref, pad_blocks + (n - 1) * dil, r, dil, 2).astype(BF16))
                for p, (r, n, dil) in zip(ps, coords)]
        for (r, n, dil), m, den, num in zip(coords, ms, dens, nums):
            scatter(m_ref.at[br], n * dil, r, dil, jnp.broadcast_to(m, (blk, HEAD_DIM)))
            scatter(den_ref.at[br], n * dil, r, dil, jnp.broadcast_to(den, (blk, HEAD_DIM)))
            scatter(num_ref.at[br], n * dil, r, dil, num)

    n_br = len(DSA_BRANCHES)
    logits_stage(0, 0, 0)
    for br in range(n_br):
        first_slot = (br * trips) % 2

        def overlapped(it, carry, br=br, first_slot=first_slot):
            slot = (first_slot + it) % 2
            softmax_stage(br, it, slot)
            logits_stage(br, it + 1, 1 - slot)
            return carry

        lax.fori_loop(0, trips - 1, overlapped, 0)
        last_slot = (first_slot + trips - 1) % 2
        softmax_stage(br, trips - 1, last_slot)
        if br + 1 < n_br:
            logits_stage(br + 1, 0, 1 - last_slot)

    def merge(j, carry):
        n_br = len(DSA_BRANCHES)
        ms = [m_ref[br, j] for br in range(n_br)]
        m_all = functools.reduce(jnp.maximum, ms)
        ws = [jnp.exp(mi - m_all) for mi in ms]
        den = functools.reduce(lambda x, y: x + y, [ws[br] * den_ref[br, j] for br in range(n_br)])
        num = functools.reduce(lambda x, y: x + y, [ws[br] * num_ref[br, j] for br in range(n_br)])
        rows = pl.ds(pl.multiple_of(j * blk, blk), blk)
        o_ref[rows, :] = (num / den).reshape(blk, HEAD_DIM).astype(o_ref.dtype)
        return carry

    lax.fori_loop(0, nj, merge, 0, unroll=2)


def _dilated_attention(qkv, rel_bias, *, batch, seq):
    hh = DSA_HEADS
    nj = seq // PERM_BLOCK
    pad_blocks = max(dil for _, dil in DSA_BRANCHES)
    bucket = _dsa_bucket_table()
    tiled = (PERM_RESIDUES, PERM_RUN, HEAD_DIM)

    def col(group):
        return pl.BlockSpec((seq, HEAD_DIM), lambda h, b: (b, group * hh + h))

    return pl.pallas_call(
        functools.partial(_dsa_kernel, seq=seq, pad_blocks=pad_blocks),
        grid=(hh, batch),
        in_specs=[pl.BlockSpec(memory_space=pltpu.SMEM),
                  col(0), col(1), col(2),
                  pl.BlockSpec(bucket.shape, lambda h, b: (0, 0, 0))],
        out_specs=pl.BlockSpec((seq, HEAD_DIM), lambda h, b: (b, h)),
        out_shape=jax.ShapeDtypeStruct((batch * seq, hh * HEAD_DIM), BF16),
        scratch_shapes=[pltpu.VMEM((nj,) + tiled, F32),
                        pltpu.VMEM((pad_blocks + nj,) + tiled, F32),
                        pltpu.VMEM((pad_blocks + nj,) + tiled, F32),
                        pltpu.VMEM((len(DSA_BRANCHES), nj) + tiled, F32),
                        pltpu.VMEM((len(DSA_BRANCHES), nj) + tiled, F32),
                        pltpu.VMEM((len(DSA_BRANCHES), nj) + tiled, F32),
                        pltpu.VMEM((2 * len(DSA_BRANCHES), DSA_BLOCK, 2 * DSA_BLOCK), F32),
                        pltpu.VMEM((2, DSA_UNROLL, DSA_BLOCK, 2 * DSA_BLOCK), F32)],
        compiler_params=_params("parallel", "arbitrary"),
        name="dilated_attention",
    )(rel_bias.astype(F32), qkv, qkv, qkv, bucket)


def _mem_kv_kernel(mem_ref, g_ref, w_ref, kg_ref, k_ref, v_ref):
    h = _rms_rows(mem_ref[...], g_ref[...]).astype(BF16)
    d = XA_HEADS * XA_HD
    kg = kg_ref[...]
    for hd in range(XA_HEADS):
        sl = slice(hd * XA_HD, (hd + 1) * XA_HD)
        k_ref[:, sl] = _rms_rows(_dot(h, w_ref[:, sl]), kg).astype(k_ref.dtype)
    v_ref[...] = _dot(h, w_ref[:, d:2 * d]).astype(v_ref.dtype)


def _mem_kv(mem2d, g, w_kv, layer, k_norm_g, *, batch):
    d = XA_HEADS * XA_HD
    return pl.pallas_call(
        _mem_kv_kernel,
        grid=(batch,),
        in_specs=[pl.BlockSpec((N_MEM, D_MODEL), lambda b: (b, 0)),
                  pl.BlockSpec((1, D_MODEL), lambda b: (0, 0)),
                  _layer_weight(w_kv, layer),
                  pl.BlockSpec((1, XA_HD), lambda b: (0, 0))],
        out_specs=[pl.BlockSpec((N_MEM, d), lambda b: (b, 0)),
                   pl.BlockSpec((N_MEM, d), lambda b: (b, 0))],
        out_shape=[jax.ShapeDtypeStruct((batch * N_MEM, d), BF16),
                   jax.ShapeDtypeStruct((batch * N_MEM, d), BF16)],
        compiler_params=_params("parallel"),
        name="mem_kv",
    )(mem2d, g.astype(F32).reshape(1, D_MODEL), w_kv, k_norm_g.astype(F32).reshape(1, XA_HD))


def _xattn_kernel(*refs, n_y, unpermute, oc):
    xin_ref = refs[0]
    y_refs = refs[1:1 + n_y]
    wmix_ref = refs[1 + n_y]
    rest = refs[2 + n_y:]
    if unpermute:
        p_ref, rest = rest[0], rest[1:]
    g_ref, wq_ref, qg_ref, mk_ref, mv_ref, wo_ref, o_ref, x_ref, att_ref = rest
    d = o_ref.shape[1]
    ys = [y_ref[...] for y_ref in y_refs]
    if unpermute:
        ys = [_permute_rows(p_ref[...], y) for y in ys]
    for n in range(d // oc):
        sl = slice(n * oc, (n + 1) * oc)
        acc = xin_ref[:, sl]
        off = 0
        for y in ys:
            acc = acc + _dot(y, wmix_ref[off:off + y.shape[1], sl])
            off += y.shape[1]
        x_ref[:, sl] = acc

    h = _rms_rows(x_ref[...], g_ref[...]).astype(BF16)
    qg = qg_ref[...]
    sls = [slice(hd * XA_HD, (hd + 1) * XA_HD) for hd in range(XA_HEADS)]
    qs = [_dot(h, wq_ref[:, sl]) for sl in sls]
    qs = [_rms_rows(q, qg).astype(BF16) for q in qs]
    logits = [_dot_nt(q, mk_ref[:, sl]) * (XA_HD ** -0.5) for q, sl in zip(qs, sls)]
    ps = [jnp.exp(lg - jnp.max(lg, axis=-1, keepdims=True)) for lg in logits]
    ps = [(p / jnp.sum(p, axis=-1, keepdims=True)).astype(BF16) for p in ps]
    for p, sl in zip(ps, sls):
        att_ref[:, sl] = _dot(p, mv_ref[:, sl]).astype(att_ref.dtype)
    for n in range(d // oc):
        sl = slice(n * oc, (n + 1) * oc)
        o_ref[:, sl] = x_ref[:, sl] + _dot(att_ref[...], wo_ref[:, sl])


def _mix_out_xattn(x2d, ys, w_mix, mix_layer, g, w_q, q_norm_g, mk, mv, w_o, layer, *, seq, tm,
                   unpermute=False, oc=256):
    t, d = x2d.shape
    nt = seq // tm
    n_y = len(ys)
    assert sum(y.shape[1] for y in ys) == w_mix.shape[1]
    const = lambda shape: pl.BlockSpec(shape, lambda i: (0, 0))
    in_specs = [pl.BlockSpec((tm, d), lambda i: (i, 0))]
    in_specs += [pl.BlockSpec((tm, y.shape[1]), lambda i: (i, 0)) for y in ys]
    in_specs.append(_layer_weight(w_mix, mix_layer))
    args = [x2d, *ys, w_mix]
    if unpermute:
        in_specs.append(const((PERM_BLOCK, PERM_BLOCK)))
        args.append(jnp.asarray(_block_permutation().T, BF16))
    in_specs += [const((1, d)), _layer_weight(w_q, layer), const((1, XA_HD)),
                 pl.BlockSpec((N_MEM, d), lambda i: (i // nt, 0)),
                 pl.BlockSpec((N_MEM, d), lambda i: (i // nt, 0)),
                 _layer_weight(w_o, layer)]
    args += [g.astype(F32).reshape(1, d), w_q, q_norm_g.astype(F32).reshape(1, XA_HD), mk, mv, w_o]
    return pl.pallas_call(
        functools.partial(_xattn_kernel, n_y=n_y, unpermute=unpermute, oc=oc),
        grid=(t // tm,),
        in_specs=in_specs,
        out_specs=pl.BlockSpec((tm, d), lambda i: (i, 0)),
        out_shape=jax.ShapeDtypeStruct((t, d), F32),
        scratch_shapes=[pltpu.VMEM((tm, d), F32), pltpu.VMEM((tm, d), BF16)],
        compiler_params=_params("parallel"),
        name="mix_out_xattn",
    )(*args)


def _ffn_kernel(x_ref, g_ref, win_ref, cw_ref, cb_ref, wout_ref, o_ref, carry_ref, act_ref, *, nt, fc, oc):
    tm, d = x_ref.shape
    assert CONV_W == 3

    @pl.when(pl.program_id(0) % nt == 0)
    def _():
        carry_ref[...] = jnp.zeros_like(carry_ref)

    h = _rms_rows(x_ref[...], g_ref[...]).astype(BF16)
    row = lax.broadcasted_iota(jnp.int32, (8, fc), 0)
    for c in range(D_FF // fc):
        sl = slice(c * fc, (c + 1) * fc)
        gate = _dot(h, win_ref[:, sl])
        up = _dot(h, win_ref[:, D_FF + c * fc:D_FF + (c + 1) * fc])
        prev = carry_ref[c]
        carry_ref[c] = gate[tm - 8:tm, :]
        w0, w1, w2, bias = cw_ref[0:1, sl], cw_ref[1:2, sl], cw_ref[2:3, sl], cb_ref[:, sl]
        conv = pltpu.roll(gate, 2, axis=0) * w0 + pltpu.roll(gate, 1, axis=0) * w1 + gate * w2 + bias
        top = gate[0:8, :]
        t1 = jnp.where(row == 0, prev[7:8, :], pltpu.roll(top, 1, axis=0))
        t2 = jnp.where(row == 0, prev[6:7, :], jnp.where(row == 1, prev[7:8, :], pltpu.roll(top, 2, axis=0)))
        conv = jnp.concatenate([t2 * w0 + t1 * w1 + top * w2 + bias, conv[8:, :]], axis=0)
        act_ref[:, sl] = (jax.nn.gelu(conv) * up).astype(BF16)
    for n in range(d // oc):
        sl = slice(n * oc, (n + 1) * oc)
        o_ref[:, sl] = x_ref[:, sl] + _dot(act_ref[...], wout_ref[:, sl])


def _ffn(x2d, g, w_in, conv_w, conv_b, w_out, layer, *, seq, tm, fc=256, oc=256):
    t, d = x2d.shape
    nt = seq // tm
    const = lambda shape: pl.BlockSpec(shape, lambda i: (0, 0))
    return pl.pallas_call(
        functools.partial(_ffn_kernel, nt=nt, fc=fc, oc=oc),
        grid=(t // tm,),
        in_specs=[pl.BlockSpec((tm, d), lambda i: (i, 0)),
                  const((1, d)), _layer_weight(w_in, layer), const((CONV_W, D_FF)), const((1, D_FF)),
                  _layer_weight(w_out, layer)],
        out_specs=pl.BlockSpec((tm, d), lambda i: (i, 0)),
        out_shape=jax.ShapeDtypeStruct((t, d), F32),
        scratch_shapes=[pltpu.VMEM((D_FF // fc, 8, fc), F32), pltpu.VMEM((tm, D_FF), BF16)],
        compiler_params=_params("arbitrary"),
        name="conv_ffn",
    )(x2d, g.astype(F32).reshape(1, d), w_in, conv_w.astype(F32), conv_b.astype(F32).reshape(1, D_FF), w_out)


def kernel(x, mem, mix_norm_g, ev_w_in, ev_ret_norm_g, ev_hg_norm_g, hg_lb_logits, ev_w_out, od_w_in, od_q_norm_g, od_k_norm_g, rel_bias, od_w_out, xa_norm_g, xa_mem_norm_g, xa_w_q, xa_w_kv, xa_q_norm_g, xa_k_norm_g, xa_w_o, ffn_norm_g, ffn_w_in, ffn_conv_w, ffn_conv_b, ffn_w_out):
    batch, seq, d = x.shape
    depth = mix_norm_g.shape[0]
    x2d = x.reshape(batch * seq, d)
    mem2d = mem.reshape(batch * mem.shape[1], d)
    tm = 512
    ev_w_in, ev_w_out, od_w_in, od_w_out, xa_w_q, xa_w_kv, xa_w_o, ffn_w_in, ffn_w_out = (
        w.astype(BF16) for w in (ev_w_in, ev_w_out, od_w_in, od_w_out, xa_w_q, xa_w_kv, xa_w_o, ffn_w_in, ffn_w_out))
    for l in range(depth):
        if l % 2 == 0:
            e = l // 2
            consts = _retention_consts(seq)
            proj = _rms_proj(x2d, mix_norm_g[l], ev_w_in, e, tm=tm,
                             rotary=(consts["cos"], consts["sin"], 2 * EV_GROUP))
            y_ret = _retention(proj, consts, ev_ret_norm_g[e], batch=batch, seq=seq, tm=2 * tm)
            y_hg = _hgrn(proj, hg_lb_logits, ev_hg_norm_g[e], layer=l, batch=batch, seq=seq, tm=2 * tm)
            ys, w_mix, mix_layer, permuted = [y_ret, y_hg], ev_w_out, e, False
        else:
            o = l // 2
            qk_gains = jnp.concatenate([jnp.tile(od_q_norm_g[o], DSA_HEADS), jnp.tile(od_k_norm_g[o], DSA_HEADS)])
            qkv = _rms_proj(x2d, mix_norm_g[l], od_w_in, o, tm=tm, permute=True, head_gains=qk_gains)
            att = _dilated_attention(qkv, rel_bias, batch=batch, seq=seq)
            ys, w_mix, mix_layer, permuted = [att], od_w_out, o, True
        mk, mv = _mem_kv(mem2d, xa_mem_norm_g[l], xa_w_kv, l, xa_k_norm_g[l], batch=batch)
        x2d = _mix_out_xattn(x2d, ys, w_mix, mix_layer, xa_norm_g[l], xa_w_q, xa_q_norm_g[l], mk, mv, xa_w_o, l,
                             seq=seq, tm=tm, unpermute=permuted)
        x2d = _ffn(x2d, ffn_norm_g[l], ffn_w_in, ffn_conv_w[l], ffn_conv_b[l], ffn_w_out, l, seq=seq, tm=tm)
    return x2d.reshape(batch, seq, d)
```

```python
import functools
import math

import jax
import jax.numpy as jnp
import numpy as np
from jax import lax
from jax.experimental import pallas as pl
from jax.experimental.pallas import tpu as pltpu

F32 = jnp.float32
BF16 = jnp.bfloat16

D_MODEL = 1024
N_MEM = 256
EPS = 1e-6
HEAD_DIM = 128
RET_HEADS = 4
RET_CHUNK = 128
RET_GROUP = 8
ROPE_BASE = 10000.0
HG_HEADS = 4
HG_CHUNK = 64
F_FLOOR = 1e-6
DSA_HEADS = 8
DSA_BRANCHES = ((128, 1), (512, 4), (2048, 16))
DSA_BLOCK = 128
DSA_UNROLL = 8
REL_BUCKETS = 32
REL_MAX_DIST = 2048
XA_HEADS = 4
XA_HD = 256
D_FF = 2816
CONV_W = 3
EV_GROUP = RET_HEADS * HEAD_DIM

VMEM_LIMIT_BYTES = 56 * 1024 * 1024
NEG_INF = float("-inf")

_NT = (((1,), (1,)), ((), ()))


def _params(*sem):
    return pltpu.CompilerParams(dimension_semantics=sem, vmem_limit_bytes=VMEM_LIMIT_BYTES)


def _rms_rows(x, g):
    return x * lax.rsqrt(jnp.mean(x * x, axis=-1, keepdims=True) + EPS) * g


def _dot(a, b):
    return jnp.dot(a, b, preferred_element_type=F32)


def _dot_nt(a, b):
    return lax.dot_general(a, b, _NT, preferred_element_type=F32)


PERM_BLOCK = 128
PERM_RESIDUES = 16
PERM_RUN = PERM_BLOCK // PERM_RESIDUES


def _block_permutation():
    rho = np.arange(PERM_BLOCK)
    src = PERM_RESIDUES * (rho % PERM_RUN) + rho // PERM_RUN
    p = np.zeros((PERM_BLOCK, PERM_BLOCK), np.float32)
    p[rho, src] = 1.0
    return p


def _permute_rows(p, x):
    blocks = [_dot(p, x[b * PERM_BLOCK:(b + 1) * PERM_BLOCK, :]).astype(BF16)
              for b in range(x.shape[0] // PERM_BLOCK)]
    return jnp.concatenate(blocks, axis=0)


def _rms_proj_kernel(*refs, col_chunk, permute, n_head_norm, n_rotary, silu_chunks):
    x_ref, g_ref, w_ref = refs[:3]
    o_ref = refs[-1]
    extras = list(refs[3:-1])
    p_ref = extras.pop(0) if permute else None
    hg_ref = extras.pop(0) if n_head_norm else None
    cos_ref, sin_ref = (extras.pop(0), extras.pop(0)) if n_rotary else (None, None)
    h = _rms_rows(x_ref[...], g_ref[...]).astype(BF16)
    if permute:
        h = _permute_rows(p_ref[...], h)
    n = o_ref.shape[1]
    for c in range(n // col_chunk):
        sl = slice(c * col_chunk, (c + 1) * col_chunk)
        y = _dot(h, w_ref[:, sl])
        if c * col_chunk < n_head_norm:
            assert (c + 1) * col_chunk <= n_head_norm
            heads = [_rms_rows(y[:, k * HEAD_DIM:(k + 1) * HEAD_DIM],
                               hg_ref[:, c * col_chunk + k * HEAD_DIM:c * col_chunk + (k + 1) * HEAD_DIM])
                     for k in range(col_chunk // HEAD_DIM)]
            y = jnp.concatenate(heads, axis=1)
        if c * col_chunk < n_rotary:
            assert (c + 1) * col_chunk <= n_rotary and (n_rotary // 2) % col_chunk == 0
            cos, sin = cos_ref[...], sin_ref[...]
            heads = []
            for k in range(col_chunk // HEAD_DIM):
                yh = y[:, k * HEAD_DIM:(k + 1) * HEAD_DIM]
                yh = yh * cos + pltpu.roll(yh, HEAD_DIM // 2, axis=1) * sin
                heads.append(yh * (HEAD_DIM ** -0.5) if c * col_chunk >= n_rotary // 2 else yh)
            y = jnp.concatenate(heads, axis=1)
        if c in silu_chunks:
            y = jax.nn.silu(y)
        o_ref[:, sl] = y.astype(o_ref.dtype)


def _layer_weight(w, layer):
    return pl.BlockSpec((None,) + w.shape[1:], lambda i: (layer, 0, 0))


def _rms_proj(x2d, g, w, layer, *, tm, out_dtype=F32, col_chunk=512, permute=False, head_gains=None,
              rotary=None, silu_chunks=()):
    t, d = x2d.shape
    n = w.shape[2]
    in_specs = [
        pl.BlockSpec((tm, d), lambda i: (i, 0)),
        pl.BlockSpec((1, d), lambda i: (0, 0)),
        _layer_weight(w, layer),
    ]
    args = [x2d, g.reshape(1, d).astype(F32), w]
    if permute:
        in_specs.append(pl.BlockSpec((PERM_BLOCK, PERM_BLOCK), lambda i: (0, 0)))
        args.append(jnp.asarray(_block_permutation(), BF16))
    n_head_norm = 0 if head_gains is None else head_gains.shape[0]
    if n_head_norm:
        in_specs.append(pl.BlockSpec((1, n_head_norm), lambda i: (0, 0)))
        args.append(head_gains.astype(F32).reshape(1, n_head_norm))
    n_rotary = 0
    if rotary is not None:
        cos, sin, n_rotary = rotary
        nt = cos.shape[0] // tm
        in_specs += [pl.BlockSpec((tm, HEAD_DIM), lambda i: (i % nt, 0))] * 2
        args += [cos, sin]
    return pl.pallas_call(
        functools.partial(_rms_proj_kernel, col_chunk=col_chunk, permute=permute, n_head_norm=n_head_norm,
                          n_rotary=n_rotary, silu_chunks=tuple(silu_chunks)),
        grid=(t // tm,),
        in_specs=in_specs,
        out_specs=pl.BlockSpec((tm, n), lambda i: (i, 0)),
        out_shape=jax.ShapeDtypeStruct((t, n), out_dtype),
        compiler_params=_params("parallel"),
        name="rms_proj",
    )(*args)


def _retention_consts(seq):
    h = jnp.arange(RET_HEADS, dtype=F32)
    log_gamma = jnp.log(1.0 - jnp.exp2(-5.0 - h))
    c = RET_CHUNK
    idx = jnp.arange(c, dtype=F32)
    diff = idx[:, None] - idx[None, :]
    decay = jnp.where(diff >= 0, jnp.exp(log_gamma[:, None, None] * jnp.maximum(diff, 0.0)), 0.0)
    xi = jnp.exp(log_gamma[:, None] * (idx + 1.0))
    zeta = jnp.exp(log_gamma[:, None] * (c - 1.0 - idx))
    chunk_decay = jnp.exp(log_gamma * c)
    ones = jnp.ones((RET_HEADS, c, HEAD_DIM), F32)
    half = HEAD_DIM // 2
    inv = 1.0 / (ROPE_BASE ** (jnp.arange(half, dtype=F32) / half))
    ang = jnp.arange(seq, dtype=F32)[:, None] * inv[None, :]
    cos, sin = jnp.cos(ang), jnp.sin(ang)
    return dict(
        decay=decay,
        xi=xi[:, :, None] * ones,
        zeta=zeta[:, :, None] * ones,
        chunk_decay=chunk_decay[:, None, None] * jnp.ones((RET_HEADS, 8, HEAD_DIM), F32),
        cos=jnp.concatenate([cos, cos], axis=-1),
        sin=jnp.concatenate([-sin, sin], axis=-1),
    )


def _retention_kernel(q_ref, k_ref, v_ref, g_ref, decay_ref, xi_ref, zeta_ref, cd_ref, ng_ref, o_ref, state_ref):
    @pl.when(pl.program_id(2) == 0)
    def _():
        state_ref[...] = jnp.zeros_like(state_ref)

    c = RET_CHUNK
    decay = decay_ref[...]
    xi = xi_ref[...]
    zeta = zeta_ref[...]
    cd = cd_ref[0:1, :]
    ng = ng_ref[...]

    def group(gi, carry):
        chunks = range(RET_GROUP)
        sls = [pl.ds(pl.multiple_of((gi * RET_GROUP + u) * c, c), c) for u in chunks]
        qs = [q_ref[sl, :].astype(BF16) for sl in sls]
        ks = [k_ref[sl, :] for sl in sls]
        vs = [v_ref[sl, :].astype(BF16) for sl in sls]
        scores = [(_dot_nt(q, k.astype(BF16)) * decay).astype(BF16) for q, k in zip(qs, ks)]
        inner = [_dot(s, v) for s, v in zip(scores, vs)]
        kvs = [_dot((k * zeta).T.astype(BF16), v) for k, v in zip(ks, vs)]
        state = state_ref[...]
        outs = []
        for u in chunks:
            outs.append(inner[u] + _dot(qs[u], state.astype(BF16)) * xi)
            state = kvs[u] + cd * state
        state_ref[...] = state
        for u in chunks:
            o = outs[u]
            mu = jnp.mean(o, axis=-1, keepdims=True)
            oc = o - mu
            var = jnp.mean(oc * oc, axis=-1, keepdims=True)
            y = oc * lax.rsqrt(var + EPS) * ng
            o_ref[sls[u], :] = (y * g_ref[sls[u], :]).astype(o_ref.dtype)
        return carry

    assert (q_ref.shape[0] // c) % RET_GROUP == 0
    lax.fori_loop(0, q_ref.shape[0] // c // RET_GROUP, group, 0)


def _retention(proj, consts, norm_g, *, batch, seq, tm):
    nt = seq // tm
    hh = RET_HEADS

    def col(group):
        return pl.BlockSpec((tm, HEAD_DIM), lambda b, h, i: (b * nt + i, group * hh + h))

    def per_head(rows):
        return pl.BlockSpec((None, rows, HEAD_DIM), lambda b, h, i: (h, 0, 0))

    return pl.pallas_call(
        _retention_kernel,
        grid=(batch, hh, nt),
        in_specs=[col(0), col(1), col(2), col(3),
                  per_head(RET_CHUNK), per_head(RET_CHUNK), per_head(RET_CHUNK), per_head(8), per_head(1)],
        out_specs=pl.BlockSpec((tm, HEAD_DIM), lambda b, h, i: (b * nt + i, h)),
        out_shape=jax.ShapeDtypeStruct((batch * seq, EV_GROUP), BF16),
        scratch_shapes=[pltpu.VMEM((HEAD_DIM, HEAD_DIM), F32)],
        compiler_params=_params("parallel", "parallel", "arbitrary"),
        name="retention",
    )(proj, proj, proj, proj, consts["decay"], consts["xi"], consts["zeta"], consts["chunk_decay"],
      norm_g.astype(F32).reshape(hh, 1, HEAD_DIM))


HG_LEVELS = (64, 32, 16, 8)


HG_GROUP = 8


def _hgrn_consts():
    c = HG_CHUNK
    t = np.arange(c)[:, None]
    s = np.arange(c)[None, :]
    tril = (s <= t).astype(np.float32)
    masks = []
    for lvl, bs in enumerate(HG_LEVELS):
        same = (t // bs) == (s // bs)
        if lvl < len(HG_LEVELS) - 1:
            masks.append(same & (t % bs >= bs // 2) & (s % bs < bs // 2))
        else:
            masks.append(same & (s <= t))
    return jnp.asarray(tril, BF16), jnp.asarray(np.stack(masks).astype(np.float32))


def _split3(x):
    hi = x.astype(BF16)
    rest = x - hi.astype(F32)
    mid = rest.astype(BF16)
    lo = (rest - mid.astype(F32)).astype(BF16)
    return hi, mid, lo


def _block_reference_rows(cum, bs, row):
    pieces = [jnp.broadcast_to(cum[b0 + row:b0 + row + 1, :], (bs, cum.shape[1]))
              for b0 in range(0, cum.shape[0], bs)]
    return pieces[0] if len(pieces) == 1 else jnp.concatenate(pieces, axis=0)


def _hgrn_kernel(q_ref, f_ref, i_ref, g_ref, lbl_ref, tril_ref, mask_ref, ng_ref, o_ref, state_ref, *,
                 layer):
    @pl.when(pl.program_id(2) == 0)
    def _():
        state_ref[...] = jnp.zeros_like(state_ref)

    c = HG_CHUNK
    logits = lbl_ref[...]
    e = jnp.exp(logits - jnp.max(logits, axis=0, keepdims=True))
    lb = jnp.sum(e[:layer + 1, :], axis=0, keepdims=True) / jnp.sum(e, axis=0, keepdims=True)
    tril = tril_ref[...]
    masks = [mask_ref[l] > 0.5 for l in range(len(HG_LEVELS))]
    ng = ng_ref[...]

    def group(gi, carry):
        chunks = range(HG_GROUP)
        sls = [pl.ds(pl.multiple_of((gi * HG_GROUP + u) * c, c), c) for u in chunks]
        fs = [lb + (1.0 - lb) * jax.nn.sigmoid(f_ref[sl, :]) for sl in sls]
        keys = [1.0 - f for f in fs]
        splits = [_split3(jnp.log(jnp.maximum(f, F_FLOOR))) for f in fs]
        cums = [_dot(tril, hi) + (_dot(tril, mid) + _dot(tril, lo)) for hi, mid, lo in splits]
        qs = [q_ref[sl, :] for sl in sls]
        scores = [jnp.zeros((c, c), F32) for _ in chunks]
        for lvl, bs in enumerate(HG_LEVELS):
            for u in chunks:
                a = cums[u] - _block_reference_rows(cums[u], bs, bs // 2 - 1)
                if lvl < len(HG_LEVELS) - 1:
                    decay = jnp.exp(-jnp.abs(a))
                    qa = qs[u] * decay
                    ka = keys[u] * decay
                else:
                    qa = qs[u] * jnp.exp(a)
                    ka = keys[u] * jnp.exp(-a)
                scores[u] = jnp.where(masks[lvl], _dot_nt(qa.astype(BF16), ka.astype(BF16)), scores[u])
        vs = [i_ref[sl, :] for sl in sls]
        lasts = [cum[c - 1:c, :] for cum in cums]
        intra = [_dot(scores[u].astype(BF16), vs[u].astype(BF16)) for u in chunks]
        kvs = [_dot(vs[u].T.astype(BF16), (keys[u] * jnp.exp(lasts[u] - cums[u])).astype(BF16)) for u in chunks]
        qcs = [(qs[u] * jnp.exp(cums[u])).astype(BF16) for u in chunks]
        state_t = state_ref[...]
        outs = []
        for u in chunks:
            outs.append(intra[u] + _dot_nt(qcs[u], state_t.astype(BF16)))
            state_t = jnp.exp(lasts[u]) * state_t + kvs[u]
        state_ref[...] = state_t
        for u in chunks:
            out = outs[u]
            y = out * lax.rsqrt(jnp.mean(out * out, axis=-1, keepdims=True) + EPS) * ng
            o_ref[sls[u], :] = (y * g_ref[sls[u], :]).astype(o_ref.dtype)
        return carry

    assert (q_ref.shape[0] // c) % HG_GROUP == 0
    lax.fori_loop(0, q_ref.shape[0] // c // HG_GROUP, group, 0)


def _hgrn(proj, lb_logits, norm_g, *, layer, batch, seq, tm):
    cum_w, masks = _hgrn_consts()
    nt = seq // tm
    hh = HG_HEADS
    slots = lb_logits.shape[0]
    lbl = lb_logits.astype(F32).reshape(slots, hh, HEAD_DIM).transpose(1, 0, 2)

    def col(group):
        return pl.BlockSpec((tm, HEAD_DIM), lambda b, h, i: (b * nt + i, group * hh + h))

    return pl.pallas_call(
        functools.partial(_hgrn_kernel, layer=layer),
        grid=(batch, hh, nt),
        in_specs=[col(4), col(5), col(6), col(7),
                  pl.BlockSpec((None, slots, HEAD_DIM), lambda b, h, i: (h, 0, 0)),
                  pl.BlockSpec(cum_w.shape, lambda b, h, i: (0, 0)),
                  pl.BlockSpec(masks.shape, lambda b, h, i: (0, 0, 0)),
                  pl.BlockSpec((None, 1, HEAD_DIM), lambda b, h, i: (h, 0, 0))],
        out_specs=pl.BlockSpec((tm, HEAD_DIM), lambda b, h, i: (b * nt + i, h)),
        out_shape=jax.ShapeDtypeStruct((batch * seq, EV_GROUP), BF16),
        scratch_shapes=[pltpu.VMEM((HEAD_DIM, HEAD_DIM), F32)],
        compiler_params=_params("parallel", "parallel", "arbitrary"),
        name="hgrn2",
    )(proj, proj, proj, proj, lbl, cum_w, masks, norm_g.astype(F32).reshape(hh, 1, HEAD_DIM))


def _t5_bucket(dist):
    exact = REL_BUCKETS // 2
    d = np.maximum(dist, 0)
    log_ratio = (np.log(np.maximum(d, 1).astype(np.float32) / np.float32(exact))
                 / np.float32(math.log(REL_MAX_DIST / exact)))
    large = np.minimum(exact + (log_ratio * np.float32(REL_BUCKETS - exact)).astype(np.int32), REL_BUCKETS - 1)
    return np.where(d < exact, d, large)


def _dsa_row_order(dil):
    rho = np.arange(DSA_BLOCK)
    run = PERM_RUN * dil
    c, j, a = rho // run, (rho % run) // PERM_RUN, rho % PERM_RUN
    return (PERM_BLOCK // dil) * j + (PERM_RESIDUES // dil) * a + c


def _dsa_bucket_table():
    blk = DSA_BLOCK
    tabs = []
    for window, dil in DSA_BRANCHES:
        order = _dsa_row_order(dil)
        qi = order[:, None]
        ki = np.concatenate([order, blk + order])[None, :]
        delta = qi + blk - ki
        n_back = window // dil
        valid = (delta >= 0) & (delta <= n_back)
        tabs.append(np.where(valid, _t5_bucket(delta * dil), -1).astype(np.int32))
    return jnp.asarray(np.stack(tabs))


def _dsa_kernel(rb_ref, q_ref, k_ref, v_ref, bucket_ref, o_ref,
                qn_ref, kp_ref, vp_ref, m_ref, den_ref, num_ref, bias_ref, lg_ref, *, seq, pad_blocks):
    blk = DSA_BLOCK
    nj = seq // PERM_BLOCK
    tiled = (nj, PERM_RESIDUES, PERM_RUN, HEAD_DIM)
    head = pl.program_id(0)
    qn_ref[...] = q_ref[...].reshape(tiled)
    kp_ref[pad_blocks:pad_blocks + nj] = k_ref[...].reshape(tiled)
    vp_ref[pad_blocks:pad_blocks + nj] = v_ref[...].reshape(tiled)

    @pl.when(pl.program_id(1) == 0)
    def _():
        zeros = jnp.zeros((pad_blocks,) + tiled[1:], F32)
        kp_ref[0:pad_blocks] = zeros
        vp_ref[0:pad_blocks] = zeros
        first_block_keys = lax.broadcasted_iota(jnp.int32, (blk, 2 * blk), 1) >= blk
        for br in range(len(DSA_BRANCHES)):
            bucket = bucket_ref[br]
            bias = jnp.full(bucket.shape, NEG_INF, F32)
            for b in range(REL_BUCKETS):
                bias = jnp.where(bucket == b, rb_ref[b, head], bias)
            bias_ref[2 * br] = bias
            bias_ref[2 * br + 1] = jnp.where(first_block_keys, bias, NEG_INF)

    scale = HEAD_DIM ** -0.5

    def gather(ref, j0, r, dil, spans):
        if dil == 1:
            return ref[pl.ds(j0, spans)].reshape(spans * blk, HEAD_DIM)
        pieces = [ref[pl.ds(j0 + s * dil, dil), r + dil * c]
                  for s in range(spans) for c in range(PERM_RESIDUES // dil)]
        return jnp.concatenate(pieces, axis=0).reshape(spans * blk, HEAD_DIM)

    def scatter(ref, j0, r, dil, val):
        if dil == 1:
            ref[j0] = val.reshape(tiled[1:])
            return
        run = PERM_RUN * dil
        for c in range(PERM_RESIDUES // dil):
            ref[pl.ds(j0, dil), r + dil * c] = val[c * run:(c + 1) * run, :].reshape(dil, PERM_RUN, HEAD_DIM)

    trips = nj // DSA_UNROLL
    assert nj % DSA_UNROLL == 0 and all(dil <= pad_blocks for _, dil in DSA_BRANCHES)

    def block_coords(br, it, u):
        dil = DSA_BRANCHES[br][1]
        nb = nj // dil
        idx = it * DSA_UNROLL + u
        return idx // nb, idx % nb, dil

    def logits_stage(br, it, slot):
        for u in range(DSA_UNROLL):
            r, n, dil = block_coords(br, it, u)
            qb = gather(qn_ref, n * dil, r, dil, 1)
            kb = gather(kp_ref, pad_blocks + (n - 1) * dil, r, dil, 2)
            bias = bias_ref[2 * br + jnp.where(n == 0, 1, 0)]
            lg_ref[slot, u] = _dot_nt(qb.astype(BF16), kb.astype(BF16)) * scale + bias

    def softmax_stage(br, it, slot):
        blocks = range(DSA_UNROLL)
        coords = [block_coords(br, it, u) for u in blocks]
        logits = [lg_ref[slot, u] for u in blocks]
        ms = [jnp.max(lg, axis=-1, keepdims=True) for lg in logits]
        ps = [jnp.exp(lg - m) for lg, m in zip(logits, ms)]
        dens = [jnp.sum(p, axis=-1, keepdims=True) for p in ps]
        nums = [_dot(p.astype(BF16), gather(vp_ref, pad_blocks + (n - 1) * dil, r, dil, 2).astype(BF16))
                for p, (r, n, dil) in zip(ps, coords)]
        for (r, n, dil), m, den, num in zip(coords, ms, dens, nums):
            scatter(m_ref.at[br], n * dil, r, dil, jnp.broadcast_to(m, (blk, HEAD_DIM)))
            scatter(den_ref.at[br], n * dil, r, dil, jnp.broadcast_to(den, (blk, HEAD_DIM)))
            scatter(num_ref.at[br], n * dil, r, dil, num)

    n_br = len(DSA_BRANCHES)
    logits_stage(0, 0, 0)
    for br in range(n_br):
        first_slot = (br * trips) % 2

        def overlapped(it, carry, br=br, first_slot=first_slot):
            slot = (first_slot + it) % 2
            softmax_stage(br, it, slot)
            logits_stage(br, it + 1, 1 - slot)
            return carry

        lax.fori_loop(0, trips - 1, overlapped, 0)
        last_slot = (first_slot + trips - 1) % 2
        softmax_stage(br, trips - 1, last_slot)
        if br + 1 < n_br:
            logits_stage(br + 1, 0, 1 - last_slot)

    def merge(j, carry):
        n_br = len(DSA_BRANCHES)
        ms = [m_ref[br, j] for br in range(n_br)]
        m_all = functools.reduce(jnp.maximum, ms)
        ws = [jnp.exp(mi - m_all) for mi in ms]
        den = functools.reduce(lambda x, y: x + y, [ws[br] * den_ref[br, j] for br in range(n_br)])
        num = functools.reduce(lambda x, y: x + y, [ws[br] * num_ref[br, j] for br in range(n_br)])
        rows = pl.ds(pl.multiple_of(j * blk, blk), blk)
        o_ref[rows, :] = (num / den).reshape(blk, HEAD_DIM).astype(o_ref.dtype)
        return carry

    lax.fori_loop(0, nj, merge, 0, unroll=2)


def _dilated_attention(qkv, rel_bias, *, batch, seq):
    hh = DSA_HEADS
    nj = seq // PERM_BLOCK
    pad_blocks = max(dil for _, dil in DSA_BRANCHES)
    bucket = _dsa_bucket_table()
    tiled = (PERM_RESIDUES, PERM_RUN, HEAD_DIM)

    def col(group):
        return pl.BlockSpec((seq, HEAD_DIM), lambda h, b: (b, group * hh + h))

    return pl.pallas_call(
        functools.partial(_dsa_kernel, seq=seq, pad_blocks=pad_blocks),
        grid=(hh, batch),
        in_specs=[pl.BlockSpec(memory_space=pltpu.SMEM),
                  col(0), col(1), col(2),
                  pl.BlockSpec(bucket.shape, lambda h, b: (0, 0, 0))],
        out_specs=pl.BlockSpec((seq, HEAD_DIM), lambda h, b: (b, h)),
        out_shape=jax.ShapeDtypeStruct((batch * seq, hh * HEAD_DIM), BF16),
        scratch_shapes=[pltpu.VMEM((nj,) + tiled, F32),
                        pltpu.VMEM((pad_blocks + nj,) + tiled, F32),
                        pltpu.VMEM((pad_blocks + nj,) + tiled, F32),
                        pltpu.VMEM((len(DSA_BRANCHES), nj) + tiled, F32),
                        pltpu.VMEM((len(DSA_BRANCHES), nj) + tiled, F32),
                        pltpu.VMEM((len(DSA_BRANCHES), nj) + tiled, F32),
                        pltpu.VMEM((2 * len(DSA_BRANCHES), DSA_BLOCK, 2 * DSA_BLOCK), F32),
                        pltpu.VMEM((2, DSA_UNROLL, DSA_BLOCK, 2 * DSA_BLOCK), F32)],
        compiler_params=_params("parallel", "arbitrary"),
        name="dilated_attention",
    )(rel_bias.astype(F32), qkv, qkv, qkv, bucket)


def _mem_kv_kernel(mem_ref, g_ref, w_ref, kg_ref, k_ref, v_ref):
    h = _rms_rows(mem_ref[...], g_ref[...]).astype(BF16)
    d = XA_HEADS * XA_HD
    kg = kg_ref[...]
    for hd in range(XA_HEADS):
        sl = slice(hd * XA_HD, (hd + 1) * XA_HD)
        k_ref[:, sl] = _rms_rows(_dot(h, w_ref[:, sl]), kg).astype(k_ref.dtype)
    v_ref[...] = _dot(h, w_ref[:, d:2 * d]).astype(v_ref.dtype)


def _mem_kv(mem2d, g, w_kv, layer, k_norm_g, *, batch):
    d = XA_HEADS * XA_HD
    return pl.pallas_call(
        _mem_kv_kernel,
        grid=(batch,),
        in_specs=[pl.BlockSpec((N_MEM, D_MODEL), lambda b: (b, 0)),
                  pl.BlockSpec((1, D_MODEL), lambda b: (0, 0)),
                  _layer_weight(w_kv, layer),
                  pl.BlockSpec((1, XA_HD), lambda b: (0, 0))],
        out_specs=[pl.BlockSpec((N_MEM, d), lambda b: (b, 0)),
                   pl.BlockSpec((N_MEM, d), lambda b: (b, 0))],
        out_shape=[jax.ShapeDtypeStruct((batch * N_MEM, d), BF16),
                   jax.ShapeDtypeStruct((batch * N_MEM, d), BF16)],
        compiler_params=_params("parallel"),
        name="mem_kv",
    )(mem2d, g.astype(F32).reshape(1, D_MODEL), w_kv, k_norm_g.astype(F32).reshape(1, XA_HD))


def _xattn_kernel(*refs, n_y, unpermute, oc):
    xin_ref = refs[0]
    y_refs = refs[1:1 + n_y]
    wmix_ref = refs[1 + n_y]
    rest = refs[2 + n_y:]
    if unpermute:
        p_ref, rest = rest[0], rest[1:]
    g_ref, wq_ref, qg_ref, mk_ref, mv_ref, wo_ref, o_ref, x_ref, att_ref = rest
    d = o_ref.shape[1]
    ys = [y_ref[...] for y_ref in y_refs]
    if unpermute:
        ys = [_permute_rows(p_ref[...], y) for y in ys]
    for n in range(d // oc):
        sl = slice(n * oc, (n + 1) * oc)
        acc = xin_ref[:, sl]
        off = 0
        for y in ys:
            acc = acc + _dot(y, wmix_ref[off:off + y.shape[1], sl])
            off += y.shape[1]
        x_ref[:, sl] = acc

    h = _rms_rows(x_ref[...], g_ref[...]).astype(BF16)
    qg = qg_ref[...]
    sls = [slice(hd * XA_HD, (hd + 1) * XA_HD) for hd in range(XA_HEADS)]
    qs = [_dot(h, wq_ref[:, sl]) for sl in sls]
    qs = [_rms_rows(q, qg).astype(BF16) for q in qs]
    logits = [_dot_nt(q, mk_ref[:, sl]) * (XA_HD ** -0.5) for q, sl in zip(qs, sls)]
    ps = [jnp.exp(lg - jnp.max(lg, axis=-1, keepdims=True)) for lg in logits]
    ps = [(p / jnp.sum(p, axis=-1, keepdims=True)).astype(BF16) for p in ps]
    for p, sl in zip(ps, sls):
        att_ref[:, sl] = _dot(p, mv_ref[:, sl]).astype(att_ref.dtype)
    for n in range(d // oc):
        sl = slice(n * oc, (n + 1) * oc)
        o_ref[:, sl] = x_ref[:, sl] + _dot(att_ref[...], wo_ref[:, sl])


def _mix_out_xattn(x2d, ys, w_mix, mix_layer, g, w_q, q_norm_g, mk, mv, w_o, layer, *, seq, tm,
                   unpermute=False, oc=256):
    t, d = x2d.shape
    nt = seq // tm
    n_y = len(ys)
    assert sum(y.shape[1] for y in ys) == w_mix.shape[1]
    const = lambda shape: pl.BlockSpec(shape, lambda i: (0, 0))
    in_specs = [pl.BlockSpec((tm, d), lambda i: (i, 0))]
    in_specs += [pl.BlockSpec((tm, y.shape[1]), lambda i: (i, 0)) for y in ys]
    in_specs.append(_layer_weight(w_mix, mix_layer))
    args = [x2d, *ys, w_mix]
    if unpermute:
        in_specs.append(const((PERM_BLOCK, PERM_BLOCK)))
        args.append(jnp.asarray(_block_permutation().T, BF16))
    in_specs += [const((1, d)), _layer_weight(w_q, layer), const((1, XA_HD)),
                 pl.BlockSpec((N_MEM, d), lambda i: (i // nt, 0)),
                 pl.BlockSpec((N_MEM, d), lambda i: (i // nt, 0)),
                 _layer_weight(w_o, layer)]
    args += [g.astype(F32).reshape(1, d), w_q, q_norm_g.astype(F32).reshape(1, XA_HD), mk, mv, w_o]
    return pl.pallas_call(
        functools.partial(_xattn_kernel, n_y=n_y, unpermute=unpermute, oc=oc),
        grid=(t // tm,),
        in_specs=in_specs,
        out_specs=pl.BlockSpec((tm, d), lambda i: (i, 0)),
        out_shape=jax.ShapeDtypeStruct((t, d), F32),
        scratch_shapes=[pltpu.VMEM((tm, d), F32), pltpu.VMEM((tm, d), BF16)],
        compiler_params=_params("parallel"),
        name="mix_out_xattn",
    )(*args)


def _ffn_kernel(x_ref, g_ref, win_ref, cw_ref, cb_ref, wout_ref, o_ref, carry_ref, act_ref, *, nt, fc, oc):
    tm, d = x_ref.shape
    assert CONV_W == 3

    @pl.when(pl.program_id(0) % nt == 0)
    def _():
        carry_ref[...] = jnp.zeros_like(carry_ref)

    h = _rms_rows(x_ref[...], g_ref[...]).astype(BF16)
    row = lax.broadcasted_iota(jnp.int32, (8, fc), 0)
    for c in range(D_FF // fc):
        sl = slice(c * fc, (c + 1) * fc)
        gate = _dot(h, win_ref[:, sl])
        up = _dot(h, win_ref[:, D_FF + c * fc:D_FF + (c + 1) * fc])
        prev = carry_ref[c]
        carry_ref[c] = gate[tm - 8:tm, :]
        w0, w1, w2, bias = cw_ref[0:1, sl], cw_ref[1:2, sl], cw_ref[2:3, sl], cb_ref[:, sl]
        conv = pltpu.roll(gate, 2, axis=0) * w0 + pltpu.roll(gate, 1, axis=0) * w1 + gate * w2 + bias
        top = gate[0:8, :]
        t1 = jnp.where(row == 0, prev[7:8, :], pltpu.roll(top, 1, axis=0))
        t2 = jnp.where(row == 0, prev[6:7, :], jnp.where(row == 1, prev[7:8, :], pltpu.roll(top, 2, axis=0)))
        conv = jnp.concatenate([t2 * w0 + t1 * w1 + top * w2 + bias, conv[8:, :]], axis=0)
        act_ref[:, sl] = (jax.nn.gelu(conv) * up).astype(BF16)
    for n in range(d // oc):
        sl = slice(n * oc, (n + 1) * oc)
        o_ref[:, sl] = x_ref[:, sl] + _dot(act_ref[...], wout_ref[:, sl])


def _ffn(x2d, g, w_in, conv_w, conv_b, w_out, layer, *, seq, tm, fc=256, oc=256):
    t, d = x2d.shape
    nt = seq // tm
    const = lambda shape: pl.BlockSpec(shape, lambda i: (0, 0))
    return pl.pallas_call(
        functools.partial(_ffn_kernel, nt=nt, fc=fc, oc=oc),
        grid=(t // tm,),
        in_specs=[pl.BlockSpec((tm, d), lambda i: (i, 0)),
                  const((1, d)), _layer_weight(w_in, layer), const((CONV_W, D_FF)), const((1, D_FF)),
                  _layer_weight(w_out, layer)],
        out_specs=pl.BlockSpec((tm, d), lambda i: (i, 0)),
        out_shape=jax.ShapeDtypeStruct((t, d), F32),
        scratch_shapes=[pltpu.VMEM((D_FF // fc, 8, fc), F32), pltpu.VMEM((tm, D_FF), BF16)],
        compiler_params=_params("arbitrary"),
        name="conv_ffn",
    )(x2d, g.astype(F32).reshape(1, d), w_in, conv_w.astype(F32), conv_b.astype(F32).reshape(1, D_FF), w_out)


def kernel(x, mem, mix_norm_g, ev_w_in, ev_ret_norm_g, ev_hg_norm_g, hg_lb_logits, ev_w_out, od_w_in, od_q_norm_g, od_k_norm_g, rel_bias, od_w_out, xa_norm_g, xa_mem_norm_g, xa_w_q, xa_w_kv, xa_q_norm_g, xa_k_norm_g, xa_w_o, ffn_norm_g, ffn_w_in, ffn_conv_w, ffn_conv_b, ffn_w_out):
    batch, seq, d = x.shape
    depth = mix_norm_g.shape[0]
    x2d = x.reshape(batch * seq, d)
    mem2d = mem.reshape(batch * mem.shape[1], d)
    tm = 512
    ev_w_in, ev_w_out, od_w_in, od_w_out, xa_w_q, xa_w_kv, xa_w_o, ffn_w_in, ffn_w_out = (
        w.astype(BF16) for w in (ev_w_in, ev_w_out, od_w_in, od_w_out, xa_w_q, xa_w_kv, xa_w_o, ffn_w_in, ffn_w_out))
    for l in range(depth):
        if l % 2 == 0:
            e = l // 2
            consts = _retention_consts(seq)
            proj = _rms_proj(x2d, mix_norm_g[l], ev_w_in, e, tm=tm,
                             rotary=(consts["cos"], consts["sin"], 2 * EV_GROUP), silu_chunks=(3, 7))
            y_ret = _retention(proj, consts, ev_ret_norm_g[e], batch=batch, seq=seq, tm=2 * tm)
            y_hg = _hgrn(proj, hg_lb_logits, ev_hg_norm_g[e], layer=l, batch=batch, seq=seq, tm=2 * tm)
            ys, w_mix, mix_layer, permuted = [y_ret, y_hg], ev_w_out, e, False
        else:
            o = l // 2
            qk_gains = jnp.concatenate([jnp.tile(od_q_norm_g[o], DSA_HEADS), jnp.tile(od_k_norm_g[o], DSA_HEADS)])
            qkv = _rms_proj(x2d, mix_norm_g[l], od_w_in, o, tm=tm, permute=True, head_gains=qk_gains)
            att = _dilated_attention(qkv, rel_bias, batch=batch, seq=seq)
            ys, w_mix, mix_layer, permuted = [att], od_w_out, o, True
        mk, mv = _mem_kv(mem2d, xa_mem_norm_g[l], xa_w_kv, l, xa_k_norm_g[l], batch=batch)
        x2d = _mix_out_xattn(x2d, ys, w_mix, mix_layer, xa_norm_g[l], xa_w_q, xa_q_norm_g[l], mk, mv, xa_w_o, l,
                             seq=seq, tm=tm, unpermute=permuted)
        x2d = _ffn(x2d, ffn_norm_g[l], ffn_w_in, ffn_conv_w[l], ffn_conv_b[l], ffn_w_out, l, seq=seq, tm=tm)
    return x2d.reshape(batch, seq, d)
```

```python
import functools
import math

import jax
import jax.numpy as jnp
import numpy as np
from jax import lax
from jax.experimental import pallas as pl
from jax.experimental.pallas import tpu as pltpu

F32 = jnp.float32
BF16 = jnp.bfloat16

D_MODEL = 1024
N_MEM = 256
EPS = 1e-6
HEAD_DIM = 128
RET_HEADS = 4
RET_CHUNK = 128
RET_GROUP = 8
ROPE_BASE = 10000.0
HG_HEADS = 4
HG_CHUNK = 64
F_FLOOR = 1e-6
DSA_HEADS = 8
DSA_BRANCHES = ((128, 1), (512, 4), (2048, 16))
DSA_BLOCK = 128
DSA_UNROLL = 8
REL_BUCKETS = 32
REL_MAX_DIST = 2048
XA_HEADS = 4
XA_HD = 256
D_FF = 2816
CONV_W = 3
EV_GROUP = RET_HEADS * HEAD_DIM

VMEM_LIMIT_BYTES = 56 * 1024 * 1024
NEG_INF = float("-inf")

_NT = (((1,), (1,)), ((), ()))


def _params(*sem):
    return pltpu.CompilerParams(dimension_semantics=sem, vmem_limit_bytes=VMEM_LIMIT_BYTES)


def _rms_rows(x, g):
    return x * lax.rsqrt(jnp.mean(x * x, axis=-1, keepdims=True) + EPS) * g


def _dot(a, b):
    return jnp.dot(a, b, preferred_element_type=F32)


def _dot_nt(a, b):
    return lax.dot_general(a, b, _NT, preferred_element_type=F32)


PERM_BLOCK = 128
PERM_RESIDUES = 16
PERM_RUN = PERM_BLOCK // PERM_RESIDUES


def _block_permutation():
    rho = np.arange(PERM_BLOCK)
    src = PERM_RESIDUES * (rho % PERM_RUN) + rho // PERM_RUN
    p = np.zeros((PERM_BLOCK, PERM_BLOCK), np.float32)
    p[rho, src] = 1.0
    return p


def _permute_rows(p, x):
    blocks = [_dot(p, x[b * PERM_BLOCK:(b + 1) * PERM_BLOCK, :]).astype(BF16)
              for b in range(x.shape[0] // PERM_BLOCK)]
    return jnp.concatenate(blocks, axis=0)


def _rms_proj_kernel(*refs, col_chunk, permute, n_head_norm, n_rotary):
    x_ref, g_ref, w_ref = refs[:3]
    o_ref = refs[-1]
    extras = list(refs[3:-1])
    p_ref = extras.pop(0) if permute else None
    hg_ref = extras.pop(0) if n_head_norm else None
    cos_ref, sin_ref = (extras.pop(0), extras.pop(0)) if n_rotary else (None, None)
    h = _rms_rows(x_ref[...], g_ref[...]).astype(BF16)
    if permute:
        h = _permute_rows(p_ref[...], h)
    n = o_ref.shape[1]
    for c in range(n // col_chunk):
        sl = slice(c * col_chunk, (c + 1) * col_chunk)
        y = _dot(h, w_ref[:, sl])
        if c * col_chunk < n_head_norm:
            assert (c + 1) * col_chunk <= n_head_norm
            heads = [_rms_rows(y[:, k * HEAD_DIM:(k + 1) * HEAD_DIM],
                               hg_ref[:, c * col_chunk + k * HEAD_DIM:c * col_chunk + (k + 1) * HEAD_DIM])
                     for k in range(col_chunk // HEAD_DIM)]
            y = jnp.concatenate(heads, axis=1)
        if c * col_chunk < n_rotary:
            assert (c + 1) * col_chunk <= n_rotary and (n_rotary // 2) % col_chunk == 0
            cos, sin = cos_ref[...], sin_ref[...]
            heads = []
            for k in range(col_chunk // HEAD_DIM):
                yh = y[:, k * HEAD_DIM:(k + 1) * HEAD_DIM]
                yh = yh * cos + pltpu.roll(yh, HEAD_DIM // 2, axis=1) * sin
                heads.append(yh * (HEAD_DIM ** -0.5) if c * col_chunk >= n_rotary // 2 else yh)
            y = jnp.concatenate(heads, axis=1)
        o_ref[:, sl] = y.astype(o_ref.dtype)


def _layer_weight(w, layer):
    return pl.BlockSpec((None,) + w.shape[1:], lambda i: (layer, 0, 0))


def _rms_proj(x2d, g, w, layer, *, tm, out_dtype=F32, col_chunk=512, permute=False, head_gains=None,
              rotary=None):
    t, d = x2d.shape
    n = w.shape[2]
    in_specs = [
        pl.BlockSpec((tm, d), lambda i: (i, 0)),
        pl.BlockSpec((1, d), lambda i: (0, 0)),
        _layer_weight(w, layer),
    ]
    args = [x2d, g.reshape(1, d).astype(F32), w]
    if permute:
        in_specs.append(pl.BlockSpec((PERM_BLOCK, PERM_BLOCK), lambda i: (0, 0)))
        args.append(jnp.asarray(_block_permutation(), BF16))
    n_head_norm = 0 if head_gains is None else head_gains.shape[0]
    if n_head_norm:
        in_specs.append(pl.BlockSpec((1, n_head_norm), lambda i: (0, 0)))
        args.append(head_gains.astype(F32).reshape(1, n_head_norm))
    n_rotary = 0
    if rotary is not None:
        cos, sin, n_rotary = rotary
        nt = cos.shape[0] // tm
        in_specs += [pl.BlockSpec((tm, HEAD_DIM), lambda i: (i % nt, 0))] * 2
        args += [cos, sin]
    return pl.pallas_call(
        functools.partial(_rms_proj_kernel, col_chunk=col_chunk, permute=permute, n_head_norm=n_head_norm,
                          n_rotary=n_rotary),
        grid=(t // tm,),
        in_specs=in_specs,
        out_specs=pl.BlockSpec((tm, n), lambda i: (i, 0)),
        out_shape=jax.ShapeDtypeStruct((t, n), out_dtype),
        compiler_params=_params("parallel"),
        name="rms_proj",
    )(*args)


def _retention_consts(seq):
    h = jnp.arange(RET_HEADS, dtype=F32)
    log_gamma = jnp.log(1.0 - jnp.exp2(-5.0 - h))
    c = RET_CHUNK
    idx = jnp.arange(c, dtype=F32)
    diff = idx[:, None] - idx[None, :]
    decay = jnp.where(diff >= 0, jnp.exp(log_gamma[:, None, None] * jnp.maximum(diff, 0.0)), 0.0)
    xi = jnp.exp(log_gamma[:, None] * (idx + 1.0))
    zeta = jnp.exp(log_gamma[:, None] * (c - 1.0 - idx))
    chunk_decay = jnp.exp(log_gamma * c)
    ones = jnp.ones((RET_HEADS, c, HEAD_DIM), F32)
    half = HEAD_DIM // 2
    inv = 1.0 / (ROPE_BASE ** (jnp.arange(half, dtype=F32) / half))
    ang = jnp.arange(seq, dtype=F32)[:, None] * inv[None, :]
    cos, sin = jnp.cos(ang), jnp.sin(ang)
    return dict(
        decay=decay,
        xi=xi[:, :, None] * ones,
        zeta=zeta[:, :, None] * ones,
        chunk_decay=chunk_decay[:, None, None] * jnp.ones((RET_HEADS, 8, HEAD_DIM), F32),
        cos=jnp.concatenate([cos, cos], axis=-1),
        sin=jnp.concatenate([-sin, sin], axis=-1),
    )


def _retention_kernel(q_ref, k_ref, v_ref, g_ref, decay_ref, xi_ref, zeta_ref, cd_ref, ng_ref, o_ref, state_ref):
    @pl.when(pl.program_id(2) == 0)
    def _():
        state_ref[...] = jnp.zeros_like(state_ref)

    c = RET_CHUNK
    decay = decay_ref[...]
    xi = xi_ref[...]
    zeta = zeta_ref[...]
    cd = cd_ref[0:1, :]
    ng = ng_ref[...]

    def group(gi, carry):
        chunks = range(RET_GROUP)
        sls = [pl.ds(pl.multiple_of((gi * RET_GROUP + u) * c, c), c) for u in chunks]
        qs = [q_ref[sl, :].astype(BF16) for sl in sls]
        ks = [k_ref[sl, :] for sl in sls]
        vs = [v_ref[sl, :].astype(BF16) for sl in sls]
        scores = [(_dot_nt(q, k.astype(BF16)) * decay).astype(BF16) for q, k in zip(qs, ks)]
        inner = [_dot(s, v) for s, v in zip(scores, vs)]
        kvs = [_dot((k * zeta).T.astype(BF16), v) for k, v in zip(ks, vs)]
        state = state_ref[...]
        outs = []
        for u in chunks:
            outs.append(inner[u] + _dot(qs[u], state.astype(BF16)) * xi)
            state = kvs[u] + cd * state
        state_ref[...] = state
        for u in chunks:
            o = outs[u]
            mu = jnp.mean(o, axis=-1, keepdims=True)
            oc = o - mu
            var = jnp.mean(oc * oc, axis=-1, keepdims=True)
            y = oc * lax.rsqrt(var + EPS) * ng
            o_ref[sls[u], :] = (y * jax.nn.silu(g_ref[sls[u], :])).astype(o_ref.dtype)
        return carry

    assert (q_ref.shape[0] // c) % RET_GROUP == 0
    lax.fori_loop(0, q_ref.shape[0] // c // RET_GROUP, group, 0)


def _retention(proj, consts, norm_g, *, batch, seq, tm):
    nt = seq // tm
    hh = RET_HEADS

    def col(group):
        return pl.BlockSpec((tm, HEAD_DIM), lambda b, h, i: (b * nt + i, group * hh + h))

    def per_head(rows):
        return pl.BlockSpec((None, rows, HEAD_DIM), lambda b, h, i: (h, 0, 0))

    return pl.pallas_call(
        _retention_kernel,
        grid=(batch, hh, nt),
        in_specs=[col(0), col(1), col(2), col(3),
                  per_head(RET_CHUNK), per_head(RET_CHUNK), per_head(RET_CHUNK), per_head(8), per_head(1)],
        out_specs=pl.BlockSpec((tm, HEAD_DIM), lambda b, h, i: (b * nt + i, h)),
        out_shape=jax.ShapeDtypeStruct((batch * seq, EV_GROUP), BF16),
        scratch_shapes=[pltpu.VMEM((HEAD_DIM, HEAD_DIM), F32)],
        compiler_params=_params("parallel", "parallel", "arbitrary"),
        name="retention",
    )(proj, proj, proj, proj, consts["decay"], consts["xi"], consts["zeta"], consts["chunk_decay"],
      norm_g.astype(F32).reshape(hh, 1, HEAD_DIM))


HG_LEVELS = (64, 32, 16, 8)


HG_GROUP = 8


def _hgrn_consts():
    c = HG_CHUNK
    t = np.arange(c)[:, None]
    s = np.arange(c)[None, :]
    tril = (s <= t).astype(np.float32)
    masks = []
    for lvl, bs in enumerate(HG_LEVELS):
        same = (t // bs) == (s // bs)
        if lvl < len(HG_LEVELS) - 1:
            masks.append(same & (t % bs >= bs // 2) & (s % bs < bs // 2))
        else:
            masks.append(same & (s <= t))
    return jnp.asarray(tril, BF16), jnp.asarray(np.stack(masks).astype(np.float32))


def _split3(x):
    hi = x.astype(BF16)
    rest = x - hi.astype(F32)
    mid = rest.astype(BF16)
    lo = (rest - mid.astype(F32)).astype(BF16)
    return hi, mid, lo


def _block_reference_rows(cum, bs, row):
    pieces = [jnp.broadcast_to(cum[b0 + row:b0 + row + 1, :], (bs, cum.shape[1]))
              for b0 in range(0, cum.shape[0], bs)]
    return pieces[0] if len(pieces) == 1 else jnp.concatenate(pieces, axis=0)


def _hgrn_kernel(q_ref, f_ref, i_ref, g_ref, lbl_ref, tril_ref, mask_ref, ng_ref, o_ref, state_ref, *,
                 layer):
    @pl.when(pl.program_id(2) == 0)
    def _():
        state_ref[...] = jnp.zeros_like(state_ref)

    c = HG_CHUNK
    logits = lbl_ref[...]
    e = jnp.exp(logits - jnp.max(logits, axis=0, keepdims=True))
    lb = jnp.sum(e[:layer + 1, :], axis=0, keepdims=True) / jnp.sum(e, axis=0, keepdims=True)
    tril = tril_ref[...]
    masks = [mask_ref[l] > 0.5 for l in range(len(HG_LEVELS))]
    ng = ng_ref[...]

    def group(gi, carry):
        chunks = range(HG_GROUP)
        sls = [pl.ds(pl.multiple_of((gi * HG_GROUP + u) * c, c), c) for u in chunks]
        fs = [lb + (1.0 - lb) * jax.nn.sigmoid(f_ref[sl, :]) for sl in sls]
        keys = [1.0 - f for f in fs]
        splits = [_split3(jnp.log(jnp.maximum(f, F_FLOOR))) for f in fs]
        cums = [_dot(tril, hi) + (_dot(tril, mid) + _dot(tril, lo)) for hi, mid, lo in splits]
        qs = [q_ref[sl, :] for sl in sls]
        scores = [jnp.zeros((c, c), F32) for _ in chunks]
        for lvl, bs in enumerate(HG_LEVELS):
            for u in chunks:
                a = cums[u] - _block_reference_rows(cums[u], bs, bs // 2 - 1)
                if lvl < len(HG_LEVELS) - 1:
                    decay = jnp.exp(-jnp.abs(a))
                    qa = qs[u] * decay
                    ka = keys[u] * decay
                else:
                    qa = qs[u] * jnp.exp(a)
                    ka = keys[u] * jnp.exp(-a)
                scores[u] = jnp.where(masks[lvl], _dot_nt(qa.astype(BF16), ka.astype(BF16)), scores[u])
        vs = [i_ref[sl, :] for sl in sls]
        lasts = [cum[c - 1:c, :] for cum in cums]
        intra = [_dot(scores[u].astype(BF16), vs[u].astype(BF16)) for u in chunks]
        kvs = [_dot(vs[u].T.astype(BF16), (keys[u] * jnp.exp(lasts[u] - cums[u])).astype(BF16)) for u in chunks]
        qcs = [(qs[u] * jnp.exp(cums[u])).astype(BF16) for u in chunks]
        state_t = state_ref[...]
        outs = []
        for u in chunks:
            outs.append(intra[u] + _dot_nt(qcs[u], state_t.astype(BF16)))
            state_t = jnp.exp(lasts[u]) * state_t + kvs[u]
        state_ref[...] = state_t
        for u in chunks:
            out = outs[u]
            y = out * lax.rsqrt(jnp.mean(out * out, axis=-1, keepdims=True) + EPS) * ng
            o_ref[sls[u], :] = (y * jax.nn.silu(g_ref[sls[u], :])).astype(o_ref.dtype)
        return carry

    assert (q_ref.shape[0] // c) % HG_GROUP == 0
    lax.fori_loop(0, q_ref.shape[0] // c // HG_GROUP, group, 0)


def _hgrn(proj, lb_logits, norm_g, *, layer, batch, seq, tm):
    cum_w, masks = _hgrn_consts()
    nt = seq // tm
    hh = HG_HEADS
    slots = lb_logits.shape[0]
    lbl = lb_logits.astype(F32).reshape(slots, hh, HEAD_DIM).transpose(1, 0, 2)

    def col(group):
        return pl.BlockSpec((tm, HEAD_DIM), lambda b, h, i: (b * nt + i, group * hh + h))

    return pl.pallas_call(
        functools.partial(_hgrn_kernel, layer=layer),
        grid=(batch, hh, nt),
        in_specs=[col(4), col(5), col(6), col(7),
                  pl.BlockSpec((None, slots, HEAD_DIM), lambda b, h, i: (h, 0, 0)),
                  pl.BlockSpec(cum_w.shape, lambda b, h, i: (0, 0)),
                  pl.BlockSpec(masks.shape, lambda b, h, i: (0, 0, 0)),
                  pl.BlockSpec((None, 1, HEAD_DIM), lambda b, h, i: (h, 0, 0))],
        out_specs=pl.BlockSpec((tm, HEAD_DIM), lambda b, h, i: (b * nt + i, h)),
        out_shape=jax.ShapeDtypeStruct((batch * seq, EV_GROUP), BF16),
        scratch_shapes=[pltpu.VMEM((HEAD_DIM, HEAD_DIM), F32)],
        compiler_params=_params("parallel", "parallel", "arbitrary"),
        name="hgrn2",
    )(proj, proj, proj, proj, lbl, cum_w, masks, norm_g.astype(F32).reshape(hh, 1, HEAD_DIM))


def _t5_bucket(dist):
    exact = REL_BUCKETS // 2
    d = np.maximum(dist, 0)
    log_ratio = (np.log(np.maximum(d, 1).astype(np.float32) / np.float32(exact))
                 / np.float32(math.log(REL_MAX_DIST / exact)))
    large = np.minimum(exact + (log_ratio * np.float32(REL_BUCKETS - exact)).astype(np.int32), REL_BUCKETS - 1)
    return np.where(d < exact, d, large)


def _dsa_row_order(dil):
    rho = np.arange(DSA_BLOCK)
    run = PERM_RUN * dil
    c, j, a = rho // run, (rho % run) // PERM_RUN, rho % PERM_RUN
    return (PERM_BLOCK // dil) * j + (PERM_RESIDUES // dil) * a + c


def _dsa_bucket_table():
    blk = DSA_BLOCK
    tabs = []
    for window, dil in DSA_BRANCHES:
        order = _dsa_row_order(dil)
        qi = order[:, None]
        ki = np.concatenate([order, blk + order])[None, :]
        delta = qi + blk - ki
        n_back = window // dil
        valid = (delta >= 0) & (delta <= n_back)
        tabs.append(np.where(valid, _t5_bucket(delta * dil), -1).astype(np.int32))
    return jnp.asarray(np.stack(tabs))


def _dsa_kernel(rb_ref, q_ref, k_ref, v_ref, bucket_ref, o_ref,
                qn_ref, kp_ref, vp_ref, m_ref, den_ref, num_ref, bias_ref, lg_ref, *, seq, pad_blocks):
    blk = DSA_BLOCK
    nj = seq // PERM_BLOCK
    tiled = (nj, PERM_RESIDUES, PERM_RUN, HEAD_DIM)
    head = pl.program_id(0)
    qn_ref[...] = q_ref[...].reshape(tiled)
    kp_ref[pad_blocks:pad_blocks + nj] = k_ref[...].reshape(tiled)
    vp_ref[pad_blocks:pad_blocks + nj] = v_ref[...].reshape(tiled)

    @pl.when(pl.program_id(1) == 0)
    def _():
        zeros = jnp.zeros((pad_blocks,) + tiled[1:], F32)
        kp_ref[0:pad_blocks] = zeros
        vp_ref[0:pad_blocks] = zeros
        first_block_keys = lax.broadcasted_iota(jnp.int32, (blk, 2 * blk), 1) >= blk
        for br in range(len(DSA_BRANCHES)):
            bucket = bucket_ref[br]
            bias = jnp.full(bucket.shape, NEG_INF, F32)
            for b in range(REL_BUCKETS):
                bias = jnp.where(bucket == b, rb_ref[b, head], bias)
            bias_ref[2 * br] = bias
            bias_ref[2 * br + 1] = jnp.where(first_block_keys, bias, NEG_INF)

    scale = HEAD_DIM ** -0.5

    def gather(ref, j0, r, dil, spans):
        if dil == 1:
            return ref[pl.ds(j0, spans)].reshape(spans * blk, HEAD_DIM)
        pieces = [ref[pl.ds(j0 + s * dil, dil), r + dil * c]
                  for s in range(spans) for c in range(PERM_RESIDUES // dil)]
        return jnp.concatenate(pieces, axis=0).reshape(spans * blk, HEAD_DIM)

    def scatter(ref, j0, r, dil, val):
        if dil == 1:
            ref[j0] = val.reshape(tiled[1:])
            return
        run = PERM_RUN * dil
        for c in range(PERM_RESIDUES // dil):
            ref[pl.ds(j0, dil), r + dil * c] = val[c * run:(c + 1) * run, :].reshape(dil, PERM_RUN, HEAD_DIM)

    trips = nj // DSA_UNROLL
    assert nj % DSA_UNROLL == 0 and all(dil <= pad_blocks for _, dil in DSA_BRANCHES)

    def block_coords(br, it, u):
        dil = DSA_BRANCHES[br][1]
        nb = nj // dil
        idx = it * DSA_UNROLL + u
        return idx // nb, idx % nb, dil

    def logits_stage(br, it, slot):
        for u in range(DSA_UNROLL):
            r, n, dil = block_coords(br, it, u)
            qb = gather(qn_ref, n * dil, r, dil, 1)
            kb = gather(kp_ref, pad_blocks + (n - 1) * dil, r, dil, 2)
            bias = bias_ref[2 * br + jnp.where(n == 0, 1, 0)]
            lg_ref[slot, u] = _dot_nt(qb.astype(BF16), kb.astype(BF16)) * scale + bias

    def softmax_stage(br, it, slot):
        blocks = range(DSA_UNROLL)
        coords = [block_coords(br, it, u) for u in blocks]
        logits = [lg_ref[slot, u] for u in blocks]
        ms = [jnp.max(lg, axis=-1, keepdims=True) for lg in logits]
        ps = [jnp.exp(lg - m) for lg, m in zip(logits, ms)]
        dens = [jnp.sum(p, axis=-1, keepdims=True) for p in ps]
        nums = [_dot(p.astype(BF16), gather(vp_ref, pad_blocks + (n - 1) * dil, r, dil, 2).astype(BF16))
                for p, (r, n, dil) in zip(ps, coords)]
        for (r, n, dil), m, den, num in zip(coords, ms, dens, nums):
            scatter(m_ref.at[br], n * dil, r, dil, jnp.broadcast_to(m, (blk, HEAD_DIM)))
            scatter(den_ref.at[br], n * dil, r, dil, jnp.broadcast_to(den, (blk, HEAD_DIM)))
            scatter(num_ref.at[br], n * dil, r, dil, num)

    n_br = len(DSA_BRANCHES)
    logits_stage(0, 0, 0)
    for br in range(n_br):
        first_slot = (br * trips) % 2

        def overlapped(it, carry, br=br, first_slot=first_slot):
            slot = (first_slot + it) % 2
            softmax_stage(br, it, slot)
            logits_stage(br, it + 1, 1 - slot)
            return carry

        lax.fori_loop(0, trips - 1, overlapped, 0)
        last_slot = (first_slot + trips - 1) % 2
        softmax_stage(br, trips - 1, last_slot)
        if br + 1 < n_br:
            logits_stage(br + 1, 0, 1 - last_slot)

    def merge(j, carry):
        n_br = len(DSA_BRANCHES)
        ms = [m_ref[br, j] for br in range(n_br)]
        m_all = functools.reduce(jnp.maximum, ms)
        ws = [jnp.exp(mi - m_all) for mi in ms]
        den = functools.reduce(lambda x, y: x + y, [ws[br] * den_ref[br, j] for br in range(n_br)])
        num = functools.reduce(lambda x, y: x + y, [ws[br] * num_ref[br, j] for br in range(n_br)])
        rows = pl.ds(pl.multiple_of(j * blk, blk), blk)
        o_ref[rows, :] = (num / den).reshape(blk, HEAD_DIM).astype(o_ref.dtype)
        return carry

    lax.fori_loop(0, nj, merge, 0, unroll=2)


def _dilated_attention(qkv, rel_bias, *, batch, seq):
    hh = DSA_HEADS
    nj = seq // PERM_BLOCK
    pad_blocks = max(dil for _, dil in DSA_BRANCHES)
    bucket = _dsa_bucket_table()
    tiled = (PERM_RESIDUES, PERM_RUN, HEAD_DIM)

    def col(group):
        return pl.BlockSpec((seq, HEAD_DIM), lambda h, b: (b, group * hh + h))

    return pl.pallas_call(
        functools.partial(_dsa_kernel, seq=seq, pad_blocks=pad_blocks),
        grid=(hh, batch),
        in_specs=[pl.BlockSpec(memory_space=pltpu.SMEM),
                  col(0), col(1), col(2),
                  pl.BlockSpec(bucket.shape, lambda h, b: (0, 0, 0))],
        out_specs=pl.BlockSpec((seq, HEAD_DIM), lambda h, b: (b, h)),
        out_shape=jax.ShapeDtypeStruct((batch * seq, hh * HEAD_DIM), BF16),
        scratch_shapes=[pltpu.VMEM((nj,) + tiled, F32),
                        pltpu.VMEM((pad_blocks + nj,) + tiled, F32),
                        pltpu.VMEM((pad_blocks + nj,) + tiled, F32),
                        pltpu.VMEM((len(DSA_BRANCHES), nj) + tiled, F32),
                        pltpu.VMEM((len(DSA_BRANCHES), nj) + tiled, F32),
                        pltpu.VMEM((len(DSA_BRANCHES), nj) + tiled, F32),
                        pltpu.VMEM((2 * len(DSA_BRANCHES), DSA_BLOCK, 2 * DSA_BLOCK), F32),
                        pltpu.VMEM((2, DSA_UNROLL, DSA_BLOCK, 2 * DSA_BLOCK), F32)],
        compiler_params=_params("parallel", "arbitrary"),
        name="dilated_attention",
    )(rel_bias.astype(F32), qkv, qkv, qkv, bucket)


def _mem_kv_kernel(mem_ref, g_ref, w_ref, kg_ref, k_ref, v_ref):
    h = _rms_rows(mem_ref[...], g_ref[...]).astype(BF16)
    d = XA_HEADS * XA_HD
    kg = kg_ref[...]
    for hd in range(XA_HEADS):
        sl = slice(hd * XA_HD, (hd + 1) * XA_HD)
        k_ref[:, sl] = _rms_rows(_dot(h, w_ref[:, sl]), kg).astype(k_ref.dtype)
    v_ref[...] = _dot(h, w_ref[:, d:2 * d]).astype(v_ref.dtype)


def _mem_kv(mem2d, g, w_kv, layer, k_norm_g, *, batch):
    d = XA_HEADS * XA_HD
    return pl.pallas_call(
        _mem_kv_kernel,
        grid=(batch,),
        in_specs=[pl.BlockSpec((N_MEM, D_MODEL), lambda b: (b, 0)),
                  pl.BlockSpec((1, D_MODEL), lambda b: (0, 0)),
                  _layer_weight(w_kv, layer),
                  pl.BlockSpec((1, XA_HD), lambda b: (0, 0))],
        out_specs=[pl.BlockSpec((N_MEM, d), lambda b: (b, 0)),
                   pl.BlockSpec((N_MEM, d), lambda b: (b, 0))],
        out_shape=[jax.ShapeDtypeStruct((batch * N_MEM, d), BF16),
                   jax.ShapeDtypeStruct((batch * N_MEM, d), BF16)],
        compiler_params=_params("parallel"),
        name="mem_kv",
    )(mem2d, g.astype(F32).reshape(1, D_MODEL), w_kv, k_norm_g.astype(F32).reshape(1, XA_HD))


def _xattn_kernel(*refs, n_y, unpermute, oc):
    xin_ref = refs[0]
    y_refs = refs[1:1 + n_y]
    wmix_ref = refs[1 + n_y]
    rest = refs[2 + n_y:]
    if unpermute:
        p_ref, rest = rest[0], rest[1:]
    g_ref, wq_ref, qg_ref, mk_ref, mv_ref, wo_ref, o_ref, x_ref, att_ref = rest
    d = o_ref.shape[1]
    ys = [y_ref[...] for y_ref in y_refs]
    if unpermute:
        ys = [_permute_rows(p_ref[...], y) for y in ys]
    for n in range(d // oc):
        sl = slice(n * oc, (n + 1) * oc)
        acc = xin_ref[:, sl]
        off = 0
        for y in ys:
            acc = acc + _dot(y, wmix_ref[off:off + y.shape[1], sl])
            off += y.shape[1]
        x_ref[:, sl] = acc

    h = _rms_rows(x_ref[...], g_ref[...]).astype(BF16)
    qg = qg_ref[...]
    sls = [slice(hd * XA_HD, (hd + 1) * XA_HD) for hd in range(XA_HEADS)]
    qs = [_dot(h, wq_ref[:, sl]) for sl in sls]
    qs = [_rms_rows(q, qg).astype(BF16) for q in qs]
    logits = [_dot_nt(q, mk_ref[:, sl]) * (XA_HD ** -0.5) for q, sl in zip(qs, sls)]
    ps = [jnp.exp(lg - jnp.max(lg, axis=-1, keepdims=True)) for lg in logits]
    ps = [(p / jnp.sum(p, axis=-1, keepdims=True)).astype(BF16) for p in ps]
    for p, sl in zip(ps, sls):
        att_ref[:, sl] = _dot(p, mv_ref[:, sl]).astype(att_ref.dtype)
    for n in range(d // oc):
        sl = slice(n * oc, (n + 1) * oc)
        o_ref[:, sl] = x_ref[:, sl] + _dot(att_ref[...], wo_ref[:, sl])


def _mix_out_xattn(x2d, ys, w_mix, mix_layer, g, w_q, q_norm_g, mk, mv, w_o, layer, *, seq, tm,
                   unpermute=False, oc=256):
    t, d = x2d.shape
    nt = seq // tm
    n_y = len(ys)
    assert sum(y.shape[1] for y in ys) == w_mix.shape[1]
    const = lambda shape: pl.BlockSpec(shape, lambda i: (0, 0))
    in_specs = [pl.BlockSpec((tm, d), lambda i: (i, 0))]
    in_specs += [pl.BlockSpec((tm, y.shape[1]), lambda i: (i, 0)) for y in ys]
    in_specs.append(_layer_weight(w_mix, mix_layer))
    args = [x2d, *ys, w_mix]
    if unpermute:
        in_specs.append(const((PERM_BLOCK, PERM_BLOCK)))
        args.append(jnp.asarray(_block_permutation().T, BF16))
    in_specs += [const((1, d)), _layer_weight(w_q, layer), const((1, XA_HD)),
                 pl.BlockSpec((N_MEM, d), lambda i: (i // nt, 0)),
                 pl.BlockSpec((N_MEM, d), lambda i: (i // nt, 0)),
                 _layer_weight(w_o, layer)]
    args += [g.astype(F32).reshape(1, d), w_q, q_norm_g.astype(F32).reshape(1, XA_HD), mk, mv, w_o]
    return pl.pallas_call(
        functools.partial(_xattn_kernel, n_y=n_y, unpermute=unpermute, oc=oc),
        grid=(t // tm,),
        in_specs=in_specs,
        out_specs=pl.BlockSpec((tm, d), lambda i: (i, 0)),
        out_shape=jax.ShapeDtypeStruct((t, d), F32),
        scratch_shapes=[pltpu.VMEM((tm, d), F32), pltpu.VMEM((tm, d), BF16)],
        compiler_params=_params("parallel"),
        name="mix_out_xattn",
    )(*args)


def _ffn_kernel(x_ref, g_ref, win_ref, cw_ref, cb_ref, wout_ref, o_ref, carry_ref, act_ref, *, nt, fc, oc):
    tm, d = x_ref.shape
    assert CONV_W == 3

    @pl.when(pl.program_id(0) % nt == 0)
    def _():
        carry_ref[...] = jnp.zeros_like(carry_ref)

    h = _rms_rows(x_ref[...], g_ref[...]).astype(BF16)
    row = lax.broadcasted_iota(jnp.int32, (8, fc), 0)
    for c in range(D_FF // fc):
        sl = slice(c * fc, (c + 1) * fc)
        gate = _dot(h, win_ref[:, sl])
        up = _dot(h, win_ref[:, D_FF + c * fc:D_FF + (c + 1) * fc])
        prev = carry_ref[c]
        carry_ref[c] = gate[tm - 8:tm, :]
        w0, w1, w2, bias = cw_ref[0:1, sl], cw_ref[1:2, sl], cw_ref[2:3, sl], cb_ref[:, sl]
        conv = pltpu.roll(gate, 2, axis=0) * w0 + pltpu.roll(gate, 1, axis=0) * w1 + gate * w2 + bias
        top = gate[0:8, :]
        t1 = jnp.where(row == 0, prev[7:8, :], pltpu.roll(top, 1, axis=0))
        t2 = jnp.where(row == 0, prev[6:7, :], jnp.where(row == 1, prev[7:8, :], pltpu.roll(top, 2, axis=0)))
        conv = jnp.concatenate([t2 * w0 + t1 * w1 + top * w2 + bias, conv[8:, :]], axis=0)
        act_ref[:, sl] = (jax.nn.gelu(conv) * up).astype(BF16)
    for n in range(d // oc):
        sl = slice(n * oc, (n + 1) * oc)
        o_ref[:, sl] = x_ref[:, sl] + _dot(act_ref[...], wout_ref[:, sl])


def _ffn(x2d, g, w_in, conv_w, conv_b, w_out, layer, *, seq, tm, fc=256, oc=256):
    t, d = x2d.shape
    nt = seq // tm
    const = lambda shape: pl.BlockSpec(shape, lambda i: (0, 0))
    return pl.pallas_call(
        functools.partial(_ffn_kernel, nt=nt, fc=fc, oc=oc),
        grid=(t // tm,),
        in_specs=[pl.BlockSpec((tm, d), lambda i: (i, 0)),
                  const((1, d)), _layer_weight(w_in, layer), const((CONV_W, D_FF)), const((1, D_FF)),
                  _layer_weight(w_out, layer)],
        out_specs=pl.BlockSpec((tm, d), lambda i: (i, 0)),
        out_shape=jax.ShapeDtypeStruct((t, d), F32),
        scratch_shapes=[pltpu.VMEM((D_FF // fc, 8, fc), F32), pltpu.VMEM((tm, D_FF), BF16)],
        compiler_params=_params("arbitrary"),
        name="conv_ffn",
    )(x2d, g.astype(F32).reshape(1, d), w_in, conv_w.astype(F32), conv_b.astype(F32).reshape(1, D_FF), w_out)


def kernel(x, mem, mix_norm_g, ev_w_in, ev_ret_norm_g, ev_hg_norm_g, hg_lb_logits, ev_w_out, od_w_in, od_q_norm_g, od_k_norm_g, rel_bias, od_w_out, xa_norm_g, xa_mem_norm_g, xa_w_q, xa_w_kv, xa_q_norm_g, xa_k_norm_g, xa_w_o, ffn_norm_g, ffn_w_in, ffn_conv_w, ffn_conv_b, ffn_w_out):
    batch, seq, d = x.shape
    depth = mix_norm_g.shape[0]
    x2d = x.reshape(batch * seq, d)
    mem2d = mem.reshape(batch * mem.shape[1], d)
    tm = 512
    ev_w_in, ev_w_out, od_w_in, od_w_out, xa_w_q, xa_w_kv, xa_w_o, ffn_w_in, ffn_w_out = (
        w.astype(BF16) for w in (ev_w_in, ev_w_out, od_w_in, od_w_out, xa_w_q, xa_w_kv, xa_w_o, ffn_w_in, ffn_w_out))
    for l in range(depth):
        if l % 2 == 0:
            e = l // 2
            consts = _retention_consts(seq)
            proj = _rms_proj(x2d, mix_norm_g[l], ev_w_in, e, tm=tm,
                             rotary=(consts["cos"], consts["sin"], 2 * EV_GROUP))
            y_ret = _retention(proj, consts, ev_ret_norm_g[e], batch=batch, seq=seq, tm=4 * tm)
            y_hg = _hgrn(proj, hg_lb_logits, ev_hg_norm_g[e], layer=l, batch=batch, seq=seq, tm=4 * tm)
            ys, w_mix, mix_layer, permuted = [y_ret, y_hg], ev_w_out, e, False
        else:
            o = l // 2
            qk_gains = jnp.concatenate([jnp.tile(od_q_norm_g[o], DSA_HEADS), jnp.tile(od_k_norm_g[o], DSA_HEADS)])
            qkv = _rms_proj(x2d, mix_norm_g[l], od_w_in, o, tm=tm, permute=True, head_gains=qk_gains)
            att = _dilated_attention(qkv, rel_bias, batch=batch, seq=seq)
            ys, w_mix, mix_layer, permuted = [att], od_w_out, o, True
        mk, mv = _mem_kv(mem2d, xa_mem_norm_g[l], xa_w_kv, l, xa_k_norm_g[l], batch=batch)
        x2d = _mix_out_xattn(x2d, ys, w_mix, mix_layer, xa_norm_g[l], xa_w_q, xa_q_norm_g[l], mk, mv, xa_w_o, l,
                             seq=seq, tm=tm, unpermute=permuted)
        x2d = _ffn(x2d, ffn_norm_g[l], ffn_w_in, ffn_conv_w[l], ffn_conv_b[l], ffn_w_out, l, seq=seq, tm=tm)
    return x2d.reshape(batch, seq, d)
```
